```python
import math
import jax, jax.numpy as jnp
from jax import lax
import numpy as np

D_MODEL = 1024
BATCH = 8
SEQ = 2048
DEPTH = 4
DEC_BATCH = 32
DEC_SEQ = 4
PAST_LEN = 8192
PAGE_SIZE = 128

D_MIX = D_MODEL
CHUNK = 128
A_WIDTH = D_MIX // 2
A_GROUPS = 4
A_GW = A_WIDTH // A_GROUPS
B_WIDTH = D_MIX - A_WIDTH
HEAD_DIM = 64
N_HEADS = B_WIDTH // HEAD_DIM
N_KV_HEADS = 2
KV_GROUP = N_HEADS // N_KV_HEADS
IDX_HEADS = 8
IDX_DIM = 32
TOPK_MAX = 256
ROPE_THETA = 10000.0
Q_BLOCK = 128
D_FF = 4 * D_MODEL
D_PLE = 256
ALPHA = (2.0 * DEPTH) ** 0.25
BETA = (8.0 * DEPTH) ** -0.25
LN_EPS = 1e-5

SPLIT_SIZES = (A_WIDTH, A_WIDTH, N_HEADS * HEAD_DIM, N_KV_HEADS * HEAD_DIM,
               N_KV_HEADS * HEAD_DIM, IDX_HEADS * IDX_DIM, IDX_DIM, IDX_HEADS)
D_IN = sum(SPLIT_SIZES)
SPLIT_POINTS = tuple(sum(SPLIT_SIZES[:i + 1]) for i in range(len(SPLIT_SIZES) - 1))

N_PAGES = PAST_LEN // PAGE_SIZE
N_PHYS_PAGES = (5 * DEC_BATCH * N_PAGES) // 4
TOPK_PROMPT = min(TOPK_MAX, SEQ // 4)
TOPK_SAMPLE = min(TOPK_MAX, (PAST_LEN + DEC_SEQ) // 4)

kernel_name = "hymba_gmlp_dsa_deepnorm_decoder_step"


def layer_norm(x, g, b):
    xf = x.astype(jnp.float32)
    mu = jnp.mean(xf, axis=-1, keepdims=True)
    var = jnp.mean(jnp.square(xf - mu), axis=-1, keepdims=True)
    y = (xf - mu) * lax.rsqrt(var + LN_EPS) * g.astype(jnp.float32) + b.astype(jnp.float32)
    return y.astype(x.dtype)


def rope(x, pos):
    d = x.shape[-1]
    half = d // 2
    inv = ROPE_THETA ** (-jnp.arange(half, dtype=jnp.float32) / half)
    ang = pos.astype(jnp.float32)[:, None] * inv[None, :]
    cos = jnp.cos(ang)[None, :, None, :]
    sin = jnp.sin(ang)[None, :, None, :]
    xf = x.astype(jnp.float32)
    x1, x2 = xf[..., :half], xf[..., half:]
    return jnp.concatenate([x1 * cos - x2 * sin, x2 * cos + x1 * sin], axis=-1).astype(x.dtype)


def chunk_mix(u, vn, w_s, b_s):
    B, S = u.shape[:2]
    n = min(CHUNK, S)
    nc = S // n
    mask = jnp.tril(jnp.ones((n, n), dtype=bool))
    w = jnp.where(mask[None], w_s[:, :n, :n], 0)
    vc = vn.reshape(B, nc, n, A_GROUPS, A_GW)
    gate = jnp.einsum('gts,bcsgd->bctgd', w, vc) + b_s[:, :n].T[None, None, :, :, None]
    return u * gate.reshape(B, S, A_WIDTH)


def indexer_scores(qi, wi, ki):
    s = jnp.einsum('bthd,bld->bthl', qi, ki).astype(jnp.float32) * (IDX_DIM ** -0.5)
    w = wi.astype(jnp.float32) * (IDX_HEADS ** -0.5)
    return jnp.einsum('bthl,bth->btl', jax.nn.relu(s), w)


def attend_selected(q, k_sel, v_sel, valid):
    B, T = q.shape[:2]
    qg = q.reshape(B, T, N_KV_HEADS, KV_GROUP, HEAD_DIM)
    s = jnp.einsum('bthgd,btkhd->bthgk', qg, k_sel).astype(jnp.float32) * (HEAD_DIM ** -0.5)
    s = jnp.where(valid[:, :, None, None, :], s, -jnp.inf)
    p = jax.nn.softmax(s, axis=-1)
    o = jnp.einsum('bthgk,btkhd->bthgd', p.astype(v_sel.dtype), v_sel)
    return o.reshape(B, T, N_HEADS * HEAD_DIM)


def gather_rows(x, idx):
    return jax.vmap(lambda xb, ib: xb[ib])(x, idx)


def prompt_sparse_attn(q, k, v, qi, wi, ki):
    B, S = q.shape[:2]
    nb = S // Q_BLOCK

    def to_blocks(a):
        return a.reshape((B, nb, Q_BLOCK) + a.shape[2:]).swapaxes(0, 1)

    key_pos = jnp.arange(S)

    def block(args):
        q_b, qi_b, wi_b, t0 = args
        tpos = t0 + jnp.arange(Q_BLOCK)
        I = indexer_scores(qi_b, wi_b, ki)
        causal = key_pos[None, :] <= tpos[:, None]
        I = jnp.where(causal[None], I, -jnp.inf)
        _, idx = lax.top_k(I, TOPK_PROMPT)
        valid = idx <= tpos[None, :, None]
        return attend_selected(q_b, gather_rows(k, idx), gather_rows(v, idx), valid)

    starts = jnp.arange(nb, dtype=jnp.int32) * Q_BLOCK
    out = lax.map(block, (to_blocks(q), to_blocks(qi), to_blocks(wi), starts))
    return out.swapaxes(0, 1).reshape(B, S, N_HEADS * HEAD_DIM)


def sample_sparse_attn(q, k_new, v_new, qi, wi, ki_new, ck, cv, cki, page_table):
    DB, T = q.shape[:2]
    L = PAST_LEN + T
    ki_past = cki[page_table].reshape(DB, PAST_LEN, IDX_DIM)
    ki_all = jnp.concatenate([ki_past, ki_new], axis=1)
    I = indexer_scores(qi, wi, ki_all)
    tpos = PAST_LEN + jnp.arange(T)
    causal = jnp.arange(L)[None, :] <= tpos[:, None]
    I = jnp.where(causal[None], I, -jnp.inf)
    _, idx = lax.top_k(I, TOPK_SAMPLE)
    valid = idx <= tpos[None, :, None]
    is_past = idx < PAST_LEN
    pidx = jnp.minimum(idx, PAST_LEN - 1)
    page = jnp.take_along_axis(page_table, (pidx // PAGE_SIZE).reshape(DB, -1), axis=1)
    row = page.reshape(idx.shape) * PAGE_SIZE + pidx % PAGE_SIZE
    nidx = jnp.clip(idx - PAST_LEN, 0, T - 1)
    ck_flat = ck.reshape(-1, N_KV_HEADS, HEAD_DIM)
    cv_flat = cv.reshape(-1, N_KV_HEADS, HEAD_DIM)
    sel = is_past[..., None, None]
    k_sel = jnp.where(sel, ck_flat[row], gather_rows(k_new, nidx))
    v_sel = jnp.where(sel, cv_flat[row], gather_rows(v_new, nidx))
    return attend_selected(q, k_sel, v_sel, valid)


def trunk_layer(h, pe, pos, attn_fn, w_in, sgu_ln_g, sgu_ln_b, sgu_w, sgu_b, w_o,
                ln1_g, ln1_b, w_ff1, w_ff2, w_ple_gate, w_ple_proj, ln2_g, ln2_b):
    B, S = h.shape[:2]
    z = h @ w_in
    a_u, a_v, q, k, v, qi, ki, wi = jnp.split(z, SPLIT_POINTS, axis=-1)
    a_u = jax.nn.gelu(a_u, approximate=False)
    a_v = jax.nn.gelu(a_v, approximate=False)
    vn = layer_norm(a_v.reshape(B, S, A_GROUPS, A_GW), sgu_ln_g, sgu_ln_b)
    a_out = chunk_mix(a_u, vn, sgu_w, sgu_b)
    q = rope(q.reshape(B, S, N_HEADS, HEAD_DIM), pos)
    k = rope(k.reshape(B, S, N_KV_HEADS, HEAD_DIM), pos)
    v = v.reshape(B, S, N_KV_HEADS, HEAD_DIM)
    qi = rope(qi.reshape(B, S, IDX_HEADS, IDX_DIM), pos)
    ki = rope(ki[:, :, None, :], pos)[:, :, 0, :]
    b_out = attn_fn(q, k, v, qi, wi, ki)
    mix = jnp.concatenate([a_out, b_out], axis=-1) @ w_o
    h = layer_norm(ALPHA * h + mix, ln1_g, ln1_b)
    ff = jnp.square(jax.nn.relu(h @ w_ff1)) @ w_ff2
    ple = jax.nn.sigmoid(h @ w_ple_gate) * (pe @ w_ple_proj)
    h = layer_norm(ALPHA * h + ff + ple, ln2_g, ln2_b)
    n = min(CHUNK, S)
    chunk_rows = vn.reshape(B, S, A_WIDTH)[:, S - n:]
    return h, k, v, ki, chunk_rows


def setup_inputs(seed: int = 0) -> dict:
    key = jax.random.key(seed)
    ks = jax.random.split(key, 24)
    f32 = jnp.float32
    nrm = lambda k, shp: jax.random.normal(k, shp, dtype=f32)
    col_scale = jnp.concatenate([
        jnp.ones((2 * A_WIDTH + N_HEADS * HEAD_DIM + N_KV_HEADS * HEAD_DIM,), f32),
        jnp.full((N_KV_HEADS * HEAD_DIM,), BETA, f32),
        jnp.ones((IDX_HEADS * IDX_DIM + IDX_DIM + IDX_HEADS,), f32)])
    perm = jax.random.permutation(ks[23], N_PHYS_PAGES)[:DEC_BATCH * N_PAGES]
    return {
        "x_prompt": nrm(ks[0], (BATCH, SEQ, D_MODEL)),
        "x_sample": nrm(ks[1], (DEC_BATCH, DEC_SEQ, D_MODEL)),
        "p_prompt": nrm(ks[2], (DEPTH, BATCH, SEQ, D_PLE)),
        "p_sample": nrm(ks[3], (DEPTH, DEC_BATCH, DEC_SEQ, D_PLE)),
        "cache_k": nrm(ks[4], (DEPTH, N_PHYS_PAGES, PAGE_SIZE, N_KV_HEADS, HEAD_DIM)),
        "cache_v": nrm(ks[5], (DEPTH, N_PHYS_PAGES, PAGE_SIZE, N_KV_HEADS, HEAD_DIM)),
        "cache_kidx": nrm(ks[6], (DEPTH, N_PHYS_PAGES, PAGE_SIZE, IDX_DIM)),
        "page_table": perm.reshape(DEC_BATCH, N_PAGES).astype(jnp.int32),
        "w_in": nrm(ks[7], (DEPTH, D_MODEL, D_IN)) * (D_MODEL ** -0.5) * col_scale,
        "sgu_ln_g": 1.0 + 0.1 * nrm(ks[8], (DEPTH, A_GROUPS, A_GW)),
        "sgu_ln_b": 0.1 * nrm(ks[9], (DEPTH, A_GROUPS, A_GW)),
        "sgu_w": nrm(ks[10], (DEPTH, A_GROUPS, CHUNK, CHUNK)) * (CHUNK ** -0.5),
        "sgu_b": 1.0 + 0.1 * nrm(ks[11], (DEPTH, A_GROUPS, CHUNK)),
        "w_o": nrm(ks[12], (DEPTH, D_MIX, D_MODEL)) * (D_MIX ** -0.5) * BETA,
        "ln1_g": 1.0 + 0.1 * nrm(ks[13], (DEPTH, D_MODEL)),
        "ln1_b": 0.1 * nrm(ks[14], (DEPTH, D_MODEL)),
        "w_ff1": nrm(ks[15], (DEPTH, D_MODEL, D_FF)) * (D_MODEL ** -0.5),
        "w_ff2": nrm(ks[16], (DEPTH, D_FF, D_MODEL)) * (D_FF ** -0.5) * BETA,
        "w_ple_gate": nrm(ks[17], (DEPTH, D_MODEL, D_MODEL)) * (D_MODEL ** -0.5),
        "w_ple_proj": nrm(ks[18], (DEPTH, D_PLE, D_MODEL)) * (D_PLE ** -0.5) * BETA,
        "ln2_g": 1.0 + 0.1 * nrm(ks[19], (DEPTH, D_MODEL)),
        "ln2_b": 0.1 * nrm(ks[20], (DEPTH, D_MODEL)),
    }


def reference(x_prompt, x_sample, p_prompt, p_sample, cache_k, cache_v, cache_kidx, page_table,
              w_in, sgu_ln_g, sgu_ln_b, sgu_w, sgu_b, w_o, ln1_g, ln1_b, w_ff1, w_ff2,
              w_ple_gate, w_ple_proj, ln2_g, ln2_b):
    pos_prompt = jnp.arange(x_prompt.shape[1], dtype=jnp.int32)
    pos_sample = PAST_LEN + jnp.arange(x_sample.shape[1], dtype=jnp.int32)
    hp, hs = x_prompt, x_sample
    kp, vp, kip, cp = [], [], [], []
    ksm, vsm, kism, csm = [], [], [], []
    for i in range(DEPTH):
        lw = (w_in[i], sgu_ln_g[i], sgu_ln_b[i], sgu_w[i], sgu_b[i], w_o[i], ln1_g[i], ln1_b[i],
              w_ff1[i], w_ff2[i], w_ple_gate[i], w_ple_proj[i], ln2_g[i], ln2_b[i])
        hp, k_, v_, ki_, c_ = trunk_layer(hp, p_prompt[i], pos_prompt, prompt_sparse_attn, *lw)
        kp.append(k_); vp.append(v_); kip.append(ki_); cp.append(c_)
        attn_s = (lambda ck, cv, cki: (lambda q, k, v, qi, wi, ki: sample_sparse_attn(
            q, k, v, qi, wi, ki, ck, cv, cki, page_table)))(cache_k[i], cache_v[i], cache_kidx[i])
        hs, k_, v_, ki_, c_ = trunk_layer(hs, p_sample[i], pos_sample, attn_s, *lw)
        ksm.append(k_); vsm.append(v_); kism.append(ki_); csm.append(c_)
    return (hp, hs, jnp.stack(kp), jnp.stack(vp), jnp.stack(kip), jnp.stack(cp),
            jnp.stack(ksm), jnp.stack(vsm), jnp.stack(kism), jnp.stack(csm))
```

```python
import functools

import jax
import jax.numpy as jnp
from jax import lax
from jax.experimental import pallas as pl
from jax.experimental.pallas import tpu as pltpu

F32 = jnp.float32
BF16 = jnp.bfloat16
I32 = jnp.int32

D_MODEL = 1024
CHUNK = 128
A_WIDTH = 512
A_GROUPS = 4
A_GW = A_WIDTH // A_GROUPS
HEAD_DIM = 64
N_HEADS = 8
N_KV_HEADS = 2
KV_GROUP = N_HEADS // N_KV_HEADS
IDX_HEADS = 8
IDX_DIM = 32
TOPK_MAX = 256
ROPE_THETA = 10000.0
D_FF = 4 * D_MODEL
D_PLE = 256
LN_EPS = 1e-5

LANES = 128
VMEM_LIMIT = 56 * 1024 * 1024

INT_MIN = -(2 ** 31)
NEG = -1e30

C_AU, C_AV, C_Q, C_K, C_V, C_QI, C_KI, C_WI = 0, 512, 1024, 1536, 1664, 1792, 2048, 2176
C_QS, C_KS, C_QIS, C_KIS, C_END = 2304, 2816, 2944, 3200, 3328

NT_DIMS = (((1,), (1,)), ((), ()))


def _dot(a, b):
    return jnp.dot(a, b, preferred_element_type=F32)


def _dot_nt(a, b):
    return lax.dot_general(a, b, NT_DIMS, preferred_element_type=F32)


def _layer_norm(x, g, b):
    mu = jnp.mean(x, axis=-1, keepdims=True)
    xc = x - mu
    var = jnp.mean(xc * xc, axis=-1, keepdims=True)
    return xc * lax.rsqrt(var + LN_EPS) * g + b


def _gelu(x):
    return 0.5 * x * (1.0 + lax.erf(x * (2.0 ** -0.5)))


def _sortable(x):
    bits = pltpu.bitcast(x, I32)
    return bits ^ ((bits >> 31) & 0x7FFFFFFF)


def _proj_kernel(h_ref, w_ref, mm_ref, mb_ref, lng_ref, lnb_ref, cq_ref, sq_ref, ci_ref, si_ref,
                 aout_ref, vn_ref, q_ref, k_ref, v_ref, qi_ref, ki_ref, wi_ref):
    rows = h_ref.shape[0]
    hb = h_ref[...].astype(BF16)

    def proj(lo, width=LANES):
        return _dot(hb, w_ref[:, lo:lo + width])

    for g in range(A_GROUPS):
        lo = g * A_GW
        au = _gelu(proj(C_AU + lo))
        av = _gelu(proj(C_AV + lo))
        vn = _layer_norm(av, lng_ref[:, lo:lo + A_GW], lnb_ref[:, lo:lo + A_GW])
        vn_ref[:, lo:lo + A_GW] = vn
        vnb = vn.astype(BF16)
        for c in range(rows // CHUNK):
            r0 = c * CHUNK
            gate = _dot(mm_ref[g], vnb[r0:r0 + CHUNK, :]) + mb_ref[:, lo:lo + A_GW]
            aout_ref[r0:r0 + CHUNK, lo:lo + A_GW] = (au[r0:r0 + CHUNK, :] * gate).astype(BF16)

    cq, sq = cq_ref[...], sq_ref[...]
    ci, si = ci_ref[...], si_ref[...]
    for j in range(N_HEADS * HEAD_DIM // LANES):
        z = proj(C_Q + j * LANES) * cq + proj(C_QS + j * LANES) * sq
        q_ref[:, j * LANES:(j + 1) * LANES] = (z * (HEAD_DIM ** -0.5)).astype(BF16)
    k_ref[...] = proj(C_K) * cq + proj(C_KS) * sq
    v_ref[...] = proj(C_V)
    for j in range(IDX_HEADS * IDX_DIM // LANES):
        z = proj(C_QI + j * LANES) * ci + proj(C_QIS + j * LANES) * si
        qi_ref[:, j * LANES:(j + 1) * LANES] = z.astype(BF16)
    ki_ref[...] = proj(C_KI) * ci + proj(C_KIS) * si
    wi_ref[...] = proj(C_WI)


def _const_spec(shape):
    nd = len(shape)
    return pl.BlockSpec(shape, lambda *_: (0,) * nd, pipeline_mode=pl.Buffered(1))


def _proj_call(h2d, w_ext, mm, mb, lng, lnb, tabs, seq_rows, row_tile):
    n = h2d.shape[0]
    nb = n // seq_rows
    ns = seq_rows // row_tile
    row_map = lambda s, b: (b * ns + s, 0)
    tab_map = lambda s, b: (s, 0)
    out_widths = [(A_WIDTH, BF16), (A_WIDTH, F32), (N_HEADS * HEAD_DIM, BF16), (LANES, F32), (LANES, F32),
                  (IDX_HEADS * IDX_DIM, BF16), (LANES, F32), (LANES, F32)]
    return pl.pallas_call(
        _proj_kernel,
        grid=(ns, nb),
        in_specs=[pl.BlockSpec((row_tile, D_MODEL), row_map),
                  _const_spec(w_ext.shape), _const_spec(mm.shape), _const_spec(mb.shape),
                  _const_spec(lng.shape), _const_spec(lnb.shape)]
                 + [pl.BlockSpec((row_tile, LANES), tab_map)] * 4,
        out_specs=[pl.BlockSpec((row_tile, w), row_map) for w, _ in out_widths],
        out_shape=[jax.ShapeDtypeStruct((n, w), dt) for w, dt in out_widths],
        compiler_params=pltpu.CompilerParams(dimension_semantics=("arbitrary", "arbitrary"),
                                             vmem_limit_bytes=VMEM_LIMIT),
        name="proj",
    )(h2d, w_ext, mm, mb, lng, lnb, *tabs)


def _head_masked_queries(qi_ref):
    lane_head = lax.broadcasted_iota(I32, (CHUNK, LANES), 1) >> 5
    heads_per_col = LANES // IDX_DIM
    out = []
    for h in range(IDX_HEADS):
        col = qi_ref[:, (h // heads_per_col) * LANES:(h // heads_per_col + 1) * LANES]
        out.append(jnp.where(lane_head == (h % heads_per_col), col, jnp.zeros_like(col)))
    return out


def _index_scores(qh, wsc, kc):
    acc = None
    for h in range(IDX_HEADS):
        s = _dot_nt(qh[h], kc)
        t = jnp.maximum(s, 0.0) * wsc[:, h:h + 1]
        acc = t if acc is None else acc + t
    return acc


def _count_ge(keys_ref, nch, cw, cand, strict=False):
    def body(c, acc):
        off = pl.multiple_of(c * cw, cw)
        for j in range(cw // LANES):
            kk = keys_ref[:, pl.ds(off + j * LANES, LANES)]
            hit = (kk > cand) if strict else (kk >= cand)
            acc = acc + jnp.where(hit, 1.0, 0.0)
        return acc
    acc = lax.fori_loop(0, nch, body, jnp.zeros((CHUNK, LANES), F32))
    cnt = jnp.sum(acc, axis=-1, keepdims=True)
    return jnp.broadcast_to(cnt, (CHUNK, LANES))


def _select_threshold(keys_ref, nch, cw, topk):
    def bit_body(i, prefix):
        cand = prefix + lax.shift_left(jnp.int32(1), 31 - i)
        cnt = _count_ge(keys_ref, nch, cw, cand)
        return jnp.where(cnt >= topk, cand, prefix)
    prefix = lax.fori_loop(0, 32, bit_body, jnp.full((CHUNK, LANES), INT_MIN, I32))
    thr = jnp.maximum(prefix, INT_MIN + 1)
    cnt_ge = _count_ge(keys_ref, nch, cw, thr)

    @pl.when(jnp.max(cnt_ge) > topk)
    def _():
        room = topk - _count_ge(keys_ref, nch, cw, thr, strict=True)
        r_i = lax.broadcasted_iota(I32, (LANES, LANES), 0)
        c_i = lax.broadcasted_iota(I32, (LANES, LANES), 1)
        upper = jnp.where(r_i <= c_i, 1.0, 0.0).astype(BF16)

        def body(c, seen):
            off = pl.multiple_of(c * cw, cw)
            for j in range(cw // LANES):
                kk = keys_ref[:, pl.ds(off + j * LANES, LANES)]
                tie = kk == thr
                rank = seen + _dot(jnp.where(tie, 1.0, 0.0).astype(BF16), upper)
                keys_ref[:, pl.ds(off + j * LANES, LANES)] = jnp.where(tie & (rank > room), thr - 1, kk)
                seen = jnp.broadcast_to(rank[:, LANES - 1:LANES], (CHUNK, LANES))
            return seen
        lax.fori_loop(0, nch, body, jnp.zeros((CHUNK, LANES), F32))

    return thr


def _prompt_attn_kernel(qi_ref, wi_ref, q_ref, ki_ref, k_ref, v_ref, o_ref,
                        keys_ref, m_ref, l_ref, acc_ref, *, topk, cw):
    qb = pl.program_id(1)
    nch = (qb * CHUNK + CHUNK + cw - 1) // cw
    row_pos = qb * CHUNK + lax.broadcasted_iota(I32, (CHUNK, cw), 0)
    col_iota = lax.broadcasted_iota(I32, (CHUNK, cw), 1)

    qh = _head_masked_queries(qi_ref)
    wsc = wi_ref[...] * ((IDX_DIM ** -0.5) * (IDX_HEADS ** -0.5))

    def score_body(c, carry):
        off = pl.multiple_of(c * cw, cw)
        kc = ki_ref[pl.ds(off, cw), :].astype(BF16)
        key = _sortable(_index_scores(qh, wsc, kc))
        keys_ref[:, pl.ds(off, cw)] = jnp.where(col_iota + off <= row_pos, key, INT_MIN)
        return carry
    lax.fori_loop(0, nch, score_body, 0)

    thr = _select_threshold(keys_ref, nch, cw, topk)

    m_ref[...] = jnp.full(m_ref.shape, NEG, F32)
    l_ref[...] = jnp.zeros(l_ref.shape, F32)
    acc_ref[...] = jnp.zeros(acc_ref.shape, F32)
    lo_kv = lax.broadcasted_iota(I32, (cw, LANES), 1) < HEAD_DIM
    lo_q = lax.broadcasted_iota(I32, (CHUNK, LANES), 1) < HEAD_DIM
    thr_w = jnp.concatenate([thr] * (cw // LANES), axis=1)

    def attn_body(c, carry):
        off = pl.multiple_of(c * cw, cw)
        sel = keys_ref[:, pl.ds(off, cw)] >= thr_w
        kf = k_ref[pl.ds(off, cw), :]
        vf = v_ref[pl.ds(off, cw), :]
        kr = pltpu.roll(kf, HEAD_DIM, 1)
        vr = pltpu.roll(vf, HEAD_DIM, 1)
        zero = jnp.zeros_like(kf)
        k_at = [[jnp.where(lo_kv, kf, zero).astype(BF16), jnp.where(lo_kv, zero, kr).astype(BF16)],
                [jnp.where(lo_kv, kr, zero).astype(BF16), jnp.where(lo_kv, zero, kf).astype(BF16)]]
        vb, vrb = vf.astype(BF16), vr.astype(BF16)
        for p in range(N_HEADS // 2):
            g = (2 * p) // KV_GROUP
            qcol = q_ref[:, p * LANES:(p + 1) * LANES]
            alphas, pvs = [], []
            for e in range(2):
                h = 2 * p + e
                s = jnp.where(sel, _dot_nt(qcol, k_at[g][e]), NEG)
                m_old = m_ref[h]
                m_new = jnp.maximum(m_old, jnp.max(s, axis=-1, keepdims=True))
                alpha = jnp.exp(m_old - m_new)
                pr = jnp.exp(s - jnp.concatenate([m_new] * (cw // LANES), axis=1))
                l_ref[h] = alpha * l_ref[h] + jnp.sum(pr, axis=-1, keepdims=True)
                m_ref[h] = m_new
                pvs.append(_dot(pr.astype(BF16), vb if e == g else vrb))
                alphas.append(alpha)
            acc_ref[p] = (jnp.where(lo_q, alphas[0], alphas[1]) * acc_ref[p]
                          + jnp.where(lo_q, pvs[0], pvs[1]))
        return carry
    lax.fori_loop(0, nch, attn_body, 0)

    for p in range(N_HEADS // 2):
        denom = jnp.where(lo_q, l_ref[2 * p], l_ref[2 * p + 1])
        o_ref[:, p * LANES:(p + 1) * LANES] = (acc_ref[p] / denom).astype(BF16)


def _prompt_attn_call(qi, wi, q, ki, k, v, nb, seq, topk, cw):
    nq = seq // CHUNK
    blk = lambda b, i: (b * nq + i, 0)
    full = lambda b, i: (b, 0)
    kern = functools.partial(_prompt_attn_kernel, topk=topk, cw=cw)
    return pl.pallas_call(
        kern,
        grid=(nb, nq),
        in_specs=[pl.BlockSpec((CHUNK, IDX_HEADS * IDX_DIM), blk),
                  pl.BlockSpec((CHUNK, LANES), blk),
                  pl.BlockSpec((CHUNK, N_HEADS * HEAD_DIM), blk),
                  pl.BlockSpec((seq, LANES), full),
                  pl.BlockSpec((seq, LANES), full),
                  pl.BlockSpec((seq, LANES), full)],
        out_specs=pl.BlockSpec((CHUNK, N_HEADS * HEAD_DIM), blk),
        out_shape=jax.ShapeDtypeStruct((nb * seq, N_HEADS * HEAD_DIM), BF16),
        scratch_shapes=[pltpu.VMEM((CHUNK, seq), I32),
                        pltpu.VMEM((N_HEADS, CHUNK, LANES), F32),
                        pltpu.VMEM((N_HEADS, CHUNK, LANES), F32),
                        pltpu.VMEM((N_HEADS // 2, CHUNK, LANES), F32)],
        compiler_params=pltpu.CompilerParams(dimension_semantics=("arbitrary", "arbitrary"),
                                             vmem_limit_bytes=VMEM_LIMIT),
        name="prompt_attn",
    )(qi, wi, q, ki, k, v)


def _mlp_kernel(h_ref, a_ref, b_ref, pe_ref, wo_ref, g1_ref, b1_ref, wf1_ref, wf2_ref, wg_ref, wp_ref,
                g2_ref, b2_ref, out_ref, *, alpha, ff_chunk):
    mix = _dot(a_ref[...], wo_ref[0:A_WIDTH, :]) + _dot(b_ref[...], wo_ref[A_WIDTH:2 * A_WIDTH, :])
    h1 = _layer_norm(alpha * h_ref[...] + mix, g1_ref[...], b1_ref[...])
    h1b = h1.astype(BF16)
    ff = None
    for c in range(D_FF // ff_chunk):
        hid = jnp.maximum(_dot(h1b, wf1_ref[:, c * ff_chunk:(c + 1) * ff_chunk]), 0.0)
        part = _dot((hid * hid).astype(BF16), wf2_ref[c * ff_chunk:(c + 1) * ff_chunk, :])
        ff = part if ff is None else ff + part
    ple = jax.nn.sigmoid(_dot(h1b, wg_ref[...])) * _dot(pe_ref[...].astype(BF16), wp_ref[...])
    out_ref[...] = _layer_norm(alpha * h1 + ff + ple, g2_ref[...], b2_ref[...])


def _mlp_call(h2d, a_out, b_out, pe, wo, g1, b1, wf1, wf2, wg, wp, g2, b2, alpha, row_tile):
    n = h2d.shape[0]
    row = lambda i: (i, 0)
    kern = functools.partial(_mlp_kernel, alpha=alpha, ff_chunk=1024)
    consts = [wo, g1, b1, wf1, wf2, wg, wp, g2, b2]
    return pl.pallas_call(
        kern,
        grid=(n // row_tile,),
        in_specs=[pl.BlockSpec((row_tile, D_MODEL), row),
                  pl.BlockSpec((row_tile, A_WIDTH), row),
                  pl.BlockSpec((row_tile, A_WIDTH), row),
                  pl.BlockSpec((row_tile, D_PLE), row)] + [_const_spec(c.shape) for c in consts],
        out_specs=pl.BlockSpec((row_tile, D_MODEL), row),
        out_shape=jax.ShapeDtypeStruct((n, D_MODEL), F32),
        compiler_params=pltpu.CompilerParams(dimension_semantics=("arbitrary",),
                                             vmem_limit_bytes=VMEM_LIMIT),
        name="mlp",
    )(h2d, a_out, b_out, pe, *consts)


def _sample_idx_kernel(pt_ref, qs_ref, w_ref, *refs, pages_per_step):
    page_refs, out_ref = refs[:pages_per_step], refs[pages_per_step]
    qs = qs_ref[0]
    w = w_ref[0]
    nt = out_ref.shape[1]
    for j in range(pages_per_step):
        kp = page_refs[j][...].astype(BF16)
        t = jnp.maximum(_dot(qs, kp), 0.0) * w
        for q in range(nt):
            sc = jnp.sum(t[q * IDX_HEADS:(q + 1) * IDX_HEADS, :], axis=0, keepdims=True)
            out_ref[0, q:q + 1, j * LANES:(j + 1) * LANES] = _sortable(sc)


def _sample_idx_call(page_table, qs, wrep, cache_kidx, layer, pps):
    db, npg = page_table.shape
    nt = qs.shape[1] // IDX_HEADS
    page = cache_kidx.shape[3]
    kern = functools.partial(_sample_idx_kernel, pages_per_step=pps)

    def page_spec(j):
        return pl.BlockSpec((None, None, IDX_DIM, page), lambda b, s, pt: (layer, pt[b, s * pps + j], 0, 0))

    grid_spec = pltpu.PrefetchScalarGridSpec(
        num_scalar_prefetch=1,
        grid=(db, npg // pps),
        in_specs=[pl.BlockSpec((1,) + qs.shape[1:], lambda b, s, pt: (b, 0, 0)),
                  pl.BlockSpec((1,) + wrep.shape[1:], lambda b, s, pt: (b, 0, 0))]
                 + [page_spec(j) for j in range(pps)],
        out_specs=pl.BlockSpec((1, nt, pps * page), lambda b, s, pt: (b, 0, s)),
    )
    return pl.pallas_call(
        kern, grid_spec=grid_spec,
        out_shape=jax.ShapeDtypeStruct((db, nt, npg * page), I32),
        compiler_params=pltpu.CompilerParams(dimension_semantics=("arbitrary", "arbitrary")),
        name="sample_idx",
    )(page_table, qs, wrep, *([cache_kidx] * pps))


def _sample_select_kernel(kp_ref, qi_ref, wi_ref, ki_ref, bp_ref, bn_ref, keys_ref, *, topk, cw, nt):
    past = kp_ref.shape[1]
    width = keys_ref.shape[1]
    nch = width // cw
    keys_ref[:, 0:past] = kp_ref[...]
    qh = _head_masked_queries(qi_ref)
    wsc = wi_ref[...] * ((IDX_DIM ** -0.5) * (IDX_HEADS ** -0.5))
    key = _sortable(_index_scores(qh, wsc, ki_ref[...].astype(BF16)))
    r_i = lax.broadcasted_iota(I32, (CHUNK, LANES), 0)
    c_i = lax.broadcasted_iota(I32, (CHUNK, LANES), 1)
    visible = ((r_i // nt) == (c_i // nt)) & ((c_i % nt) <= (r_i % nt))
    keys_ref[:, past:past + LANES] = jnp.where(visible, key, INT_MIN)
    if width > past + LANES:
        keys_ref[:, past + LANES:width] = jnp.full((CHUNK, width - past - LANES), INT_MIN, I32)

    thr = _select_threshold(keys_ref, nch, cw, topk)

    for c in range(past // LANES):
        bp_ref[:, c * LANES:(c + 1) * LANES] = jnp.where(
            keys_ref[:, c * LANES:(c + 1) * LANES] >= thr, 0.0, NEG)
    bn_ref[...] = jnp.where(keys_ref[:, past:past + LANES] >= thr, 0.0, NEG)


def _sample_select_call(keys_past, qi, wi, ki4, topk, nt, cw):
    rows, past = keys_past.shape
    width = -(-(past + LANES) // cw) * cw
    kern = functools.partial(_sample_select_kernel, topk=topk, cw=cw, nt=nt)
    return pl.pallas_call(
        kern,
        out_shape=[jax.ShapeDtypeStruct((rows, past), F32), jax.ShapeDtypeStruct((rows, LANES), F32)],
        scratch_shapes=[pltpu.VMEM((rows, width), I32)],
        compiler_params=pltpu.CompilerParams(vmem_limit_bytes=VMEM_LIMIT),
        name="sample_select",
    )(keys_past, qi, wi, ki4)


def _sample_attn_kernel(pt_ref, qz_ref, bp_ref, bn_ref, kn_ref, vn_ref, *refs, pages_per_step, nt):
    pps = pages_per_step
    k_refs, v_refs = refs[:pps], refs[pps:2 * pps]
    o_ref, m_ref, l_ref, acc_ref = refs[2 * pps:]
    step = pl.program_id(1)
    qz = qz_ref[0]
    rows = qz.shape[0]
    hpq = rows // nt

    @pl.when(step == 0)
    def _():
        m_ref[...] = jnp.full(m_ref.shape, NEG, F32)
        l_ref[...] = jnp.zeros(l_ref.shape, F32)
        acc_ref[...] = jnp.zeros(acc_ref.shape, F32)

    def update(s, pv_fn, bias):
        width = s.shape[1]
        s = jnp.concatenate(
            [s[q * hpq:(q + 1) * hpq, :] + bias[q:q + 1, :] for q in range(nt)], axis=0)
        m_old = m_ref[...]
        m_new = jnp.maximum(m_old, jnp.max(s, axis=-1, keepdims=True))
        alpha = jnp.exp(m_old - m_new)
        pr = jnp.exp(s - jnp.concatenate([m_new] * (width // LANES), axis=1))
        l_ref[...] = alpha * l_ref[...] + jnp.sum(pr, axis=-1, keepdims=True)
        m_ref[...] = m_new
        acc_ref[...] = alpha * acc_ref[...] + pv_fn(pr.astype(BF16))

    kb = jnp.concatenate([r[...].astype(BF16) for r in k_refs], axis=1)
    vb = jnp.concatenate([r[...].astype(BF16) for r in v_refs], axis=1)
    update(_dot(qz, kb), lambda pr: _dot_nt(pr, vb), bp_ref[0])

    @pl.when(step == pl.num_programs(1) - 1)
    def _():
        vn = vn_ref[...].astype(BF16)
        update(_dot_nt(qz, kn_ref[...].astype(BF16)), lambda pr: _dot(pr, vn), bn_ref[0])
        o_ref[0] = acc_ref[...] / l_ref[...]


def _sample_attn_call(page_table, qz, bias_past, bias_new, k_new, v_new, cache_k, cache_v, layer, pps, nt):
    db, npg = page_table.shape
    page = cache_k.shape[3]
    rows = qz.shape[1]
    kern = functools.partial(_sample_attn_kernel, pages_per_step=pps, nt=nt)

    def page_spec(j):
        return pl.BlockSpec((None, None, LANES, page), lambda b, s, pt: (layer, pt[b, s * pps + j], 0, 0))

    grid_spec = pltpu.PrefetchScalarGridSpec(
        num_scalar_prefetch=1,
        grid=(db, npg // pps),
        in_specs=[pl.BlockSpec((1, rows, LANES), lambda b, s, pt: (b, 0, 0)),
                  pl.BlockSpec((1, nt, pps * page), lambda b, s, pt: (b, 0, s)),
                  pl.BlockSpec((1, nt, LANES), lambda b, s, pt: (b, 0, 0)),
                  pl.BlockSpec(k_new.shape, lambda b, s, pt: (0, 0)),
                  pl.BlockSpec(v_new.shape, lambda b, s, pt: (0, 0))]
                 + [page_spec(j) for j in range(pps)] * 2,
        out_specs=pl.BlockSpec((1, rows, LANES), lambda b, s, pt: (b, 0, 0)),
        scratch_shapes=[pltpu.VMEM((rows, LANES), F32)] * 3,
    )
    return pl.pallas_call(
        kern, grid_spec=grid_spec,
        out_shape=jax.ShapeDtypeStruct((db, rows, LANES), F32),
        compiler_params=pltpu.CompilerParams(dimension_semantics=("arbitrary", "arbitrary")),
        name="sample_attn",
    )(page_table, qz, bias_past, bias_new, k_new, v_new, *([cache_k] * pps), *([cache_v] * pps))


def _swap_halves(w, head_dim):
    lead = w.shape[:-1]
    nh = w.shape[-1] // head_dim
    return w.reshape(lead + (nh, 2, head_dim // 2))[..., ::-1, :].reshape(lead + (nh * head_dim,))


def _extended_w_in(w_in):
    sizes = (A_WIDTH, A_WIDTH, N_HEADS * HEAD_DIM, N_KV_HEADS * HEAD_DIM, N_KV_HEADS * HEAD_DIM,
             IDX_HEADS * IDX_DIM, IDX_DIM, IDX_HEADS)
    pts = [sum(sizes[:i + 1]) for i in range(len(sizes) - 1)]
    au, av, q, k, v, qi, ki, wi = jnp.split(w_in, pts, axis=-1)
    rep = LANES // IDX_DIM
    ki4 = jnp.tile(ki, (1, 1, rep))
    ki4s = jnp.tile(_swap_halves(ki, IDX_DIM), (1, 1, rep))
    wi_pad = jnp.pad(wi, ((0, 0), (0, 0), (0, LANES - IDX_HEADS)))
    ext = jnp.concatenate([au, av, q, k, v, qi, ki4, wi_pad,
                           _swap_halves(q, HEAD_DIM), _swap_halves(k, HEAD_DIM),
                           _swap_halves(qi, IDX_DIM), ki4s], axis=-1)
    assert ext.shape[-1] == C_END
    return ext.astype(BF16)


def _rope_tables(pos, head_dim):
    half = head_dim // 2
    inv = ROPE_THETA ** (-jnp.arange(half, dtype=F32) / half)
    ang = pos.astype(F32)[:, None] * inv[None, :]
    cos, sin = jnp.cos(ang), jnp.sin(ang)
    reps = LANES // head_dim
    return (jnp.tile(jnp.concatenate([cos, cos], axis=-1), (1, reps)),
            jnp.tile(jnp.concatenate([-sin, sin], axis=-1), (1, reps)))


def _pick(n, prefs):
    for p in prefs:
        if n % p == 0:
            return p
    return n


def kernel(x_prompt, x_sample, p_prompt, p_sample, cache_k, cache_v, cache_kidx, page_table, w_in, sgu_ln_g,
           sgu_ln_b, sgu_w, sgu_b, w_o, ln1_g, ln1_b, w_ff1, w_ff2, w_ple_gate, w_ple_proj, ln2_g, ln2_b):
    nb, seq, _ = x_prompt.shape
    db, nt, _ = x_sample.shape
    depth = w_in.shape[0]
    npg = page_table.shape[1]
    page = cache_k.shape[2]
    past = npg * page
    alpha = (2.0 * depth) ** 0.25
    topk_p = min(TOPK_MAX, seq // 4)
    topk_s = min(TOPK_MAX, (past + nt) // 4)
    n_p, n_s = nb * seq, db * nt
    assert n_s == CHUNK and seq % CHUNK == 0 and page == LANES

    w_ext = _extended_w_in(w_in)
    wo_b, wf1_b, wf2_b = w_o.astype(BF16), w_ff1.astype(BF16), w_ff2.astype(BF16)
    wg_b, wp_b = w_ple_gate.astype(BF16), w_ple_proj.astype(BF16)
    tril = jnp.tril(jnp.ones((CHUNK, CHUNK), F32))
    mm_p = (sgu_w * tril).astype(BF16)
    mb_p = jnp.repeat(jnp.swapaxes(sgu_b, 1, 2), A_GW, axis=2)
    eye = jnp.eye(CHUNK // nt, dtype=F32)
    small = sgu_w[:, :, :nt, :nt] * tril[:nt, :nt]
    mm_s = jnp.einsum('ab,lgts->lgatbs', eye, small).reshape(depth, A_GROUPS, CHUNK, CHUNK).astype(BF16)
    mb_s = jnp.tile(jnp.repeat(jnp.swapaxes(sgu_b[:, :, :nt], 1, 2), A_GW, axis=2), (1, CHUNK // nt, 1))
    lng = sgu_ln_g.reshape(depth, 1, A_WIDTH)
    lnb = sgu_ln_b.reshape(depth, 1, A_WIDTH)

    pos_p = jnp.arange(seq, dtype=I32)
    pos_s = past + (jnp.arange(n_s, dtype=I32) % nt)
    tabs_p = _rope_tables(pos_p, HEAD_DIM) + _rope_tables(pos_p, IDX_DIM)
    tabs_s = _rope_tables(pos_s, HEAD_DIM) + _rope_tables(pos_s, IDX_DIM)

    cki = jnp.swapaxes(cache_kidx, 2, 3)
    ck = jnp.transpose(cache_k, (0, 1, 3, 4, 2)).reshape(cache_k.shape[:2] + (LANES, page))
    cv = jnp.transpose(cache_v, (0, 1, 3, 4, 2)).reshape(cache_v.shape[:2] + (LANES, page))

    rt_proj = _pick(seq, (512, 256, 128))
    rt_mlp = _pick(n_p, (512, 256, 128))
    cw_p = _pick(seq, (256, 128))
    pps = _pick(npg, (8, 4, 2, 1))
    head_lo = (jnp.arange(N_HEADS) // KV_GROUP) * HEAD_DIM

    hp = x_prompt.reshape(n_p, D_MODEL)
    hs = x_sample.reshape(n_s, D_MODEL)
    outs = [[] for _ in range(8)]
    for i in range(depth):
        vec = lambda a: a[i].reshape(1, -1)
        mlp_w = (wo_b[i], vec(ln1_g), vec(ln1_b), wf1_b[i], wf2_b[i], wg_b[i], wp_b[i], vec(ln2_g), vec(ln2_b))

        a_out, vn, q, k, v, qi, ki4, wi = _proj_call(hp, w_ext[i], mm_p[i], mb_p[i], lng[i], lnb[i], tabs_p,
                                                     seq, rt_proj)
        b_out = _prompt_attn_call(qi, wi, q, ki4, k, v, nb, seq, topk_p, cw_p)
        hp = _mlp_call(hp, a_out, b_out, p_prompt[i].reshape(n_p, D_PLE), *mlp_w, alpha, rt_mlp)
        outs[0].append(k.reshape(nb, seq, N_KV_HEADS, HEAD_DIM))
        outs[1].append(v.reshape(nb, seq, N_KV_HEADS, HEAD_DIM))
        outs[2].append(ki4[:, :IDX_DIM].reshape(nb, seq, IDX_DIM))
        nc = min(CHUNK, seq)
        outs[3].append(vn.reshape(nb, seq, A_WIDTH)[:, seq - nc:])

        a_out, vn, q, k, v, qi, ki4, wi = _proj_call(hs, w_ext[i], mm_s[i], mb_s[i], lng[i], lnb[i], tabs_s,
                                                     n_s, n_s)
        qs = qi.reshape(db, nt * IDX_HEADS, IDX_DIM)
        wrep = jnp.broadcast_to(
            (wi[:, :IDX_HEADS] * ((IDX_DIM ** -0.5) * (IDX_HEADS ** -0.5))).reshape(db, nt * IDX_HEADS, 1),
            (db, nt * IDX_HEADS, LANES))
        keys_past = _sample_idx_call(page_table, qs, wrep, cki, i, pps)
        bias_past, bias_new = _sample_select_call(keys_past.reshape(n_s, past), qi, wi, ki4, topk_s, nt, cw_p)
        q4 = q.reshape(db, nt, N_HEADS, 1, HEAD_DIM)
        half = (jnp.arange(N_KV_HEADS) * HEAD_DIM)[None, None, None, :, None]
        qz = jnp.where(half == head_lo[None, None, :, None, None], q4, jnp.zeros_like(q4))
        qz = qz.reshape(db, nt * N_HEADS, LANES)
        o = _sample_attn_call(page_table, qz, bias_past.reshape(db, nt, past), bias_new.reshape(db, nt, LANES),
                              k, v, ck, cv, i, pps, nt)
        o = o.reshape(db, nt, N_HEADS, N_KV_HEADS, HEAD_DIM)
        idx = (jnp.arange(N_HEADS) // KV_GROUP)[None, None, :, None, None]
        b_out = jnp.take_along_axis(o, jnp.broadcast_to(idx, (db, nt, N_HEADS, 1, HEAD_DIM)), axis=3)
        b_out = b_out.reshape(n_s, N_HEADS * HEAD_DIM).astype(BF16)
        hs = _mlp_call(hs, a_out, b_out, p_sample[i].reshape(n_s, D_PLE), *mlp_w, alpha, n_s)
        outs[4].append(k.reshape(db, nt, N_KV_HEADS, HEAD_DIM))
        outs[5].append(v.reshape(db, nt, N_KV_HEADS, HEAD_DIM))
        outs[6].append(ki4[:, :IDX_DIM].reshape(db, nt, IDX_DIM))
        outs[7].append(vn.reshape(db, nt, A_WIDTH))

    st = [jnp.stack(o) for o in outs]
    return (hp.reshape(nb, seq, D_MODEL), hs.reshape(db, nt, D_MODEL),
            st[0], st[1], st[2], st[3], st[4], st[5], st[6], st[7])
```

```python
import functools

import jax
import jax.numpy as jnp
from jax import lax
from jax.experimental import pallas as pl
from jax.experimental.pallas import tpu as pltpu

F32 = jnp.float32
BF16 = jnp.bfloat16
I32 = jnp.int32
I16 = jnp.int16

D_MODEL = 1024
CHUNK = 128
A_WIDTH = 512
A_GROUPS = 4
A_GW = A_WIDTH // A_GROUPS
HEAD_DIM = 64
N_HEADS = 8
N_KV_HEADS = 2
KV_GROUP = N_HEADS // N_KV_HEADS
IDX_HEADS = 8
IDX_DIM = 32
TOPK_MAX = 256
ROPE_THETA = 10000.0
D_FF = 4 * D_MODEL
D_PLE = 256
LN_EPS = 1e-5

LANES = 128
PACK = 16
VMEM_LIMIT = 56 * 1024 * 1024

INT_MIN = -(2 ** 31)
I16_MIN = -(2 ** 15)
NEG = -1e30

C_AU, C_AV, C_Q, C_K, C_V, C_QI, C_KI, C_WI = 0, 512, 1024, 1536, 1664, 1792, 2048, 2176
C_QS, C_KS, C_QIS, C_KIS, C_END = 2304, 2816, 2944, 3200, 3328
N_AU, N_AV, N_K, N_KS, N_KI, N_KIS, N_END = 0, 512, 1024, 1152, 1280, 1408, 1536
T_Q, T_QS, T_K, T_KS, T_V, T_QI, T_QIS, T_KI, T_KIS, T_WI, T_END = (
    0, 512, 1024, 1152, 1280, 1408, 1664, 1920, 1952, 1984, 2000)
WI_ROWS = T_END - T_WI
Q_SCALE_LOG2 = (HEAD_DIM ** -0.5) * 1.4426950408889634
IDX_SCALE = (IDX_DIM ** -0.5) * (IDX_HEADS ** -0.5)

KEY_CHUNK = 256
COUNT_CHUNK = 512

NT_DIMS = (((1,), (1,)), ((), ()))


def _dot(a, b):
    return jnp.dot(a, b, preferred_element_type=F32)


def _dot_nt(a, b):
    return lax.dot_general(a, b, NT_DIMS, preferred_element_type=F32)


def _layer_norm(x, g, b):
    mu = jnp.mean(x, axis=-1, keepdims=True)
    xc = x - mu
    var = jnp.mean(xc * xc, axis=-1, keepdims=True)
    return xc * lax.rsqrt(var + LN_EPS) * g + b


def _gelu(x):
    return 0.5 * x * (1.0 + lax.erf(x * (2.0 ** -0.5)))


def _sortable(x):
    bits = pltpu.bitcast(x, I32)
    return bits ^ ((bits >> 31) & 0x7FFFFFFF)


def _const_spec(shape):
    nd = len(shape)
    return pl.BlockSpec(shape, lambda *_: (0,) * nd, pipeline_mode=pl.Buffered(1))


def _gated_mixer(au, av, g, lng_ref, lnb_ref, mm_ref, mb_ref, vn_ref, aout_ref):
    rows = au.shape[0]
    lo = g * A_GW
    vn = _layer_norm(av, lng_ref[:, lo:lo + A_GW], lnb_ref[:, lo:lo + A_GW])
    vn_ref[:, lo:lo + A_GW] = vn
    vnb = vn.astype(BF16)
    for c in range(rows // CHUNK):
        r0 = c * CHUNK
        gate = _dot(mm_ref[g], vnb[r0:r0 + CHUNK, :]) + mb_ref[:, lo:lo + A_GW]
        aout_ref[r0:r0 + CHUNK, lo:lo + A_GW] = (au[r0:r0 + CHUNK, :] * gate).astype(BF16)


def _proj_kernel(h_ref, w_ref, mm_ref, mb_ref, lng_ref, lnb_ref, cq_ref, sq_ref, ci_ref, si_ref,
                 aout_ref, vn_ref, q_ref, k_ref, v_ref, qi_ref, ki_ref, wi_ref):
    hb = h_ref[...].astype(BF16)

    def proj(lo, width=LANES):
        return _dot(hb, w_ref[:, lo:lo + width])

    for g in range(A_GROUPS):
        lo = g * A_GW
        _gated_mixer(_gelu(proj(C_AU + lo)), _gelu(proj(C_AV + lo)), g,
                     lng_ref, lnb_ref, mm_ref, mb_ref, vn_ref, aout_ref)

    cq, sq = cq_ref[...], sq_ref[...]
    ci, si = ci_ref[...], si_ref[...]
    for j in range(N_HEADS * HEAD_DIM // LANES):
        z = proj(C_Q + j * LANES) * cq + proj(C_QS + j * LANES) * sq
        q_ref[:, j * LANES:(j + 1) * LANES] = (z * (HEAD_DIM ** -0.5)).astype(BF16)
    k_ref[...] = proj(C_K) * cq + proj(C_KS) * sq
    v_ref[...] = proj(C_V)
    for j in range(IDX_HEADS * IDX_DIM // LANES):
        z = proj(C_QI + j * LANES) * ci + proj(C_QIS + j * LANES) * si
        qi_ref[:, j * LANES:(j + 1) * LANES] = z.astype(BF16)
    ki_ref[...] = proj(C_KI) * ci + proj(C_KIS) * si
    wi_ref[...] = proj(C_WI)


def _proj_call(h2d, w_ext, mm, mb, lng, lnb, tabs):
    n = h2d.shape[0]
    out_widths = [(A_WIDTH, BF16), (A_WIDTH, F32), (N_HEADS * HEAD_DIM, BF16), (LANES, F32), (LANES, F32),
                  (IDX_HEADS * IDX_DIM, BF16), (LANES, F32), (LANES, F32)]
    return pl.pallas_call(
        _proj_kernel,
        out_shape=[jax.ShapeDtypeStruct((n, w), dt) for w, dt in out_widths],
        compiler_params=pltpu.CompilerParams(vmem_limit_bytes=VMEM_LIMIT),
        name="proj",
    )(h2d, w_ext, mm, mb, lng, lnb, *tabs)


def _proj_prompt_kernel(h_ref, wn_ref, wt_ref, mm_ref, mb_ref, lng_ref, lnb_ref,
                        cq_ref, sq_ref, ci_ref, si_ref, cqt_ref, sqt_ref, cit_ref, sit_ref,
                        aout_ref, vn_ref, qt_ref, kb_ref, kt_ref, vt_ref, qit_ref, kib_ref, kit_ref, wit_ref):
    hb = h_ref[...].astype(BF16)
    wide = 2 * LANES

    for gp in range(A_GROUPS // 2):
        au2 = _gelu(_dot(hb, wn_ref[:, N_AU + gp * wide:N_AU + (gp + 1) * wide]))
        av2 = _gelu(_dot(hb, wn_ref[:, N_AV + gp * wide:N_AV + (gp + 1) * wide]))
        for e in range(2):
            _gated_mixer(au2[:, e * A_GW:(e + 1) * A_GW], av2[:, e * A_GW:(e + 1) * A_GW], 2 * gp + e,
                         lng_ref, lnb_ref, mm_ref, mb_ref, vn_ref, aout_ref)

    z = _dot(hb, wn_ref[:, N_K:N_K + wide])
    kb_ref[...] = (z[:, :LANES] * cq_ref[...] + z[:, LANES:] * sq_ref[...]).astype(BF16)
    z = _dot(hb, wn_ref[:, N_KI:N_KI + wide])
    kib_ref[...] = (z[:, :LANES] * ci_ref[...] + z[:, LANES:] * si_ref[...]).astype(BF16)

    zt = _dot_nt(wt_ref[...], hb)
    cqt, sqt = cqt_ref[...], sqt_ref[...]
    cit, sit = cit_ref[...], sit_ref[...]
    for j in range(N_HEADS * HEAD_DIM // LANES):
        r = j * LANES
        q = zt[T_Q + r:T_Q + r + LANES] * cqt + zt[T_QS + r:T_QS + r + LANES] * sqt
        qt_ref[r:r + LANES, :] = (q * Q_SCALE_LOG2).astype(BF16)
    kt_ref[...] = zt[T_K:T_K + LANES] * cqt + zt[T_KS:T_KS + LANES] * sqt
    vt_ref[...] = zt[T_V:T_V + LANES]
    for j in range(IDX_HEADS * IDX_DIM // LANES):
        r = j * LANES
        qi = zt[T_QI + r:T_QI + r + LANES] * cit + zt[T_QIS + r:T_QIS + r + LANES] * sit
        qit_ref[r:r + LANES, :] = qi.astype(BF16)
    kit_ref[...] = zt[T_KI:T_KI + IDX_DIM] * cit[:IDX_DIM] + zt[T_KIS:T_KIS + IDX_DIM] * sit[:IDX_DIM]
    wit_ref[...] = zt[T_WI:T_END]


def _proj_prompt_call(h2d, wn, wt, mm, mb, lng, lnb, tabs, tabs_t, nb, seq, row_tile):
    n = nb * seq
    ns = seq // row_tile
    row_map = lambda s, b: (b * ns + s, 0)
    col_map = lambda s, b: (0, b * ns + s)
    bcol_map = lambda s, b: (b, 0, s)
    hd, idd = N_HEADS * HEAD_DIM, IDX_HEADS * IDX_DIM
    out_specs = [pl.BlockSpec((row_tile, A_WIDTH), row_map), pl.BlockSpec((row_tile, A_WIDTH), row_map),
                 pl.BlockSpec((hd, row_tile), col_map),
                 pl.BlockSpec((row_tile, LANES), row_map),
                 pl.BlockSpec((None, LANES, row_tile), bcol_map),
                 pl.BlockSpec((None, LANES, row_tile), bcol_map),
                 pl.BlockSpec((idd, row_tile), col_map),
                 pl.BlockSpec((row_tile, LANES), row_map),
                 pl.BlockSpec((None, IDX_DIM, row_tile), bcol_map),
                 pl.BlockSpec((WI_ROWS, row_tile), col_map)]
    out_shape = [jax.ShapeDtypeStruct((n, A_WIDTH), BF16), jax.ShapeDtypeStruct((n, A_WIDTH), F32),
                 jax.ShapeDtypeStruct((hd, n), BF16),
                 jax.ShapeDtypeStruct((n, LANES), BF16),
                 jax.ShapeDtypeStruct((nb, LANES, seq), F32),
                 jax.ShapeDtypeStruct((nb, LANES, seq), F32),
                 jax.ShapeDtypeStruct((idd, n), BF16),
                 jax.ShapeDtypeStruct((n, LANES), BF16),
                 jax.ShapeDtypeStruct((nb, IDX_DIM, seq), F32),
                 jax.ShapeDtypeStruct((WI_ROWS, n), F32)]
    return pl.pallas_call(
        _proj_prompt_kernel,
        grid=(ns, nb),
        in_specs=[pl.BlockSpec((row_tile, D_MODEL), row_map),
                  _const_spec(wn.shape), _const_spec(wt.shape), _const_spec(mm.shape), _const_spec(mb.shape),
                  _const_spec(lng.shape), _const_spec(lnb.shape)]
                 + [pl.BlockSpec((row_tile, LANES), lambda s, b: (s, 0))] * 4
                 + [pl.BlockSpec((LANES, row_tile), lambda s, b: (0, s))] * 4,
        out_specs=out_specs,
        out_shape=out_shape,
        compiler_params=pltpu.CompilerParams(dimension_semantics=("arbitrary", "arbitrary"),
                                             vmem_limit_bytes=VMEM_LIMIT),
        name="proj_prompt",
    )(h2d, wn, wt, mm, mb, lng, lnb, *tabs, *tabs_t)


def _count16(ref, nsteps, cand, strict=False):
    one, zero = jnp.ones(cand.shape, I16), jnp.zeros(cand.shape, I16)
    n_acc = 4

    def body(c, accs):
        off = pl.multiple_of(c * COUNT_CHUNK, COUNT_CHUNK)
        blk = ref[pl.ds(off, COUNT_CHUNK), :]
        accs = list(accs)
        for j in range(COUNT_CHUNK // PACK):
            kk = blk[j * PACK:(j + 1) * PACK, :]
            hit = (kk > cand) if strict else (kk >= cand)
            accs[j % n_acc] = accs[j % n_acc] + jnp.where(hit, one, zero)
        return tuple(accs)
    accs = lax.fori_loop(0, nsteps, body, (zero,) * n_acc)
    tot = (accs[0] + accs[1]) + (accs[2] + accs[3])
    return jnp.sum(tot.astype(F32), axis=0, keepdims=True)


def _search16(ref, nsteps, target):
    qw = ref.shape[1]

    def bit_body(i, prefix):
        cand = prefix + lax.shift_left(jnp.int32(1), 15 - i)
        cand16 = jnp.broadcast_to(cand, (PACK, qw)).astype(I16)
        return jnp.where(_count16(ref, nsteps, cand16) >= target, cand, prefix)
    return lax.fori_loop(0, 16, bit_body, jnp.full((1, qw), I16_MIN, I32))


def _prompt_attn_kernel(qit_ref, wit_ref, qt_ref, kib_ref, kb_ref, vt_ref, o_ref,
                        keys_ref, hi_ref, lo_ref, rhs_ref, qz_ref, acc_ref, *, topk):
    qw = o_ref.shape[0]
    qb = pl.program_id(1)
    nch = (qb * qw + qw + KEY_CHUNK - 1) // KEY_CHUNK
    nsteps = (nch * KEY_CHUNK + COUNT_CHUNK - 1) // COUNT_CHUNK
    qpos = qb * qw + lax.broadcasted_iota(I32, (KEY_CHUNK, qw), 1)
    krow = lax.broadcasted_iota(I32, (KEY_CHUNK, qw), 0)

    rhs_ref[...] = jnp.zeros(rhs_ref.shape, BF16)
    qz_ref[...] = jnp.zeros(qz_ref.shape, BF16)
    slots = LANES // IDX_DIM
    for h in range(IDX_HEADS):
        r = (h % slots) * IDX_DIM
        rhs_ref[r:r + IDX_DIM, h * qw:(h + 1) * qw] = qit_ref[h * IDX_DIM:(h + 1) * IDX_DIM, :]
    for h in range(N_HEADS):
        r = (h // KV_GROUP) * HEAD_DIM
        qz_ref[r:r + HEAD_DIM, h * qw:(h + 1) * qw] = qt_ref[h * HEAD_DIM:(h + 1) * HEAD_DIM, :]
    wsc = wit_ref[...] * IDX_SCALE

    def score_body(c, carry):
        off = pl.multiple_of(c * KEY_CHUNK, KEY_CHUNK)
        kc = kib_ref[pl.ds(off, KEY_CHUNK), :]
        acc = None
        for h in range(IDX_HEADS):
            s = _dot(kc, rhs_ref[:, h * qw:(h + 1) * qw])
            t = jnp.maximum(s, 0.0) * wsc[h:h + 1, :]
            acc = t if acc is None else acc + t
        key = jnp.where(krow + off <= qpos, _sortable(acc), INT_MIN)
        keys_ref[pl.ds(off, KEY_CHUNK), :] = key
        hi_ref[pl.ds(off, KEY_CHUNK), :] = (key >> 16).astype(I16)
        lo_ref[pl.ds(off, KEY_CHUNK), :] = ((key & 0xFFFF) - 2 ** 15).astype(I16)
        return carry
    lax.fori_loop(0, nch, score_body, 0)

    @pl.when(nch * KEY_CHUNK < nsteps * COUNT_CHUNK)
    def _():
        off = pl.multiple_of(nch * KEY_CHUNK, KEY_CHUNK)
        fill = jnp.full((KEY_CHUNK, qw), I16_MIN, I16)
        hi_ref[pl.ds(off, KEY_CHUNK), :] = fill
        lo_ref[pl.ds(off, KEY_CHUNK), :] = fill

    p_hi = _search16(hi_ref, nsteps, float(topk))
    p16 = jnp.broadcast_to(p_hi, (PACK, qw)).astype(I16)
    room = topk - _count16(hi_ref, nsteps, p16, strict=True)

    def low_body(c, carry):
        off = pl.multiple_of(c * COUNT_CHUNK, COUNT_CHUNK)
        lowest = jnp.full((PACK, qw), I16_MIN, I16)
        his, los = hi_ref[pl.ds(off, COUNT_CHUNK), :], lo_ref[pl.ds(off, COUNT_CHUNK), :]
        lo_ref[pl.ds(off, COUNT_CHUNK), :] = jnp.concatenate(
            [jnp.where(his[j * PACK:(j + 1) * PACK, :] == p16, los[j * PACK:(j + 1) * PACK, :], lowest)
             for j in range(COUNT_CHUNK // PACK)], axis=0)
        return carry
    lax.fori_loop(0, nsteps, low_body, 0)
    p_lo = _search16(lo_ref, nsteps, room)
    thr = jnp.maximum(p_hi * 65536 + (p_lo + 2 ** 15), INT_MIN + 1)

    def count32(strict):
        def body(c, acc):
            off = pl.multiple_of(c * KEY_CHUNK, KEY_CHUNK)
            kk = keys_ref[pl.ds(off, KEY_CHUNK), :]
            hit = (kk > thr) if strict else (kk >= thr)
            return acc + jnp.sum(jnp.where(hit, 1.0, 0.0), axis=0, keepdims=True)
        return lax.fori_loop(0, nch, body, jnp.zeros((1, qw), F32))

    @pl.when(jnp.max(count32(False)) > topk)
    def _():
        spare = topk - count32(True)
        r_i = lax.broadcasted_iota(I32, (KEY_CHUNK, KEY_CHUNK), 0)
        c_i = lax.broadcasted_iota(I32, (KEY_CHUNK, KEY_CHUNK), 1)
        lower = jnp.where(c_i <= r_i, 1.0, 0.0).astype(BF16)

        def body(c, seen):
            off = pl.multiple_of(c * KEY_CHUNK, KEY_CHUNK)
            kk = keys_ref[pl.ds(off, KEY_CHUNK), :]
            tie = kk == thr
            rank = seen + _dot(lower, jnp.where(tie, 1.0, 0.0).astype(BF16))
            keys_ref[pl.ds(off, KEY_CHUNK), :] = jnp.where(tie & (rank > spare), thr - 1, kk)
            return rank[KEY_CHUNK - 1:KEY_CHUNK, :]
        lax.fori_loop(0, nch, body, jnp.zeros((1, qw), F32))

    acc_ref[...] = jnp.zeros(acc_ref.shape, F32)

    def attn_body(c, carry):
        ms, ls = list(carry[:N_HEADS]), list(carry[N_HEADS:])
        off = pl.multiple_of(c * KEY_CHUNK, KEY_CHUNK)
        sel = keys_ref[pl.ds(off, KEY_CHUNK), :] >= thr
        kb = kb_ref[pl.ds(off, KEY_CHUNK), :]
        vt = vt_ref[:, pl.ds(off, KEY_CHUNK)].astype(BF16)
        for g in range(N_KV_HEADS):
            ps, alphas = [], []
            for j in range(KV_GROUP):
                h = g * KV_GROUP + j
                s = jnp.where(sel, _dot(kb, qz_ref[:, h * qw:(h + 1) * qw]), NEG)
                m_new = jnp.maximum(ms[h], jnp.max(s, axis=0, keepdims=True))
                alpha = jnp.exp2(ms[h] - m_new)
                pr = jnp.exp2(s - m_new)
                ls[h] = alpha * ls[h] + jnp.sum(pr, axis=0, keepdims=True)
                ms[h] = m_new
                ps.append(pr.astype(BF16))
                alphas.append(alpha)
            pv = _dot(vt[g * HEAD_DIM:(g + 1) * HEAD_DIM, :], jnp.concatenate(ps, axis=1))
            acc_ref[g] = acc_ref[g] * jnp.concatenate(alphas, axis=1) + pv
        return tuple(ms + ls)
    init = (jnp.full((1, qw), NEG, F32),) * N_HEADS + (jnp.zeros((1, qw), F32),) * N_HEADS
    fin = lax.fori_loop(0, nch, attn_body, init)
    ls = fin[N_HEADS:]

    for p in range(N_HEADS // 2):
        halves = []
        for e in range(2):
            h = 2 * p + e
            g, j = h // KV_GROUP, h % KV_GROUP
            halves.append(acc_ref[g][:, j * qw:(j + 1) * qw] / ls[h])
        o_ref[:, p * LANES:(p + 1) * LANES] = jnp.concatenate(halves, axis=0).T.astype(BF16)


def _prompt_attn_call(qit, wit, qt, kib, kb, vt, nb, seq, topk, qw):
    nq = seq // qw
    hd, idd = N_HEADS * HEAD_DIM, IDX_HEADS * IDX_DIM
    col = lambda b, i: (0, b * nq + i)
    kern = functools.partial(_prompt_attn_kernel, topk=topk)
    return pl.pallas_call(
        kern,
        grid=(nb, nq),
        in_specs=[pl.BlockSpec((idd, qw), col),
                  pl.BlockSpec((WI_ROWS, qw), col),
                  pl.BlockSpec((hd, qw), col),
                  pl.BlockSpec((seq, LANES), lambda b, i: (b, 0)),
                  pl.BlockSpec((seq, LANES), lambda b, i: (b, 0)),
                  pl.BlockSpec((None, LANES, seq), lambda b, i: (b, 0, 0))],
        out_specs=pl.BlockSpec((qw, hd), lambda b, i: (b * nq + i, 0)),
        out_shape=jax.ShapeDtypeStruct((nb * seq, hd), BF16),
        scratch_shapes=[pltpu.VMEM((seq, qw), I32),
                        pltpu.VMEM((seq, qw), I16),
                        pltpu.VMEM((seq, qw), I16),
                        pltpu.VMEM((LANES, IDX_HEADS * qw), BF16),
                        pltpu.VMEM((LANES, N_HEADS * qw), BF16),
                        pltpu.VMEM((N_KV_HEADS, HEAD_DIM, KV_GROUP * qw), F32)],
        compiler_params=pltpu.CompilerParams(dimension_semantics=("arbitrary", "arbitrary"),
                                             vmem_limit_bytes=VMEM_LIMIT),
        name="prompt_attn",
    )(qit, wit, qt, kib, kb, vt)


def _mlp_kernel(h_ref, a_ref, b_ref, pe_ref, wo_ref, g1_ref, b1_ref, wf1_ref, wf2_ref, wg_ref, wp_ref,
                g2_ref, b2_ref, out_ref, *, alpha, ff_chunk):
    mix = _dot(a_ref[...], wo_ref[0:A_WIDTH, :]) + _dot(b_ref[...], wo_ref[A_WIDTH:2 * A_WIDTH, :])
    h1 = _layer_norm(alpha * h_ref[...] + mix, g1_ref[...], b1_ref[...])
    h1b = h1.astype(BF16)
    ff = None
    for c in range(D_FF // ff_chunk):
        hid = jnp.maximum(_dot(h1b, wf1_ref[:, c * ff_chunk:(c + 1) * ff_chunk]), 0.0)
        part = _dot((hid * hid).astype(BF16), wf2_ref[c * ff_chunk:(c + 1) * ff_chunk, :])
        ff = part if ff is None else ff + part
    ple = jax.nn.sigmoid(_dot(h1b, wg_ref[...])) * _dot(pe_ref[...].astype(BF16), wp_ref[...])
    out_ref[...] = _layer_norm(alpha * h1 + ff + ple, g2_ref[...], b2_ref[...])


def _mlp_call(h2d, a_out, b_out, pe, wo, g1, b1, wf1, wf2, wg, wp, g2, b2, alpha, row_tile):
    n = h2d.shape[0]
    row = lambda i: (i, 0)
    kern = functools.partial(_mlp_kernel, alpha=alpha, ff_chunk=1024)
    consts = [wo, g1, b1, wf1, wf2, wg, wp, g2, b2]
    return pl.pallas_call(
        kern,
        grid=(n // row_tile,),
        in_specs=[pl.BlockSpec((row_tile, D_MODEL), row),
                  pl.BlockSpec((row_tile, A_WIDTH), row),
                  pl.BlockSpec((row_tile, A_WIDTH), row),
                  pl.BlockSpec((row_tile, D_PLE), row)] + [_const_spec(c.shape) for c in consts],
        out_specs=pl.BlockSpec((row_tile, D_MODEL), row),
        out_shape=jax.ShapeDtypeStruct((n, D_MODEL), F32),
        compiler_params=pltpu.CompilerParams(dimension_semantics=("arbitrary",),
                                             vmem_limit_bytes=VMEM_LIMIT),
        name="mlp",
    )(h2d, a_out, b_out, pe, *consts)


def _head_masked_queries(qi_ref):
    lane_head = lax.broadcasted_iota(I32, (CHUNK, LANES), 1) >> 5
    heads_per_col = LANES // IDX_DIM
    out = []
    for h in range(IDX_HEADS):
        col = qi_ref[:, (h // heads_per_col) * LANES:(h // heads_per_col + 1) * LANES]
        out.append(jnp.where(lane_head == (h % heads_per_col), col, jnp.zeros_like(col)))
    return out


def _index_scores(qh, wsc, kc):
    acc = None
    for h in range(IDX_HEADS):
        s = _dot_nt(qh[h], kc)
        t = jnp.maximum(s, 0.0) * wsc[:, h:h + 1]
        acc = t if acc is None else acc + t
    return acc


def _count_ge(keys_ref, nch, cw, cand, strict=False):
    def body(c, acc):
        off = pl.multiple_of(c * cw, cw)
        for j in range(cw // LANES):
            kk = keys_ref[:, pl.ds(off + j * LANES, LANES)]
            hit = (kk > cand) if strict else (kk >= cand)
            acc = acc + jnp.where(hit, 1.0, 0.0)
        return acc
    acc = lax.fori_loop(0, nch, body, jnp.zeros((CHUNK, LANES), F32))
    cnt = jnp.sum(acc, axis=-1, keepdims=True)
    return jnp.broadcast_to(cnt, (CHUNK, LANES))


def _select_threshold(keys_ref, nch, cw, topk):
    def bit_body(i, prefix):
        cand = prefix + lax.shift_left(jnp.int32(1), 31 - i)
        cnt = _count_ge(keys_ref, nch, cw, cand)
        return jnp.where(cnt >= topk, cand, prefix)
    prefix = lax.fori_loop(0, 32, bit_body, jnp.full((CHUNK, LANES), INT_MIN, I32))
    thr = jnp.maximum(prefix, INT_MIN + 1)
    cnt_ge = _count_ge(keys_ref, nch, cw, thr)

    @pl.when(jnp.max(cnt_ge) > topk)
    def _():
        room = topk - _count_ge(keys_ref, nch, cw, thr, strict=True)
        r_i = lax.broadcasted_iota(I32, (LANES, LANES), 0)
        c_i = lax.broadcasted_iota(I32, (LANES, LANES), 1)
        upper = jnp.where(r_i <= c_i, 1.0, 0.0).astype(BF16)

        def body(c, seen):
            off = pl.multiple_of(c * cw, cw)
            for j in range(cw // LANES):
                kk = keys_ref[:, pl.ds(off + j * LANES, LANES)]
                tie = kk == thr
                rank = seen + _dot(jnp.where(tie, 1.0, 0.0).astype(BF16), upper)
                keys_ref[:, pl.ds(off + j * LANES, LANES)] = jnp.where(tie & (rank > room), thr - 1, kk)
                seen = jnp.broadcast_to(rank[:, LANES - 1:LANES], (CHUNK, LANES))
            return seen
        lax.fori_loop(0, nch, body, jnp.zeros((CHUNK, LANES), F32))

    return thr


def _sample_idx_kernel(pt_ref, qs_ref, w_ref, *refs, pages_per_step):
    page_refs, out_ref = refs[:pages_per_step], refs[pages_per_step]
    qs = qs_ref[0]
    w = w_ref[0]
    nt = out_ref.shape[1]
    for j in range(pages_per_step):
        kp = page_refs[j][...].astype(BF16)
        t = jnp.maximum(_dot(qs, kp), 0.0) * w
        for q in range(nt):
            sc = jnp.sum(t[q * IDX_HEADS:(q + 1) * IDX_HEADS, :], axis=0, keepdims=True)
            out_ref[0, q:q + 1, j * LANES:(j + 1) * LANES] = _sortable(sc)


def _sample_idx_call(page_table, qs, wrep, cache_kidx, layer, pps):
    db, npg = page_table.shape
    nt = qs.shape[1] // IDX_HEADS
    page = cache_kidx.shape[3]
    kern = functools.partial(_sample_idx_kernel, pages_per_step=pps)

    def page_spec(j):
        return pl.BlockSpec((None, None, IDX_DIM, page), lambda b, s, pt: (layer, pt[b, s * pps + j], 0, 0))

    grid_spec = pltpu.PrefetchScalarGridSpec(
        num_scalar_prefetch=1,
        grid=(db, npg // pps),
        in_specs=[pl.BlockSpec((1,) + qs.shape[1:], lambda b, s, pt: (b, 0, 0)),
                  pl.BlockSpec((1,) + wrep.shape[1:], lambda b, s, pt: (b, 0, 0))]
                 + [page_spec(j) for j in range(pps)],
        out_specs=pl.BlockSpec((1, nt, pps * page), lambda b, s, pt: (b, 0, s)),
    )
    return pl.pallas_call(
        kern, grid_spec=grid_spec,
        out_shape=jax.ShapeDtypeStruct((db, nt, npg * page), I32),
        compiler_params=pltpu.CompilerParams(dimension_semantics=("arbitrary", "arbitrary")),
        name="sample_idx",
    )(page_table, qs, wrep, *([cache_kidx] * pps))


def _sample_select_kernel(kp_ref, qi_ref, wi_ref, ki_ref, bp_ref, bn_ref, keys_ref, *, topk, cw, nt):
    past = kp_ref.shape[1]
    width = keys_ref.shape[1]
    nch = width // cw
    keys_ref[:, 0:past] = kp_ref[...]
    qh = _head_masked_queries(qi_ref)
    wsc = wi_ref[...] * IDX_SCALE
    key = _sortable(_index_scores(qh, wsc, ki_ref[...].astype(BF16)))
    r_i = lax.broadcasted_iota(I32, (CHUNK, LANES), 0)
    c_i = lax.broadcasted_iota(I32, (CHUNK, LANES), 1)
    visible = ((r_i // nt) == (c_i // nt)) & ((c_i % nt) <= (r_i % nt))
    keys_ref[:, past:past + LANES] = jnp.where(visible, key, INT_MIN)
    if width > past + LANES:
        keys_ref[:, past + LANES:width] = jnp.full((CHUNK, width - past - LANES), INT_MIN, I32)

    thr = _select_threshold(keys_ref, nch, cw, topk)

    for c in range(past // LANES):
        bp_ref[:, c * LANES:(c + 1) * LANES] = jnp.where(
            keys_ref[:, c * LANES:(c + 1) * LANES] >= thr, 0.0, NEG)
    bn_ref[...] = jnp.where(keys_ref[:, past:past + LANES] >= thr, 0.0, NEG)


def _sample_select_call(keys_past, qi, wi, ki4, topk, nt, cw):
    rows, past = keys_past.shape
    width = -(-(past + LANES) // cw) * cw
    kern = functools.partial(_sample_select_kernel, topk=topk, cw=cw, nt=nt)
    return pl.pallas_call(
        kern,
        out_shape=[jax.ShapeDtypeStruct((rows, past), F32), jax.ShapeDtypeStruct((rows, LANES), F32)],
        scratch_shapes=[pltpu.VMEM((rows, width), I32)],
        compiler_params=pltpu.CompilerParams(vmem_limit_bytes=VMEM_LIMIT),
        name="sample_select",
    )(keys_past, qi, wi, ki4)


def _sample_attn_kernel(pt_ref, qz_ref, bp_ref, bn_ref, kn_ref, vn_ref, *refs, pages_per_step, nt):
    pps = pages_per_step
    k_refs, v_refs = refs[:pps], refs[pps:2 * pps]
    o_ref, m_ref, l_ref, acc_ref = refs[2 * pps:]
    step = pl.program_id(1)
    qz = qz_ref[0]
    rows = qz.shape[0]
    hpq = rows // nt

    @pl.when(step == 0)
    def _():
        m_ref[...] = jnp.full(m_ref.shape, NEG, F32)
        l_ref[...] = jnp.zeros(l_ref.shape, F32)
        acc_ref[...] = jnp.zeros(acc_ref.shape, F32)

    def update(s, pv_fn, bias):
        width = s.shape[1]
        s = jnp.concatenate(
            [s[q * hpq:(q + 1) * hpq, :] + bias[q:q + 1, :] for q in range(nt)], axis=0)
        m_old = m_ref[...]
        m_new = jnp.maximum(m_old, jnp.max(s, axis=-1, keepdims=True))
        alpha = jnp.exp(m_old - m_new)
        pr = jnp.exp(s - jnp.concatenate([m_new] * (width // LANES), axis=1))
        l_ref[...] = alpha * l_ref[...] + jnp.sum(pr, axis=-1, keepdims=True)
        m_ref[...] = m_new
        acc_ref[...] = alpha * acc_ref[...] + pv_fn(pr.astype(BF16))

    kb = jnp.concatenate([r[...].astype(BF16) for r in k_refs], axis=1)
    vb = jnp.concatenate([r[...].astype(BF16) for r in v_refs], axis=1)
    update(_dot(qz, kb), lambda pr: _dot_nt(pr, vb), bp_ref[0])

    @pl.when(step == pl.num_programs(1) - 1)
    def _():
        vn = vn_ref[...].astype(BF16)
        update(_dot_nt(qz, kn_ref[...].astype(BF16)), lambda pr: _dot(pr, vn), bn_ref[0])
        o_ref[0] = acc_ref[...] / l_ref[...]


def _sample_attn_call(page_table, qz, bias_past, bias_new, k_new, v_new, cache_k, cache_v, layer, pps, nt):
    db, npg = page_table.shape
    page = cache_k.shape[3]
    rows = qz.shape[1]
    kern = functools.partial(_sample_attn_kernel, pages_per_step=pps, nt=nt)

    def page_spec(j):
        return pl.BlockSpec((None, None, LANES, page), lambda b, s, pt: (layer, pt[b, s * pps + j], 0, 0))

    grid_spec = pltpu.PrefetchScalarGridSpec(
        num_scalar_prefetch=1,
        grid=(db, npg // pps),
        in_specs=[pl.BlockSpec((1, rows, LANES), lambda b, s, pt: (b, 0, 0)),
                  pl.BlockSpec((1, nt, pps * page), lambda b, s, pt: (b, 0, s)),
                  pl.BlockSpec((1, nt, LANES), lambda b, s, pt: (b, 0, 0)),
                  pl.BlockSpec(k_new.shape, lambda b, s, pt: (0, 0)),
                  pl.BlockSpec(v_new.shape, lambda b, s, pt: (0, 0))]
                 + [page_spec(j) for j in range(pps)] * 2,
        out_specs=pl.BlockSpec((1, rows, LANES), lambda b, s, pt: (b, 0, 0)),
        scratch_shapes=[pltpu.VMEM((rows, LANES), F32)] * 3,
    )
    return pl.pallas_call(
        kern, grid_spec=grid_spec,
        out_shape=jax.ShapeDtypeStruct((db, rows, LANES), F32),
        compiler_params=pltpu.CompilerParams(dimension_semantics=("arbitrary", "arbitrary")),
        name="sample_attn",
    )(page_table, qz, bias_past, bias_new, k_new, v_new, *([cache_k] * pps), *([cache_v] * pps))


def _swap_halves(w, head_dim):
    lead = w.shape[:-1]
    nh = w.shape[-1] // head_dim
    return w.reshape(lead + (nh, 2, head_dim // 2))[..., ::-1, :].reshape(lead + (nh * head_dim,))


def _projection_weights(w_in):
    sizes = (A_WIDTH, A_WIDTH, N_HEADS * HEAD_DIM, N_KV_HEADS * HEAD_DIM, N_KV_HEADS * HEAD_DIM,
             IDX_HEADS * IDX_DIM, IDX_DIM, IDX_HEADS)
    pts = [sum(sizes[:i + 1]) for i in range(len(sizes) - 1)]
    au, av, q, k, v, qi, ki, wi = jnp.split(w_in.astype(BF16), pts, axis=-1)
    rep = LANES // IDX_DIM
    q_s, k_s, qi_s, ki_s = (_swap_halves(q, HEAD_DIM), _swap_halves(k, HEAD_DIM),
                            _swap_halves(qi, IDX_DIM), _swap_halves(ki, IDX_DIM))
    ki4, ki4_s = jnp.tile(ki, (1, 1, rep)), jnp.tile(ki_s, (1, 1, rep))
    pad_cols = lambda w, n: jnp.pad(w, ((0, 0), (0, 0), (0, n - w.shape[-1])))
    ext = jnp.concatenate([au, av, q, k, v, qi, ki4, pad_cols(wi, LANES), q_s, k_s, qi_s, ki4_s], axis=-1)
    wn = jnp.concatenate([au, av, k, k_s, ki4, ki4_s], axis=-1)
    wt = jnp.swapaxes(jnp.concatenate([q, q_s, k, k_s, v, qi, qi_s, ki, ki_s, pad_cols(wi, WI_ROWS)], axis=-1),
                      1, 2)
    assert ext.shape[-1] == C_END and wn.shape[-1] == N_END and wt.shape[1] == T_END
    return ext, wn, wt


def _rope_tables(pos, head_dim):
    half = head_dim // 2
    inv = ROPE_THETA ** (-jnp.arange(half, dtype=F32) / half)
    ang = pos.astype(F32)[:, None] * inv[None, :]
    cos, sin = jnp.cos(ang), jnp.sin(ang)
    reps = LANES // head_dim
    return (jnp.tile(jnp.concatenate([cos, cos], axis=-1), (1, reps)),
            jnp.tile(jnp.concatenate([-sin, sin], axis=-1), (1, reps)))


def _pick(n, prefs):
    for p in prefs:
        if n % p == 0:
            return p
    return n


def kernel(x_prompt, x_sample, p_prompt, p_sample, cache_k, cache_v, cache_kidx, page_table, w_in, sgu_ln_g,
           sgu_ln_b, sgu_w, sgu_b, w_o, ln1_g, ln1_b, w_ff1, w_ff2, w_ple_gate, w_ple_proj, ln2_g, ln2_b):
    nb, seq, _ = x_prompt.shape
    db, nt, _ = x_sample.shape
    depth = w_in.shape[0]
    npg = page_table.shape[1]
    page = cache_k.shape[2]
    past = npg * page
    alpha = (2.0 * depth) ** 0.25
    topk_p = min(TOPK_MAX, seq // 4)
    topk_s = min(TOPK_MAX, (past + nt) // 4)
    n_p, n_s = nb * seq, db * nt
    assert n_s == CHUNK and seq % COUNT_CHUNK == 0 and page == LANES

    w_ext, w_n, w_t = _projection_weights(w_in)
    wo_b, wf1_b, wf2_b = w_o.astype(BF16), w_ff1.astype(BF16), w_ff2.astype(BF16)
    wg_b, wp_b = w_ple_gate.astype(BF16), w_ple_proj.astype(BF16)
    tril = jnp.tril(jnp.ones((CHUNK, CHUNK), F32))
    mm_p = (sgu_w * tril).astype(BF16)
    mb_p = jnp.repeat(jnp.swapaxes(sgu_b, 1, 2), A_GW, axis=2)
    eye = jnp.eye(CHUNK // nt, dtype=F32)
    small = sgu_w[:, :, :nt, :nt] * tril[:nt, :nt]
    mm_s = jnp.einsum('ab,lgts->lgatbs', eye, small).reshape(depth, A_GROUPS, CHUNK, CHUNK).astype(BF16)
    mb_s = jnp.tile(jnp.repeat(jnp.swapaxes(sgu_b[:, :, :nt], 1, 2), A_GW, axis=2), (1, CHUNK // nt, 1))
    lng = sgu_ln_g.reshape(depth, 1, A_WIDTH)
    lnb = sgu_ln_b.reshape(depth, 1, A_WIDTH)

    pos_p = jnp.arange(seq, dtype=I32)
    pos_s = past + (jnp.arange(n_s, dtype=I32) % nt)
    tabs_p = _rope_tables(pos_p, HEAD_DIM) + _rope_tables(pos_p, IDX_DIM)
    tabs_pt = tuple(t.T for t in tabs_p)
    tabs_s = _rope_tables(pos_s, HEAD_DIM) + _rope_tables(pos_s, IDX_DIM)

    cki = jnp.swapaxes(cache_kidx, 2, 3)
    ck = jnp.transpose(cache_k, (0, 1, 3, 4, 2)).reshape(cache_k.shape[:2] + (LANES, page))
    cv = jnp.transpose(cache_v, (0, 1, 3, 4, 2)).reshape(cache_v.shape[:2] + (LANES, page))

    rt_proj = _pick(seq, (512, 256, 128))
    rt_mlp = _pick(n_p, (512, 256, 128))
    qw_p = _pick(seq, (256, 128))
    cw_s = 256
    pps = _pick(npg, (64, 32, 16, 8, 4, 2, 1))
    head_lo = (jnp.arange(N_HEADS) // KV_GROUP) * HEAD_DIM

    hp = x_prompt.reshape(n_p, D_MODEL)
    hs = x_sample.reshape(n_s, D_MODEL)
    outs = [[] for _ in range(8)]
    for i in range(depth):
        vec = lambda a: a[i].reshape(1, -1)
        mlp_w = (wo_b[i], vec(ln1_g), vec(ln1_b), wf1_b[i], wf2_b[i], wg_b[i], wp_b[i], vec(ln2_g), vec(ln2_b))

        a_out, vn, qt, kb, kt, vt, qit, kib, kit, wit = _proj_prompt_call(
            hp, w_n[i], w_t[i], mm_p[i], mb_p[i], lng[i], lnb[i], tabs_p, tabs_pt, nb, seq, rt_proj)
        b_out = _prompt_attn_call(qit, wit, qt, kib, kb, vt, nb, seq, topk_p, qw_p)
        hp = _mlp_call(hp, a_out, b_out, p_prompt[i].reshape(n_p, D_PLE), *mlp_w, alpha, rt_mlp)
        to_tokens = lambda a: jnp.transpose(a.reshape(nb, N_KV_HEADS, HEAD_DIM, seq), (0, 3, 1, 2))
        outs[0].append(to_tokens(kt))
        outs[1].append(to_tokens(vt))
        outs[2].append(jnp.swapaxes(kit, 1, 2))
        nc = min(CHUNK, seq)
        outs[3].append(vn.reshape(nb, seq, A_WIDTH)[:, seq - nc:])

        a_out, vn, q, k, v, qi, ki4, wi = _proj_call(hs, w_ext[i], mm_s[i], mb_s[i], lng[i], lnb[i], tabs_s)
        qs = qi.reshape(db, nt * IDX_HEADS, IDX_DIM)
        wrep = jnp.broadcast_to((wi[:, :IDX_HEADS] * IDX_SCALE).reshape(db, nt * IDX_HEADS, 1),
                                (db, nt * IDX_HEADS, LANES))
        keys_past = _sample_idx_call(page_table, qs, wrep, cki, i, pps)
        bias_past, bias_new = _sample_select_call(keys_past.reshape(n_s, past), qi, wi, ki4, topk_s, nt, cw_s)
        q4 = q.reshape(db, nt, N_HEADS, 1, HEAD_DIM)
        half = (jnp.arange(N_KV_HEADS) * HEAD_DIM)[None, None, None, :, None]
        qz = jnp.where(half == head_lo[None, None, :, None, None], q4, jnp.zeros_like(q4))
        qz = qz.reshape(db, nt * N_HEADS, LANES)
        o = _sample_attn_call(page_table, qz, bias_past.reshape(db, nt, past), bias_new.reshape(db, nt, LANES),
                              k, v, ck, cv, i, pps, nt)
        o = o.reshape(db, nt, N_HEADS, N_KV_HEADS, HEAD_DIM)
        idx = (jnp.arange(N_HEADS) // KV_GROUP)[None, None, :, None, None]
        b_out = jnp.take_along_axis(o, jnp.broadcast_to(idx, (db, nt, N_HEADS, 1, HEAD_DIM)), axis=3)
        b_out = b_out.reshape(n_s, N_HEADS * HEAD_DIM).astype(BF16)
        hs = _mlp_call(hs, a_out, b_out, p_sample[i].reshape(n_s, D_PLE), *mlp_w, alpha, n_s)
        outs[4].append(k.reshape(db, nt, N_KV_HEADS, HEAD_DIM))
        outs[5].append(v.reshape(db, nt, N_KV_HEADS, HEAD_DIM))
        outs[6].append(ki4[:, :IDX_DIM].reshape(db, nt, IDX_DIM))
        outs[7].append(vn.reshape(db, nt, A_WIDTH))

    st = [jnp.stack(o) for o in outs]
    return (hp.reshape(nb, seq, D_MODEL), hs.reshape(db, nt, D_MODEL),
            st[0], st[1], st[2], st[3], st[4], st[5], st[6], st[7])
```

```python
import functools

import jax
import jax.numpy as jnp
from jax import lax
from jax.experimental import pallas as pl
from jax.experimental.pallas import tpu as pltpu

F32 = jnp.float32
BF16 = jnp.bfloat16
I32 = jnp.int32
I16 = jnp.int16

D_MODEL = 1024
CHUNK = 128
A_WIDTH = 512
A_GROUPS = 4
A_GW = A_WIDTH // A_GROUPS
HEAD_DIM = 64
N_HEADS = 8
N_KV_HEADS = 2
KV_GROUP = N_HEADS // N_KV_HEADS
IDX_HEADS = 8
IDX_DIM = 32
TOPK_MAX = 256
ROPE_THETA = 10000.0
D_FF = 4 * D_MODEL
D_PLE = 256
LN_EPS = 1e-5

LANES = 128
PACK = 16
VMEM_LIMIT = 56 * 1024 * 1024

INT_MIN = -(2 ** 31)
I16_MIN = -(2 ** 15)
NEG = -1e30

C_AU, C_AV, C_Q, C_K, C_V, C_QI, C_KI, C_WI = 0, 512, 1024, 1536, 1664, 1792, 2048, 2176
C_QS, C_KS, C_QIS, C_KIS, C_END = 2304, 2816, 2944, 3200, 3328
N_AU, N_AV, N_K, N_KS, N_KI, N_KIS, N_END = 0, 512, 1024, 1152, 1280, 1408, 1536
T_Q, T_QS, T_K, T_KS, T_V, T_QI, T_QIS, T_KI, T_KIS, T_WI, T_END = (
    0, 512, 1024, 1152, 1280, 1408, 1664, 1920, 1952, 1984, 2000)
WI_ROWS = T_END - T_WI
Q_SCALE_LOG2 = (HEAD_DIM ** -0.5) * 1.4426950408889634
IDX_SCALE = (IDX_DIM ** -0.5) * (IDX_HEADS ** -0.5)

KEY_CHUNK = 256
COUNT_CHUNK = 512

NT_DIMS = (((1,), (1,)), ((), ()))


def _dot(a, b):
    return jnp.dot(a, b, preferred_element_type=F32)


def _dot_nt(a, b):
    return lax.dot_general(a, b, NT_DIMS, preferred_element_type=F32)


def _layer_norm(x, g, b):
    mu = jnp.mean(x, axis=-1, keepdims=True)
    xc = x - mu
    var = jnp.mean(xc * xc, axis=-1, keepdims=True)
    return xc * lax.rsqrt(var + LN_EPS) * g + b


def _gelu(x):
    return 0.5 * x * (1.0 + lax.erf(x * (2.0 ** -0.5)))


def _sortable(x):
    bits = pltpu.bitcast(x, I32)
    return bits ^ ((bits >> 31) & 0x7FFFFFFF)


def _layer_spec(stacked, layer):
    shape = stacked.shape[1:]
    return pl.BlockSpec((None,) + shape, lambda *_: (layer,) + (0,) * len(shape),
                        pipeline_mode=pl.Buffered(1))


def _gated_mixer(au, av, g, lng_ref, lnb_ref, mm_ref, mb_ref, vn_ref, aout_ref):
    rows = au.shape[0]
    lo = g * A_GW
    vn = _layer_norm(av, lng_ref[:, lo:lo + A_GW], lnb_ref[:, lo:lo + A_GW])
    vn_ref[:, lo:lo + A_GW] = vn[rows - vn_ref.shape[0]:, :]
    vnb = vn.astype(BF16)
    for c in range(rows // CHUNK):
        r0 = c * CHUNK
        gate = _dot(mm_ref[g], vnb[r0:r0 + CHUNK, :]) + mb_ref[:, lo:lo + A_GW]
        aout_ref[r0:r0 + CHUNK, lo:lo + A_GW] = (au[r0:r0 + CHUNK, :] * gate).astype(BF16)


def _proj_kernel(h_ref, w_ref, mm_ref, mb_ref, lng_ref, lnb_ref, cq_ref, sq_ref, ci_ref, si_ref,
                 aout_ref, vn_ref, q_ref, k_ref, v_ref, qi_ref, ki_ref, wi_ref):
    hb = h_ref[...].astype(BF16)

    def proj(lo, width=LANES):
        return _dot(hb, w_ref[:, lo:lo + width])

    for g in range(A_GROUPS):
        lo = g * A_GW
        _gated_mixer(_gelu(proj(C_AU + lo)), _gelu(proj(C_AV + lo)), g,
                     lng_ref, lnb_ref, mm_ref, mb_ref, vn_ref, aout_ref)

    cq, sq = cq_ref[...], sq_ref[...]
    ci, si = ci_ref[...], si_ref[...]
    for j in range(N_HEADS * HEAD_DIM // LANES):
        z = proj(C_Q + j * LANES) * cq + proj(C_QS + j * LANES) * sq
        q_ref[:, j * LANES:(j + 1) * LANES] = (z * (HEAD_DIM ** -0.5)).astype(BF16)
    k_ref[...] = proj(C_K) * cq + proj(C_KS) * sq
    v_ref[...] = proj(C_V)
    for j in range(IDX_HEADS * IDX_DIM // LANES):
        z = proj(C_QI + j * LANES) * ci + proj(C_QIS + j * LANES) * si
        qi_ref[:, j * LANES:(j + 1) * LANES] = z.astype(BF16)
    ki_ref[...] = proj(C_KI) * ci + proj(C_KIS) * si
    wi_ref[...] = proj(C_WI)


def _proj_call(h2d, layer, w_ext, mm, mb, lng, lnb, tabs):
    n = h2d.shape[0]
    whole = lambda a: pl.BlockSpec(a.shape, lambda i: (0, 0))
    out_widths = [(A_WIDTH, BF16), (A_WIDTH, F32), (N_HEADS * HEAD_DIM, BF16), (LANES, F32), (LANES, F32),
                  (IDX_HEADS * IDX_DIM, BF16), (LANES, F32), (LANES, F32)]
    return pl.pallas_call(
        _proj_kernel,
        grid=(1,),
        in_specs=[whole(h2d)] + [_layer_spec(a, layer) for a in (w_ext, mm, mb, lng, lnb)]
                 + [whole(t) for t in tabs],
        out_specs=[pl.BlockSpec((n, w), lambda i: (0, 0)) for w, _ in out_widths],
        out_shape=[jax.ShapeDtypeStruct((n, w), dt) for w, dt in out_widths],
        compiler_params=pltpu.CompilerParams(dimension_semantics=("arbitrary",), vmem_limit_bytes=VMEM_LIMIT),
        name="proj",
    )(h2d, w_ext, mm, mb, lng, lnb, *tabs)


def _proj_prompt_kernel(h_ref, wn_ref, wt_ref, mm_ref, mb_ref, lng_ref, lnb_ref,
                        cq_ref, sq_ref, ci_ref, si_ref, cqt_ref, sqt_ref, cit_ref, sit_ref,
                        aout_ref, vn_ref, qt_ref, kb_ref, kt_ref, vt_ref, qit_ref, kib_ref, kit_ref, wit_ref):
    hb = h_ref[...].astype(BF16)
    wide = 2 * LANES

    for gp in range(A_GROUPS // 2):
        au2 = _gelu(_dot(hb, wn_ref[:, N_AU + gp * wide:N_AU + (gp + 1) * wide]))
        av2 = _gelu(_dot(hb, wn_ref[:, N_AV + gp * wide:N_AV + (gp + 1) * wide]))
        for e in range(2):
            _gated_mixer(au2[:, e * A_GW:(e + 1) * A_GW], av2[:, e * A_GW:(e + 1) * A_GW], 2 * gp + e,
                         lng_ref, lnb_ref, mm_ref, mb_ref, vn_ref, aout_ref)

    z = _dot(hb, wn_ref[:, N_K:N_K + wide])
    kb_ref[...] = (z[:, :LANES] * cq_ref[...] + z[:, LANES:] * sq_ref[...]).astype(BF16)
    z = _dot(hb, wn_ref[:, N_KI:N_KI + wide])
    kib_ref[...] = (z[:, :LANES] * ci_ref[...] + z[:, LANES:] * si_ref[...]).astype(BF16)

    zt = _dot_nt(wt_ref[...], hb)
    cqt, sqt = cqt_ref[...], sqt_ref[...]
    cit, sit = cit_ref[...], sit_ref[...]
    for j in range(N_HEADS * HEAD_DIM // LANES):
        r = j * LANES
        q = zt[T_Q + r:T_Q + r + LANES] * cqt + zt[T_QS + r:T_QS + r + LANES] * sqt
        qt_ref[r:r + LANES, :] = (q * Q_SCALE_LOG2).astype(BF16)
    kt_ref[...] = zt[T_K:T_K + LANES] * cqt + zt[T_KS:T_KS + LANES] * sqt
    vt_ref[...] = zt[T_V:T_V + LANES]
    for j in range(IDX_HEADS * IDX_DIM // LANES):
        r = j * LANES
        qi = zt[T_QI + r:T_QI + r + LANES] * cit + zt[T_QIS + r:T_QIS + r + LANES] * sit
        qit_ref[r:r + LANES, :] = qi.astype(BF16)
    kit_ref[...] = zt[T_KI:T_KI + IDX_DIM] * cit[:IDX_DIM] + zt[T_KIS:T_KIS + IDX_DIM] * sit[:IDX_DIM]
    wit_ref[...] = zt[T_WI:T_END]


def _proj_prompt_call(h2d, layer, wn, wt, mm, mb, lng, lnb, tabs, tabs_t, nb, seq, row_tile):
    n = nb * seq
    ns = seq // row_tile
    row_map = lambda b, s: (b * ns + s, 0)
    col_map = lambda b, s: (0, b * ns + s)
    bcol_map = lambda b, s: (b, 0, s)
    hd, idd = N_HEADS * HEAD_DIM, IDX_HEADS * IDX_DIM
    out_specs = [pl.BlockSpec((row_tile, A_WIDTH), row_map),
                 pl.BlockSpec((None, CHUNK, A_WIDTH), lambda b, s: (b, 0, 0)),
                 pl.BlockSpec((hd, row_tile), col_map),
                 pl.BlockSpec((row_tile, LANES), row_map),
                 pl.BlockSpec((None, LANES, row_tile), bcol_map),
                 pl.BlockSpec((None, LANES, row_tile), bcol_map),
                 pl.BlockSpec((idd, row_tile), col_map),
                 pl.BlockSpec((row_tile, LANES), row_map),
                 pl.BlockSpec((None, IDX_DIM, row_tile), bcol_map),
                 pl.BlockSpec((WI_ROWS, row_tile), col_map)]
    out_shape = [jax.ShapeDtypeStruct((n, A_WIDTH), BF16), jax.ShapeDtypeStruct((nb, CHUNK, A_WIDTH), F32),
                 jax.ShapeDtypeStruct((hd, n), BF16),
                 jax.ShapeDtypeStruct((n, LANES), BF16),
                 jax.ShapeDtypeStruct((nb, LANES, seq), F32),
                 jax.ShapeDtypeStruct((nb, LANES, seq), F32),
                 jax.ShapeDtypeStruct((idd, n), BF16),
                 jax.ShapeDtypeStruct((n, LANES), BF16),
                 jax.ShapeDtypeStruct((nb, IDX_DIM, seq), F32),
                 jax.ShapeDtypeStruct((WI_ROWS, n), F32)]
    return pl.pallas_call(
        _proj_prompt_kernel,
        grid=(nb, ns),
        in_specs=[pl.BlockSpec((row_tile, D_MODEL), row_map)]
                 + [_layer_spec(a, layer) for a in (wn, wt, mm, mb, lng, lnb)]
                 + [pl.BlockSpec((row_tile, LANES), lambda b, s: (s, 0))] * 4
                 + [pl.BlockSpec((LANES, row_tile), lambda b, s: (0, s))] * 4,
        out_specs=out_specs,
        out_shape=out_shape,
        compiler_params=pltpu.CompilerParams(dimension_semantics=("arbitrary", "arbitrary"),
                                             vmem_limit_bytes=VMEM_LIMIT),
        name="proj_prompt",
    )(h2d, wn, wt, mm, mb, lng, lnb, *tabs, *tabs_t)


def _count16(ref, nsteps, cand, strict=False):
    one, zero = jnp.ones(cand.shape, I16), jnp.zeros(cand.shape, I16)
    n_acc = 4

    def body(c, accs):
        off = pl.multiple_of(c * COUNT_CHUNK, COUNT_CHUNK)
        blk = ref[pl.ds(off, COUNT_CHUNK), :]
        accs = list(accs)
        for j in range(COUNT_CHUNK // PACK):
            kk = blk[j * PACK:(j + 1) * PACK, :]
            hit = (kk > cand) if strict else (kk >= cand)
            accs[j % n_acc] = accs[j % n_acc] + jnp.where(hit, one, zero)
        return tuple(accs)
    accs = lax.fori_loop(0, nsteps, body, (zero,) * n_acc)
    tot = (accs[0] + accs[1]) + (accs[2] + accs[3])
    return jnp.sum(tot.astype(F32), axis=0, keepdims=True)


def _search16(ref, nsteps, target):
    qw = ref.shape[1]

    def bit_body(i, prefix):
        cand = prefix + lax.shift_left(jnp.int32(1), 15 - i)
        cand16 = jnp.broadcast_to(cand, (PACK, qw)).astype(I16)
        return jnp.where(_count16(ref, nsteps, cand16) >= target, cand, prefix)
    return lax.fori_loop(0, 16, bit_body, jnp.full((1, qw), I16_MIN, I32))


def _prompt_attn_kernel(qit_ref, wit_ref, qt_ref, kib_ref, kb_ref, vt_ref, o_ref,
                        keys_ref, hi_ref, lo_ref, rhs_ref, qz_ref, acc_ref, *, topk):
    qw = o_ref.shape[0]
    qb = pl.program_id(1)
    nch = (qb * qw + qw + KEY_CHUNK - 1) // KEY_CHUNK
    nsteps = (nch * KEY_CHUNK + COUNT_CHUNK - 1) // COUNT_CHUNK
    qpos = qb * qw + lax.broadcasted_iota(I32, (KEY_CHUNK, qw), 1)
    krow = lax.broadcasted_iota(I32, (KEY_CHUNK, qw), 0)

    rhs_ref[...] = jnp.zeros(rhs_ref.shape, BF16)
    qz_ref[...] = jnp.zeros(qz_ref.shape, BF16)
    slots = LANES // IDX_DIM
    for h in range(IDX_HEADS):
        r = (h % slots) * IDX_DIM
        rhs_ref[r:r + IDX_DIM, h * qw:(h + 1) * qw] = qit_ref[h * IDX_DIM:(h + 1) * IDX_DIM, :]
    for h in range(N_HEADS):
        r = (h // KV_GROUP) * HEAD_DIM
        qz_ref[r:r + HEAD_DIM, h * qw:(h + 1) * qw] = qt_ref[h * HEAD_DIM:(h + 1) * HEAD_DIM, :]
    wsc = wit_ref[...] * IDX_SCALE

    def score_chunk(c, causal_edge):
        off = pl.multiple_of(c * KEY_CHUNK, KEY_CHUNK)
        kc = kib_ref[pl.ds(off, KEY_CHUNK), :]
        acc = None
        for h in range(IDX_HEADS):
            s = _dot(kc, rhs_ref[:, h * qw:(h + 1) * qw])
            t = jnp.maximum(s, 0.0) * wsc[h:h + 1, :]
            acc = t if acc is None else acc + t
        key = _sortable(acc)
        if causal_edge:
            key = jnp.where(krow + off <= qpos, key, INT_MIN)
        keys_ref[pl.ds(off, KEY_CHUNK), :] = key
        hi_ref[pl.ds(off, KEY_CHUNK), :] = (key >> 16).astype(I16)
        lo_ref[pl.ds(off, KEY_CHUNK), :] = (key ^ 0x8000).astype(I16)

    def score_body(c, carry):
        score_chunk(c, False)
        return carry
    lax.fori_loop(0, nch - 1, score_body, 0)
    score_chunk(nch - 1, True)

    @pl.when(nch * KEY_CHUNK < nsteps * COUNT_CHUNK)
    def _():
        off = pl.multiple_of(nch * KEY_CHUNK, KEY_CHUNK)
        fill = jnp.full((KEY_CHUNK, qw), I16_MIN, I16)
        hi_ref[pl.ds(off, KEY_CHUNK), :] = fill
        lo_ref[pl.ds(off, KEY_CHUNK), :] = fill

    p_hi = _search16(hi_ref, nsteps, float(topk))
    p16 = jnp.broadcast_to(p_hi, (PACK, qw)).astype(I16)
    room = topk - _count16(hi_ref, nsteps, p16, strict=True)

    def low_body(c, carry):
        off = pl.multiple_of(c * COUNT_CHUNK, COUNT_CHUNK)
        lowest = jnp.full((PACK, qw), I16_MIN, I16)
        his, los = hi_ref[pl.ds(off, COUNT_CHUNK), :], lo_ref[pl.ds(off, COUNT_CHUNK), :]
        lo_ref[pl.ds(off, COUNT_CHUNK), :] = jnp.concatenate(
            [jnp.where(his[j * PACK:(j + 1) * PACK, :] == p16, los[j * PACK:(j + 1) * PACK, :], lowest)
             for j in range(COUNT_CHUNK // PACK)], axis=0)
        return carry
    lax.fori_loop(0, nsteps, low_body, 0)
    p_lo = _search16(lo_ref, nsteps, room)
    thr = jnp.maximum(p_hi * 65536 + (p_lo + 2 ** 15), INT_MIN + 1)

    def count32(strict):
        def body(c, acc):
            off = pl.multiple_of(c * KEY_CHUNK, KEY_CHUNK)
            kk = keys_ref[pl.ds(off, KEY_CHUNK), :]
            hit = (kk > thr) if strict else (kk >= thr)
            return acc + jnp.sum(jnp.where(hit, 1.0, 0.0), axis=0, keepdims=True)
        return lax.fori_loop(0, nch, body, jnp.zeros((1, qw), F32))

    @pl.when(jnp.max(count32(False)) > topk)
    def _():
        spare = topk - count32(True)
        r_i = lax.broadcasted_iota(I32, (KEY_CHUNK, KEY_CHUNK), 0)
        c_i = lax.broadcasted_iota(I32, (KEY_CHUNK, KEY_CHUNK), 1)
        lower = jnp.where(c_i <= r_i, 1.0, 0.0).astype(BF16)

        def body(c, seen):
            off = pl.multiple_of(c * KEY_CHUNK, KEY_CHUNK)
            kk = keys_ref[pl.ds(off, KEY_CHUNK), :]
            tie = kk == thr
            rank = seen + _dot(lower, jnp.where(tie, 1.0, 0.0).astype(BF16))
            keys_ref[pl.ds(off, KEY_CHUNK), :] = jnp.where(tie & (rank > spare), thr - 1, kk)
            return rank[KEY_CHUNK - 1:KEY_CHUNK, :]
        lax.fori_loop(0, nch, body, jnp.zeros((1, qw), F32))

    acc_ref[...] = jnp.zeros(acc_ref.shape, F32)
    ones_rows = jnp.where(lax.broadcasted_iota(I32, (PACK, KEY_CHUNK), 0) == 0, 1.0, 0.0).astype(BF16)

    def attn_body(c, ms):
        ms = list(ms)
        off = pl.multiple_of(c * KEY_CHUNK, KEY_CHUNK)
        sel = keys_ref[pl.ds(off, KEY_CHUNK), :] >= thr
        kb = kb_ref[pl.ds(off, KEY_CHUNK), :]
        vt = vt_ref[:, pl.ds(off, KEY_CHUNK)].astype(BF16)
        for g in range(N_KV_HEADS):
            ps, alphas = [], []
            for j in range(KV_GROUP):
                h = g * KV_GROUP + j
                s = jnp.where(sel, _dot(kb, qz_ref[:, h * qw:(h + 1) * qw]), NEG)
                m_new = jnp.maximum(ms[h], jnp.max(s, axis=0, keepdims=True))
                alphas.append(jnp.exp2(ms[h] - m_new))
                ms[h] = m_new
                ps.append(jnp.exp2(s - m_new).astype(BF16))
            vg = jnp.concatenate([vt[g * HEAD_DIM:(g + 1) * HEAD_DIM, :], ones_rows], axis=0)
            pv = _dot(vg, jnp.concatenate(ps, axis=1))
            acc_ref[g] = acc_ref[g] * jnp.concatenate(alphas, axis=1) + pv
        return tuple(ms)
    lax.fori_loop(0, nch, attn_body, (jnp.full((1, qw), NEG, F32),) * N_HEADS)

    for p in range(N_HEADS // 2):
        halves = []
        for h in (2 * p, 2 * p + 1):
            g, j = h // KV_GROUP, h % KV_GROUP
            acc = acc_ref[g][:, j * qw:(j + 1) * qw]
            halves.append(acc[:HEAD_DIM, :] / acc[HEAD_DIM:HEAD_DIM + 1, :])
        o_ref[:, p * LANES:(p + 1) * LANES] = jnp.concatenate(halves, axis=0).T.astype(BF16)


def _prompt_attn_call(qit, wit, qt, kib, kb, vt, nb, seq, topk, qw):
    nq = seq // qw
    hd, idd = N_HEADS * HEAD_DIM, IDX_HEADS * IDX_DIM
    col = lambda b, i: (0, b * nq + i)
    kern = functools.partial(_prompt_attn_kernel, topk=topk)
    return pl.pallas_call(
        kern,
        grid=(nb, nq),
        in_specs=[pl.BlockSpec((idd, qw), col),
                  pl.BlockSpec((WI_ROWS, qw), col),
                  pl.BlockSpec((hd, qw), col),
                  pl.BlockSpec((seq, LANES), lambda b, i: (b, 0)),
                  pl.BlockSpec((seq, LANES), lambda b, i: (b, 0)),
                  pl.BlockSpec((None, LANES, seq), lambda b, i: (b, 0, 0))],
        out_specs=pl.BlockSpec((qw, hd), lambda b, i: (b * nq + i, 0)),
        out_shape=jax.ShapeDtypeStruct((nb * seq, hd), BF16),
        scratch_shapes=[pltpu.VMEM((seq, qw), I32),
                        pltpu.VMEM((seq, qw), I16),
                        pltpu.VMEM((seq, qw), I16),
                        pltpu.VMEM((LANES, IDX_HEADS * qw), BF16),
                        pltpu.VMEM((LANES, N_HEADS * qw), BF16),
                        pltpu.VMEM((N_KV_HEADS, HEAD_DIM + PACK, KV_GROUP * qw), F32)],
        compiler_params=pltpu.CompilerParams(dimension_semantics=("arbitrary", "arbitrary"),
                                             vmem_limit_bytes=VMEM_LIMIT),
        name="prompt_attn",
    )(qit, wit, qt, kib, kb, vt)


def _mlp_kernel(h_ref, a_ref, b_ref, pe_ref, wo_ref, g1_ref, b1_ref, wf1_ref, wf2_ref, wg_ref, wp_ref,
                g2_ref, b2_ref, out_ref, *, alpha, ff_chunk):
    mix = _dot(a_ref[...], wo_ref[0:A_WIDTH, :]) + _dot(b_ref[...], wo_ref[A_WIDTH:2 * A_WIDTH, :])
    h1 = _layer_norm(alpha * h_ref[...] + mix, g1_ref[...], b1_ref[...])
    h1b = h1.astype(BF16)
    ff = None
    for c in range(D_FF // ff_chunk):
        hid = jnp.maximum(_dot(h1b, wf1_ref[:, c * ff_chunk:(c + 1) * ff_chunk]), 0.0)
        part = _dot((hid * hid).astype(BF16), wf2_ref[c * ff_chunk:(c + 1) * ff_chunk, :])
        ff = part if ff is None else ff + part
    ple = jax.nn.sigmoid(_dot(h1b, wg_ref[...])) * _dot(pe_ref[...].astype(BF16), wp_ref[...])
    out_ref[...] = _layer_norm(alpha * h1 + ff + ple, g2_ref[...], b2_ref[...])


def _mlp_call(h2d, a_out, b_out, pe, layer, consts, alpha, row_tile):
    n = h2d.shape[0]
    row = lambda i: (i, 0)
    kern = functools.partial(_mlp_kernel, alpha=alpha, ff_chunk=1024)
    return pl.pallas_call(
        kern,
        grid=(n // row_tile,),
        in_specs=[pl.BlockSpec((row_tile, D_MODEL), row),
                  pl.BlockSpec((row_tile, A_WIDTH), row),
                  pl.BlockSpec((row_tile, A_WIDTH), row),
                  pl.BlockSpec((None, row_tile, D_PLE), lambda i: (layer, i, 0))]
                 + [_layer_spec(c, layer) for c in consts],
        out_specs=pl.BlockSpec((row_tile, D_MODEL), row),
        out_shape=jax.ShapeDtypeStruct((n, D_MODEL), F32),
        compiler_params=pltpu.CompilerParams(dimension_semantics=("arbitrary",),
                                             vmem_limit_bytes=VMEM_LIMIT),
        name="mlp",
    )(h2d, a_out, b_out, pe, *consts)


def _head_masked_queries(qi_ref):
    lane_head = lax.broadcasted_iota(I32, (CHUNK, LANES), 1) >> 5
    heads_per_col = LANES // IDX_DIM
    out = []
    for h in range(IDX_HEADS):
        col = qi_ref[:, (h // heads_per_col) * LANES:(h // heads_per_col + 1) * LANES]
        out.append(jnp.where(lane_head == (h % heads_per_col), col, jnp.zeros_like(col)))
    return out


def _index_scores(qh, wsc, kc):
    acc = None
    for h in range(IDX_HEADS):
        s = _dot_nt(qh[h], kc)
        t = jnp.maximum(s, 0.0) * wsc[:, h:h + 1]
        acc = t if acc is None else acc + t
    return acc


def _count_ge(keys_ref, nch, cw, cand, strict=False):
    def body(c, acc):
        off = pl.multiple_of(c * cw, cw)
        for j in range(cw // LANES):
            kk = keys_ref[:, pl.ds(off + j * LANES, LANES)]
            hit = (kk > cand) if strict else (kk >= cand)
            acc = acc + jnp.where(hit, 1.0, 0.0)
        return acc
    acc = lax.fori_loop(0, nch, body, jnp.zeros((CHUNK, LANES), F32))
    cnt = jnp.sum(acc, axis=-1, keepdims=True)
    return jnp.broadcast_to(cnt, (CHUNK, LANES))


def _select_threshold(keys_ref, nch, cw, topk):
    def bit_body(i, prefix):
        cand = prefix + lax.shift_left(jnp.int32(1), 31 - i)
        cnt = _count_ge(keys_ref, nch, cw, cand)
        return jnp.where(cnt >= topk, cand, prefix)
    prefix = lax.fori_loop(0, 32, bit_body, jnp.full((CHUNK, LANES), INT_MIN, I32))
    thr = jnp.maximum(prefix, INT_MIN + 1)
    cnt_ge = _count_ge(keys_ref, nch, cw, thr)

    @pl.when(jnp.max(cnt_ge) > topk)
    def _():
        room = topk - _count_ge(keys_ref, nch, cw, thr, strict=True)
        r_i = lax.broadcasted_iota(I32, (LANES, LANES), 0)
        c_i = lax.broadcasted_iota(I32, (LANES, LANES), 1)
        upper = jnp.where(r_i <= c_i, 1.0, 0.0).astype(BF16)

        def body(c, seen):
            off = pl.multiple_of(c * cw, cw)
            for j in range(cw // LANES):
                kk = keys_ref[:, pl.ds(off + j * LANES, LANES)]
                tie = kk == thr
                rank = seen + _dot(jnp.where(tie, 1.0, 0.0).astype(BF16), upper)
                keys_ref[:, pl.ds(off + j * LANES, LANES)] = jnp.where(tie & (rank > room), thr - 1, kk)
                seen = jnp.broadcast_to(rank[:, LANES - 1:LANES], (CHUNK, LANES))
            return seen
        lax.fori_loop(0, nch, body, jnp.zeros((CHUNK, LANES), F32))

    return thr


def _sample_idx_kernel(pt_ref, qs_ref, w_ref, *refs, pages_per_step):
    page_refs, out_ref = refs[:pages_per_step], refs[pages_per_step]
    qs = qs_ref[0]
    w = w_ref[0]
    nt = out_ref.shape[1]
    for j in range(pages_per_step):
        kp = page_refs[j][...].astype(BF16)
        t = jnp.maximum(_dot(qs, kp), 0.0) * w
        for q in range(nt):
            sc = jnp.sum(t[q * IDX_HEADS:(q + 1) * IDX_HEADS, :], axis=0, keepdims=True)
            out_ref[0, q:q + 1, j * LANES:(j + 1) * LANES] = _sortable(sc)


def _sample_idx_call(page_table, qs, wrep, cache_kidx, layer, pps):
    db, npg = page_table.shape
    nt = qs.shape[1] // IDX_HEADS
    page = cache_kidx.shape[3]
    kern = functools.partial(_sample_idx_kernel, pages_per_step=pps)

    def page_spec(j):
        return pl.BlockSpec((None, None, IDX_DIM, page), lambda b, s, pt: (layer, pt[b, s * pps + j], 0, 0))

    grid_spec = pltpu.PrefetchScalarGridSpec(
        num_scalar_prefetch=1,
        grid=(db, npg // pps),
        in_specs=[pl.BlockSpec((1,) + qs.shape[1:], lambda b, s, pt: (b, 0, 0)),
                  pl.BlockSpec((1,) + wrep.shape[1:], lambda b, s, pt: (b, 0, 0))]
                 + [page_spec(j) for j in range(pps)],
        out_specs=pl.BlockSpec((1, nt, pps * page), lambda b, s, pt: (b, 0, s)),
    )
    return pl.pallas_call(
        kern, grid_spec=grid_spec,
        out_shape=jax.ShapeDtypeStruct((db, nt, npg * page), I32),
        compiler_params=pltpu.CompilerParams(dimension_semantics=("arbitrary", "arbitrary")),
        name="sample_idx",
    )(page_table, qs, wrep, *([cache_kidx] * pps))


def _sample_select_kernel(kp_ref, qi_ref, wi_ref, ki_ref, bp_ref, bn_ref, keys_ref, *, topk, cw, nt):
    past = kp_ref.shape[1]
    width = keys_ref.shape[1]
    nch = width // cw
    keys_ref[:, 0:past] = kp_ref[...]
    qh = _head_masked_queries(qi_ref)
    wsc = wi_ref[...] * IDX_SCALE
    key = _sortable(_index_scores(qh, wsc, ki_ref[...].astype(BF16)))
    r_i = lax.broadcasted_iota(I32, (CHUNK, LANES), 0)
    c_i = lax.broadcasted_iota(I32, (CHUNK, LANES), 1)
    visible = ((r_i // nt) == (c_i // nt)) & ((c_i % nt) <= (r_i % nt))
    keys_ref[:, past:past + LANES] = jnp.where(visible, key, INT_MIN)
    if width > past + LANES:
        keys_ref[:, past + LANES:width] = jnp.full((CHUNK, width - past - LANES), INT_MIN, I32)

    thr = _select_threshold(keys_ref, nch, cw, topk)

    for c in range(past // LANES):
        bp_ref[:, c * LANES:(c + 1) * LANES] = jnp.where(
            keys_ref[:, c * LANES:(c + 1) * LANES] >= thr, 0.0, NEG)
    bn_ref[...] = jnp.where(keys_ref[:, past:past + LANES] >= thr, 0.0, NEG)


def _sample_select_call(keys_past, qi, wi, ki4, topk, nt, cw):
    rows, past = keys_past.shape
    width = -(-(past + LANES) // cw) * cw
    kern = functools.partial(_sample_select_kernel, topk=topk, cw=cw, nt=nt)
    return pl.pallas_call(
        kern,
        out_shape=[jax.ShapeDtypeStruct((rows, past), F32), jax.ShapeDtypeStruct((rows, LANES), F32)],
        scratch_shapes=[pltpu.VMEM((rows, width), I32)],
        compiler_params=pltpu.CompilerParams(vmem_limit_bytes=VMEM_LIMIT),
        name="sample_select",
    )(keys_past, qi, wi, ki4)


def _sample_attn_kernel(pt_ref, qz_ref, bp_ref, bn_ref, kn_ref, vn_ref, *refs, pages_per_step, nt):
    pps = pages_per_step
    k_refs, v_refs = refs[:pps], refs[pps:2 * pps]
    o_ref, m_ref, l_ref, acc_ref = refs[2 * pps:]
    step = pl.program_id(1)
    qz = qz_ref[0]
    rows = qz.shape[0]
    hpq = rows // nt

    @pl.when(step == 0)
    def _():
        m_ref[...] = jnp.full(m_ref.shape, NEG, F32)
        l_ref[...] = jnp.zeros(l_ref.shape, F32)
        acc_ref[...] = jnp.zeros(acc_ref.shape, F32)

    def update(s, pv_fn, bias):
        width = s.shape[1]
        s = jnp.concatenate(
            [s[q * hpq:(q + 1) * hpq, :] + bias[q:q + 1, :] for q in range(nt)], axis=0)
        m_old = m_ref[...]
        m_new = jnp.maximum(m_old, jnp.max(s, axis=-1, keepdims=True))
        alpha = jnp.exp(m_old - m_new)
        pr = jnp.exp(s - jnp.concatenate([m_new] * (width // LANES), axis=1))
        l_ref[...] = alpha * l_ref[...] + jnp.sum(pr, axis=-1, keepdims=True)
        m_ref[...] = m_new
        acc_ref[...] = alpha * acc_ref[...] + pv_fn(pr.astype(BF16))

    kb = jnp.concatenate([r[...].astype(BF16) for r in k_refs], axis=1)
    vb = jnp.concatenate([r[...].astype(BF16) for r in v_refs], axis=1)
    update(_dot(qz, kb), lambda pr: _dot_nt(pr, vb), bp_ref[0])

    @pl.when(step == pl.num_programs(1) - 1)
    def _():
        vn = vn_ref[...].astype(BF16)
        update(_dot_nt(qz, kn_ref[...].astype(BF16)), lambda pr: _dot(pr, vn), bn_ref[0])
        o_ref[0] = acc_ref[...] / l_ref[...]


def _sample_attn_call(page_table, qz, bias_past, bias_new, k_new, v_new, cache_k, cache_v, layer, pps, nt):
    db, npg = page_table.shape
    page = cache_k.shape[3]
    rows = qz.shape[1]
    kern = functools.partial(_sample_attn_kernel, pages_per_step=pps, nt=nt)

    def page_spec(j):
        return pl.BlockSpec((None, None, LANES, page), lambda b, s, pt: (layer, pt[b, s * pps + j], 0, 0))

    grid_spec = pltpu.PrefetchScalarGridSpec(
        num_scalar_prefetch=1,
        grid=(db, npg // pps),
        in_specs=[pl.BlockSpec((1, rows, LANES), lambda b, s, pt: (b, 0, 0)),
                  pl.BlockSpec((1, nt, pps * page), lambda b, s, pt: (b, 0, s)),
                  pl.BlockSpec((1, nt, LANES), lambda b, s, pt: (b, 0, 0)),
                  pl.BlockSpec(k_new.shape, lambda b, s, pt: (0, 0)),
                  pl.BlockSpec(v_new.shape, lambda b, s, pt: (0, 0))]
                 + [page_spec(j) for j in range(pps)] * 2,
        out_specs=pl.BlockSpec((1, rows, LANES), lambda b, s, pt: (b, 0, 0)),
        scratch_shapes=[pltpu.VMEM((rows, LANES), F32)] * 3,
    )
    return pl.pallas_call(
        kern, grid_spec=grid_spec,
        out_shape=jax.ShapeDtypeStruct((db, rows, LANES), F32),
        compiler_params=pltpu.CompilerParams(dimension_semantics=("arbitrary", "arbitrary")),
        name="sample_attn",
    )(page_table, qz, bias_past, bias_new, k_new, v_new, *([cache_k] * pps), *([cache_v] * pps))


def _swap_halves(w, head_dim):
    lead = w.shape[:-1]
    nh = w.shape[-1] // head_dim
    return w.reshape(lead + (nh, 2, head_dim // 2))[..., ::-1, :].reshape(lead + (nh * head_dim,))


def _projection_weights(w_in):
    sizes = (A_WIDTH, A_WIDTH, N_HEADS * HEAD_DIM, N_KV_HEADS * HEAD_DIM, N_KV_HEADS * HEAD_DIM,
             IDX_HEADS * IDX_DIM, IDX_DIM, IDX_HEADS)
    pts = [sum(sizes[:i + 1]) for i in range(len(sizes) - 1)]
    au, av, q, k, v, qi, ki, wi = jnp.split(w_in.astype(BF16), pts, axis=-1)
    rep = LANES // IDX_DIM
    q_s, k_s, qi_s, ki_s = (_swap_halves(q, HEAD_DIM), _swap_halves(k, HEAD_DIM),
                            _swap_halves(qi, IDX_DIM), _swap_halves(ki, IDX_DIM))
    ki4, ki4_s = jnp.tile(ki, (1, 1, rep)), jnp.tile(ki_s, (1, 1, rep))
    pad_cols = lambda w, n: jnp.pad(w, ((0, 0), (0, 0), (0, n - w.shape[-1])))
    ext = jnp.concatenate([au, av, q, k, v, qi, ki4, pad_cols(wi, LANES), q_s, k_s, qi_s, ki4_s], axis=-1)
    wn = jnp.concatenate([au, av, k, k_s, ki4, ki4_s], axis=-1)
    wt = jnp.swapaxes(jnp.concatenate([q, q_s, k, k_s, v, qi, qi_s, ki, ki_s, pad_cols(wi, WI_ROWS)], axis=-1),
                      1, 2)
    assert ext.shape[-1] == C_END and wn.shape[-1] == N_END and wt.shape[1] == T_END
    return ext, wn, wt


def _rope_tables(pos, head_dim):
    half = head_dim // 2
    inv = ROPE_THETA ** (-jnp.arange(half, dtype=F32) / half)
    ang = pos.astype(F32)[:, None] * inv[None, :]
    cos, sin = jnp.cos(ang), jnp.sin(ang)
    reps = LANES // head_dim
    return (jnp.tile(jnp.concatenate([cos, cos], axis=-1), (1, reps)),
            jnp.tile(jnp.concatenate([-sin, sin], axis=-1), (1, reps)))


def _pick(n, prefs):
    for p in prefs:
        if n % p == 0:
            return p
    return n


def kernel(x_prompt, x_sample, p_prompt, p_sample, cache_k, cache_v, cache_kidx, page_table, w_in, sgu_ln_g,
           sgu_ln_b, sgu_w, sgu_b, w_o, ln1_g, ln1_b, w_ff1, w_ff2, w_ple_gate, w_ple_proj, ln2_g, ln2_b):
    nb, seq, _ = x_prompt.shape
    db, nt, _ = x_sample.shape
    depth = w_in.shape[0]
    npg = page_table.shape[1]
    page = cache_k.shape[2]
    past = npg * page
    alpha = (2.0 * depth) ** 0.25
    topk_p = min(TOPK_MAX, seq // 4)
    topk_s = min(TOPK_MAX, (past + nt) // 4)
    n_p, n_s = nb * seq, db * nt
    assert n_s == CHUNK and seq % COUNT_CHUNK == 0 and page == LANES

    w_ext, w_n, w_t = _projection_weights(w_in)
    wo_b, wf1_b, wf2_b = w_o.astype(BF16), w_ff1.astype(BF16), w_ff2.astype(BF16)
    wg_b, wp_b = w_ple_gate.astype(BF16), w_ple_proj.astype(BF16)
    tril = jnp.tril(jnp.ones((CHUNK, CHUNK), F32))
    mm_p = (sgu_w * tril).astype(BF16)
    mb_p = jnp.repeat(jnp.swapaxes(sgu_b, 1, 2), A_GW, axis=2)
    r_i = jnp.arange(CHUNK)
    same_req = (r_i[:, None] // nt == r_i[None, :] // nt) & (r_i[None, :] % nt <= r_i[:, None] % nt)
    mm_s = jnp.where(same_req, sgu_w[:, :, r_i % nt][:, :, :, r_i % nt], 0.0).astype(BF16)
    mb_s = jnp.tile(jnp.repeat(jnp.swapaxes(sgu_b[:, :, :nt], 1, 2), A_GW, axis=2), (1, CHUNK // nt, 1))
    lng = sgu_ln_g.reshape(depth, 1, A_WIDTH)
    lnb = sgu_ln_b.reshape(depth, 1, A_WIDTH)
    mlp_w = (wo_b, ln1_g.reshape(depth, 1, D_MODEL), ln1_b.reshape(depth, 1, D_MODEL), wf1_b, wf2_b, wg_b, wp_b,
             ln2_g.reshape(depth, 1, D_MODEL), ln2_b.reshape(depth, 1, D_MODEL))
    pe_p = p_prompt.reshape(depth, n_p, D_PLE)
    pe_s = p_sample.reshape(depth, n_s, D_PLE)

    pos_p = jnp.arange(seq, dtype=I32)
    pos_s = past + (jnp.arange(n_s, dtype=I32) % nt)
    tabs_p = _rope_tables(pos_p, HEAD_DIM) + _rope_tables(pos_p, IDX_DIM)
    tabs_pt = tuple(t.T for t in tabs_p)
    tabs_s = _rope_tables(pos_s, HEAD_DIM) + _rope_tables(pos_s, IDX_DIM)

    cki = jnp.swapaxes(cache_kidx, 2, 3)
    ck = jnp.transpose(cache_k, (0, 1, 3, 4, 2)).reshape(cache_k.shape[:2] + (LANES, page))
    cv = jnp.transpose(cache_v, (0, 1, 3, 4, 2)).reshape(cache_v.shape[:2] + (LANES, page))

    rt_proj = _pick(seq, (512, 256, 128))
    rt_mlp = _pick(n_p, (512, 256, 128))
    qw_p = _pick(seq, (256, 128))
    cw_s = 256
    pps = _pick(npg, (64, 32, 16, 8, 4, 2, 1))
    head_lo = (jnp.arange(N_HEADS) // KV_GROUP) * HEAD_DIM

    hp = x_prompt.reshape(n_p, D_MODEL)
    hs = x_sample.reshape(n_s, D_MODEL)
    outs = [[] for _ in range(8)]
    for i in range(depth):
        a_out, vn, qt, kb, kt, vt, qit, kib, kit, wit = _proj_prompt_call(
            hp, i, w_n, w_t, mm_p, mb_p, lng, lnb, tabs_p, tabs_pt, nb, seq, rt_proj)
        b_out = _prompt_attn_call(qit, wit, qt, kib, kb, vt, nb, seq, topk_p, qw_p)
        hp = _mlp_call(hp, a_out, b_out, pe_p, i, mlp_w, alpha, rt_mlp)
        to_tokens = lambda a: jnp.transpose(a.reshape(nb, N_KV_HEADS, HEAD_DIM, seq), (0, 3, 1, 2))
        outs[0].append(to_tokens(kt))
        outs[1].append(to_tokens(vt))
        outs[2].append(jnp.swapaxes(kit, 1, 2))
        outs[3].append(vn)

        a_out, vn, q, k, v, qi, ki4, wi = _proj_call(hs, i, w_ext, mm_s, mb_s, lng, lnb, tabs_s)
        qs = qi.reshape(db, nt * IDX_HEADS, IDX_DIM)
        wrep = jnp.broadcast_to((wi[:, :IDX_HEADS] * IDX_SCALE).reshape(db, nt * IDX_HEADS, 1),
                                (db, nt * IDX_HEADS, LANES))
        keys_past = _sample_idx_call(page_table, qs, wrep, cki, i, pps)
        bias_past, bias_new = _sample_select_call(keys_past.reshape(n_s, past), qi, wi, ki4, topk_s, nt, cw_s)
        q4 = q.reshape(db, nt, N_HEADS, 1, HEAD_DIM)
        half = (jnp.arange(N_KV_HEADS) * HEAD_DIM)[None, None, None, :, None]
        qz = jnp.where(half == head_lo[None, None, :, None, None], q4, jnp.zeros_like(q4))
        qz = qz.reshape(db, nt * N_HEADS, LANES)
        o = _sample_attn_call(page_table, qz, bias_past.reshape(db, nt, past), bias_new.reshape(db, nt, LANES),
                              k, v, ck, cv, i, pps, nt)
        o = o.reshape(db, nt, N_HEADS, N_KV_HEADS, HEAD_DIM)
        idx = (jnp.arange(N_HEADS) // KV_GROUP)[None, None, :, None, None]
        b_out = jnp.take_along_axis(o, jnp.broadcast_to(idx, (db, nt, N_HEADS, 1, HEAD_DIM)), axis=3)
        b_out = b_out.reshape(n_s, N_HEADS * HEAD_DIM).astype(BF16)
        hs = _mlp_call(hs, a_out, b_out, pe_s, i, mlp_w, alpha, n_s)
        outs[4].append(k.reshape(db, nt, N_KV_HEADS, HEAD_DIM))
        outs[5].append(v.reshape(db, nt, N_KV_HEADS, HEAD_DIM))
        outs[6].append(ki4[:, :IDX_DIM].reshape(db, nt, IDX_DIM))
        outs[7].append(vn.reshape(db, nt, A_WIDTH))

    st = [jnp.stack(o) for o in outs]
    return (hp.reshape(nb, seq, D_MODEL), hs.reshape(db, nt, D_MODEL),
            st[0], st[1], st[2], st[3], st[4], st[5], st[6], st[7])
```

```python
import functools

import jax
import jax.numpy as jnp
from jax import lax
from jax.experimental import pallas as pl
from jax.experimental.pallas import tpu as pltpu

F32 = jnp.float32
BF16 = jnp.bfloat16
I32 = jnp.int32
I16 = jnp.int16

D_MODEL = 1024
CHUNK = 128
A_WIDTH = 512
A_GROUPS = 4
A_GW = A_WIDTH // A_GROUPS
HEAD_DIM = 64
N_HEADS = 8
N_KV_HEADS = 2
KV_GROUP = N_HEADS // N_KV_HEADS
IDX_HEADS = 8
IDX_DIM = 32
TOPK_MAX = 256
ROPE_THETA = 10000.0
D_FF = 4 * D_MODEL
D_PLE = 256
LN_EPS = 1e-5

LANES = 128
PACK = 16
VMEM_LIMIT = 56 * 1024 * 1024

INT_MIN = -(2 ** 31)
I16_MIN = -(2 ** 15)
NEG = -1e30
DENOM_FLOOR = 2.0 ** -100

C_AU, C_AV, C_Q, C_K, C_V, C_QI, C_KI, C_WI = 0, 512, 1024, 1536, 1664, 1792, 2048, 2176
C_QS, C_KS, C_QIS, C_KIS, C_END = 2304, 2816, 2944, 3200, 3328
N_AU, N_AV, N_K, N_KS, N_KI, N_KIS, N_END = 0, 512, 1024, 1152, 1280, 1408, 1536
T_Q, T_QS, T_K, T_KS, T_V, T_QI, T_QIS, T_KI, T_KIS, T_WI, T_END = (
    0, 512, 1024, 1152, 1280, 1408, 1664, 1920, 1952, 1984, 2000)
WI_ROWS = T_END - T_WI
Q_SCALE_LOG2 = (HEAD_DIM ** -0.5) * 1.4426950408889634
IDX_SCALE = (IDX_DIM ** -0.5) * (IDX_HEADS ** -0.5)

KEY_CHUNK = 256
COUNT_CHUNK = 512

NT_DIMS = (((1,), (1,)), ((), ()))


def _dot(a, b):
    return jnp.dot(a, b, preferred_element_type=F32)


def _dot_nt(a, b):
    return lax.dot_general(a, b, NT_DIMS, preferred_element_type=F32)


def _layer_norm(x, g, b):
    mu = jnp.mean(x, axis=-1, keepdims=True)
    xc = x - mu
    var = jnp.mean(xc * xc, axis=-1, keepdims=True)
    return xc * lax.rsqrt(var + LN_EPS) * g + b


def _gelu(x):
    return 0.5 * x * (1.0 + lax.erf(x * (2.0 ** -0.5)))


def _sortable(x):
    bits = pltpu.bitcast(x, I32)
    return bits ^ ((bits >> 31) & 0x7FFFFFFF)


def _layer_spec(stacked, layer):
    shape = stacked.shape[1:]
    return pl.BlockSpec((None,) + shape, lambda *_: (layer,) + (0,) * len(shape),
                        pipeline_mode=pl.Buffered(1))


def _gated_mixer(au, av, g, lng_ref, lnb_ref, mm_ref, mb_ref, vn_ref, aout_ref):
    rows = au.shape[0]
    lo = g * A_GW
    vn = _layer_norm(av, lng_ref[:, lo:lo + A_GW], lnb_ref[:, lo:lo + A_GW])
    vn_ref[:, lo:lo + A_GW] = vn[rows - vn_ref.shape[0]:, :]
    vnb = vn.astype(BF16)
    for c in range(rows // CHUNK):
        r0 = c * CHUNK
        gate = _dot(mm_ref[g], vnb[r0:r0 + CHUNK, :]) + mb_ref[:, lo:lo + A_GW]
        aout_ref[r0:r0 + CHUNK, lo:lo + A_GW] = (au[r0:r0 + CHUNK, :] * gate).astype(BF16)


def _proj_kernel(h_ref, w_ref, mm_ref, mb_ref, lng_ref, lnb_ref, cq_ref, sq_ref, ci_ref, si_ref,
                 aout_ref, vn_ref, q_ref, k_ref, v_ref, qi_ref, ki_ref, wi_ref):
    hb = h_ref[...].astype(BF16)

    def proj(lo, width=LANES):
        return _dot(hb, w_ref[:, lo:lo + width])

    for g in range(A_GROUPS):
        lo = g * A_GW
        _gated_mixer(_gelu(proj(C_AU + lo)), _gelu(proj(C_AV + lo)), g,
                     lng_ref, lnb_ref, mm_ref, mb_ref, vn_ref, aout_ref)

    cq, sq = cq_ref[...], sq_ref[...]
    ci, si = ci_ref[...], si_ref[...]
    for j in range(N_HEADS * HEAD_DIM // LANES):
        z = proj(C_Q + j * LANES) * cq + proj(C_QS + j * LANES) * sq
        q_ref[:, j * LANES:(j + 1) * LANES] = (z * (HEAD_DIM ** -0.5)).astype(BF16)
    k_ref[...] = proj(C_K) * cq + proj(C_KS) * sq
    v_ref[...] = proj(C_V)
    for j in range(IDX_HEADS * IDX_DIM // LANES):
        z = proj(C_QI + j * LANES) * ci + proj(C_QIS + j * LANES) * si
        qi_ref[:, j * LANES:(j + 1) * LANES] = z.astype(BF16)
    ki_ref[...] = proj(C_KI) * ci + proj(C_KIS) * si
    wi_ref[...] = proj(C_WI)


def _proj_call(h2d, layer, w_ext, mm, mb, lng, lnb, tabs):
    n = h2d.shape[0]
    whole = lambda a: pl.BlockSpec(a.shape, lambda i: (0, 0))
    out_widths = [(A_WIDTH, BF16), (A_WIDTH, F32), (N_HEADS * HEAD_DIM, BF16), (LANES, F32), (LANES, F32),
                  (IDX_HEADS * IDX_DIM, BF16), (LANES, F32), (LANES, F32)]
    return pl.pallas_call(
        _proj_kernel,
        grid=(1,),
        in_specs=[whole(h2d)] + [_layer_spec(a, layer) for a in (w_ext, mm, mb, lng, lnb)]
                 + [whole(t) for t in tabs],
        out_specs=[pl.BlockSpec((n, w), lambda i: (0, 0)) for w, _ in out_widths],
        out_shape=[jax.ShapeDtypeStruct((n, w), dt) for w, dt in out_widths],
        compiler_params=pltpu.CompilerParams(dimension_semantics=("arbitrary",), vmem_limit_bytes=VMEM_LIMIT),
        name="proj",
    )(h2d, w_ext, mm, mb, lng, lnb, *tabs)


def _proj_prompt_kernel(h_ref, wn_ref, wt_ref, mm_ref, mb_ref, lng_ref, lnb_ref,
                        cq_ref, sq_ref, ci_ref, si_ref, cqt_ref, sqt_ref, cit_ref, sit_ref,
                        aout_ref, vn_ref, qt_ref, kb_ref, kt_ref, vt_ref, qit_ref, kib_ref, kit_ref, wit_ref):
    hb = h_ref[...].astype(BF16)
    wide = 2 * LANES

    for gp in range(A_GROUPS // 2):
        au2 = _gelu(_dot(hb, wn_ref[:, N_AU + gp * wide:N_AU + (gp + 1) * wide]))
        av2 = _gelu(_dot(hb, wn_ref[:, N_AV + gp * wide:N_AV + (gp + 1) * wide]))
        for e in range(2):
            _gated_mixer(au2[:, e * A_GW:(e + 1) * A_GW], av2[:, e * A_GW:(e + 1) * A_GW], 2 * gp + e,
                         lng_ref, lnb_ref, mm_ref, mb_ref, vn_ref, aout_ref)

    z = _dot(hb, wn_ref[:, N_K:N_K + wide])
    kb_ref[...] = (z[:, :LANES] * cq_ref[...] + z[:, LANES:] * sq_ref[...]).astype(BF16)
    z = _dot(hb, wn_ref[:, N_KI:N_KI + wide])
    kib_ref[...] = (z[:, :LANES] * ci_ref[...] + z[:, LANES:] * si_ref[...]).astype(BF16)

    zt = _dot_nt(wt_ref[...], hb)
    cqt, sqt = cqt_ref[...], sqt_ref[...]
    cit, sit = cit_ref[...], sit_ref[...]
    for j in range(N_HEADS * HEAD_DIM // LANES):
        r = j * LANES
        q = zt[T_Q + r:T_Q + r + LANES] * cqt + zt[T_QS + r:T_QS + r + LANES] * sqt
        qt_ref[r:r + LANES, :] = (q * Q_SCALE_LOG2).astype(BF16)
    kt_ref[...] = zt[T_K:T_K + LANES] * cqt + zt[T_KS:T_KS + LANES] * sqt
    vt_ref[...] = zt[T_V:T_V + LANES]
    for j in range(IDX_HEADS * IDX_DIM // LANES):
        r = j * LANES
        qi = zt[T_QI + r:T_QI + r + LANES] * cit + zt[T_QIS + r:T_QIS + r + LANES] * sit
        qit_ref[r:r + LANES, :] = qi.astype(BF16)
    kit_ref[...] = zt[T_KI:T_KI + IDX_DIM] * cit[:IDX_DIM] + zt[T_KIS:T_KIS + IDX_DIM] * sit[:IDX_DIM]
    wit_ref[...] = zt[T_WI:T_END]


def _proj_prompt_call(h2d, layer, wn, wt, mm, mb, lng, lnb, tabs, tabs_t, nb, seq, row_tile):
    n = nb * seq
    ns = seq // row_tile
    row_map = lambda b, s: (b * ns + s, 0)
    col_map = lambda b, s: (0, b * ns + s)
    bcol_map = lambda b, s: (b, 0, s)
    hd, idd = N_HEADS * HEAD_DIM, IDX_HEADS * IDX_DIM
    out_specs = [pl.BlockSpec((row_tile, A_WIDTH), row_map),
                 pl.BlockSpec((None, CHUNK, A_WIDTH), lambda b, s: (b, 0, 0)),
                 pl.BlockSpec((hd, row_tile), col_map),
                 pl.BlockSpec((row_tile, LANES), row_map),
                 pl.BlockSpec((None, LANES, row_tile), bcol_map),
                 pl.BlockSpec((None, LANES, row_tile), bcol_map),
                 pl.BlockSpec((idd, row_tile), col_map),
                 pl.BlockSpec((row_tile, LANES), row_map),
                 pl.BlockSpec((None, IDX_DIM, row_tile), bcol_map),
                 pl.BlockSpec((WI_ROWS, row_tile), col_map)]
    out_shape = [jax.ShapeDtypeStruct((n, A_WIDTH), BF16), jax.ShapeDtypeStruct((nb, CHUNK, A_WIDTH), F32),
                 jax.ShapeDtypeStruct((hd, n), BF16),
                 jax.ShapeDtypeStruct((n, LANES), BF16),
                 jax.ShapeDtypeStruct((nb, LANES, seq), F32),
                 jax.ShapeDtypeStruct((nb, LANES, seq), F32),
                 jax.ShapeDtypeStruct((idd, n), BF16),
                 jax.ShapeDtypeStruct((n, LANES), BF16),
                 jax.ShapeDtypeStruct((nb, IDX_DIM, seq), F32),
                 jax.ShapeDtypeStruct((WI_ROWS, n), F32)]
    return pl.pallas_call(
        _proj_prompt_kernel,
        grid=(nb, ns),
        in_specs=[pl.BlockSpec((row_tile, D_MODEL), row_map)]
                 + [_layer_spec(a, layer) for a in (wn, wt, mm, mb, lng, lnb)]
                 + [pl.BlockSpec((row_tile, LANES), lambda b, s: (s, 0))] * 4
                 + [pl.BlockSpec((LANES, row_tile), lambda b, s: (0, s))] * 4,
        out_specs=out_specs,
        out_shape=out_shape,
        compiler_params=pltpu.CompilerParams(dimension_semantics=("arbitrary", "arbitrary"),
                                             vmem_limit_bytes=VMEM_LIMIT),
        name="proj_prompt",
    )(h2d, wn, wt, mm, mb, lng, lnb, *tabs, *tabs_t)


def _count16(ref, nsteps, cand, strict=False):
    one, zero = jnp.ones(cand.shape, I16), jnp.zeros(cand.shape, I16)
    n_acc = 4

    def body(c, accs):
        off = pl.multiple_of(c * COUNT_CHUNK, COUNT_CHUNK)
        blk = ref[pl.ds(off, COUNT_CHUNK), :]
        accs = list(accs)
        for j in range(COUNT_CHUNK // PACK):
            kk = blk[j * PACK:(j + 1) * PACK, :]
            hit = (kk > cand) if strict else (kk >= cand)
            accs[j % n_acc] = accs[j % n_acc] + jnp.where(hit, one, zero)
        return tuple(accs)
    accs = lax.fori_loop(0, nsteps, body, (zero,) * n_acc)
    tot = (accs[0] + accs[1]) + (accs[2] + accs[3])
    return jnp.sum(tot.astype(F32), axis=0, keepdims=True)


def _search16(ref, nsteps, target):
    qw = ref.shape[1]

    def bit_body(i, prefix):
        cand = prefix + lax.shift_left(jnp.int32(1), 15 - i)
        cand16 = jnp.broadcast_to(cand, (PACK, qw)).astype(I16)
        return jnp.where(_count16(ref, nsteps, cand16) >= target, cand, prefix)
    return lax.fori_loop(0, 16, bit_body, jnp.full((1, qw), I16_MIN, I32))


def _prompt_attn_kernel(qit_ref, wit_ref, qt_ref, kib_ref, kb_ref, vt_ref, o_ref,
                        keys_ref, hi_ref, lo_ref, rhs_ref, qz_ref, acc_ref, kmax_ref, *, topk):
    qw = o_ref.shape[0]
    qb = pl.program_id(1)
    nch = (qb * qw + qw + KEY_CHUNK - 1) // KEY_CHUNK
    nsteps = (nch * KEY_CHUNK + COUNT_CHUNK - 1) // COUNT_CHUNK
    qpos = qb * qw + lax.broadcasted_iota(I32, (KEY_CHUNK, qw), 1)
    krow = lax.broadcasted_iota(I32, (KEY_CHUNK, qw), 0)

    rhs_ref[...] = jnp.zeros(rhs_ref.shape, BF16)
    qz_ref[...] = jnp.zeros(qz_ref.shape, BF16)
    slots = LANES // IDX_DIM
    for h in range(IDX_HEADS):
        r = (h % slots) * IDX_DIM
        rhs_ref[r:r + IDX_DIM, h * qw:(h + 1) * qw] = qit_ref[h * IDX_DIM:(h + 1) * IDX_DIM, :]
    for h in range(N_HEADS):
        r = (h // KV_GROUP) * HEAD_DIM
        qz_ref[r:r + HEAD_DIM, h * qw:(h + 1) * qw] = qt_ref[h * HEAD_DIM:(h + 1) * HEAD_DIM, :]
    wsc = wit_ref[...] * IDX_SCALE

    def score_chunk(c, causal_edge):
        off = pl.multiple_of(c * KEY_CHUNK, KEY_CHUNK)
        kc = kib_ref[pl.ds(off, KEY_CHUNK), :]
        acc = None
        for h in range(IDX_HEADS):
            s = _dot(kc, rhs_ref[:, h * qw:(h + 1) * qw])
            t = jnp.maximum(s, 0.0) * wsc[h:h + 1, :]
            acc = t if acc is None else acc + t
        key = _sortable(acc)
        if causal_edge:
            key = jnp.where(krow + off <= qpos, key, INT_MIN)
        keys_ref[pl.ds(off, KEY_CHUNK), :] = key
        hi_ref[pl.ds(off, KEY_CHUNK), :] = (key >> 16).astype(I16)
        lo_ref[pl.ds(off, KEY_CHUNK), :] = (key ^ 0x8000).astype(I16)

    def score_body(c, carry):
        score_chunk(c, False)
        return carry
    lax.fori_loop(0, nch - 1, score_body, 0)
    score_chunk(nch - 1, True)

    @pl.when(nch * KEY_CHUNK < nsteps * COUNT_CHUNK)
    def _():
        off = pl.multiple_of(nch * KEY_CHUNK, KEY_CHUNK)
        fill = jnp.full((KEY_CHUNK, qw), I16_MIN, I16)
        hi_ref[pl.ds(off, KEY_CHUNK), :] = fill
        lo_ref[pl.ds(off, KEY_CHUNK), :] = fill

    p_hi = _search16(hi_ref, nsteps, float(topk))
    p16 = jnp.broadcast_to(p_hi, (PACK, qw)).astype(I16)
    room = topk - _count16(hi_ref, nsteps, p16, strict=True)

    def low_body(c, carry):
        off = pl.multiple_of(c * COUNT_CHUNK, COUNT_CHUNK)
        lowest = jnp.full((PACK, qw), I16_MIN, I16)
        his, los = hi_ref[pl.ds(off, COUNT_CHUNK), :], lo_ref[pl.ds(off, COUNT_CHUNK), :]
        lo_ref[pl.ds(off, COUNT_CHUNK), :] = jnp.concatenate(
            [jnp.where(his[j * PACK:(j + 1) * PACK, :] == p16, los[j * PACK:(j + 1) * PACK, :], lowest)
             for j in range(COUNT_CHUNK // PACK)], axis=0)
        return carry
    lax.fori_loop(0, nsteps, low_body, 0)
    p_lo = _search16(lo_ref, nsteps, room)
    thr = jnp.maximum(p_hi * 65536 + (p_lo + 2 ** 15), INT_MIN + 1)

    def count32(strict):
        def body(c, acc):
            off = pl.multiple_of(c * KEY_CHUNK, KEY_CHUNK)
            kk = keys_ref[pl.ds(off, KEY_CHUNK), :]
            hit = (kk > thr) if strict else (kk >= thr)
            return acc + jnp.sum(jnp.where(hit, 1.0, 0.0), axis=0, keepdims=True)
        return lax.fori_loop(0, nch, body, jnp.zeros((1, qw), F32))

    @pl.when(jnp.max(count32(False)) > topk)
    def _():
        spare = topk - count32(True)
        r_i = lax.broadcasted_iota(I32, (KEY_CHUNK, KEY_CHUNK), 0)
        c_i = lax.broadcasted_iota(I32, (KEY_CHUNK, KEY_CHUNK), 1)
        lower = jnp.where(c_i <= r_i, 1.0, 0.0).astype(BF16)

        def body(c, seen):
            off = pl.multiple_of(c * KEY_CHUNK, KEY_CHUNK)
            kk = keys_ref[pl.ds(off, KEY_CHUNK), :]
            tie = kk == thr
            rank = seen + _dot(lower, jnp.where(tie, 1.0, 0.0).astype(BF16))
            keys_ref[pl.ds(off, KEY_CHUNK), :] = jnp.where(tie & (rank > spare), thr - 1, kk)
            return rank[KEY_CHUNK - 1:KEY_CHUNK, :]
        lax.fori_loop(0, nch, body, jnp.zeros((1, qw), F32))

    ones_rows = jnp.where(lax.broadcasted_iota(I32, (PACK, KEY_CHUNK), 0) == 0, 1.0, 0.0).astype(BF16)

    @pl.when(qb == 0)
    def _():
        kf = kb_ref[...].astype(F32)
        kmax_ref[0] = jnp.max(jnp.sum(kf * kf, axis=1, keepdims=True))
    shifts = []
    for h in range(N_HEADS):
        qf = qt_ref[h * HEAD_DIM:(h + 1) * HEAD_DIM, :].astype(F32)
        shifts.append(jnp.sqrt(jnp.sum(qf * qf, axis=0, keepdims=True) * kmax_ref[0]))

    def attn_chunk(c, shift_fn):
        off = pl.multiple_of(c * KEY_CHUNK, KEY_CHUNK)
        sel = keys_ref[pl.ds(off, KEY_CHUNK), :] >= thr
        kb = kb_ref[pl.ds(off, KEY_CHUNK), :]
        vt = vt_ref[:, pl.ds(off, KEY_CHUNK)].astype(BF16)
        for g in range(N_KV_HEADS):
            ps, alphas = [], []
            for j in range(KV_GROUP):
                h = g * KV_GROUP + j
                s = jnp.where(sel, _dot(kb, qz_ref[:, h * qw:(h + 1) * qw]), NEG)
                shift, alpha = shift_fn(h, s)
                alphas.append(alpha)
                ps.append(jnp.exp2(s - shift).astype(BF16))
            vg = jnp.concatenate([vt[g * HEAD_DIM:(g + 1) * HEAD_DIM, :], ones_rows], axis=0)
            pv = _dot(vg, jnp.concatenate(ps, axis=1))
            if alphas[0] is None:
                acc_ref[g] = acc_ref[g] + pv
            else:
                acc_ref[g] = acc_ref[g] * jnp.concatenate(alphas, axis=1) + pv

    acc_ref[...] = jnp.zeros(acc_ref.shape, F32)

    def fast_body(c, carry):
        attn_chunk(c, lambda h, s: (shifts[h], None))
        return carry
    lax.fori_loop(0, nch, fast_body, 0)

    denoms = [acc_ref[g][HEAD_DIM:HEAD_DIM + 1, :] for g in range(N_KV_HEADS)]
    denom_min = jnp.min(functools.reduce(jnp.minimum, denoms))

    @pl.when(jnp.logical_not(denom_min > DENOM_FLOOR))
    def _():
        acc_ref[...] = jnp.zeros(acc_ref.shape, F32)

        def online_body(c, ms):
            ms = list(ms)

            def running_max(h, s):
                m_new = jnp.maximum(ms[h], jnp.max(s, axis=0, keepdims=True))
                alpha = jnp.exp2(ms[h] - m_new)
                ms[h] = m_new
                return m_new, alpha
            attn_chunk(c, running_max)
            return tuple(ms)
        lax.fori_loop(0, nch, online_body, (jnp.full((1, qw), NEG, F32),) * N_HEADS)

    for p in range(N_HEADS // 2):
        halves = []
        for h in (2 * p, 2 * p + 1):
            g, j = h // KV_GROUP, h % KV_GROUP
            acc = acc_ref[g][:, j * qw:(j + 1) * qw]
            halves.append(acc[:HEAD_DIM, :] / acc[HEAD_DIM:HEAD_DIM + 1, :])
        o_ref[:, p * LANES:(p + 1) * LANES] = jnp.concatenate(halves, axis=0).T.astype(BF16)


def _prompt_attn_call(qit, wit, qt, kib, kb, vt, nb, seq, topk, qw):
    nq = seq // qw
    hd, idd = N_HEADS * HEAD_DIM, IDX_HEADS * IDX_DIM
    col = lambda b, i: (0, b * nq + i)
    kern = functools.partial(_prompt_attn_kernel, topk=topk)
    return pl.pallas_call(
        kern,
        grid=(nb, nq),
        in_specs=[pl.BlockSpec((idd, qw), col),
                  pl.BlockSpec((WI_ROWS, qw), col),
                  pl.BlockSpec((hd, qw), col),
                  pl.BlockSpec((seq, LANES), lambda b, i: (b, 0)),
                  pl.BlockSpec((seq, LANES), lambda b, i: (b, 0)),
                  pl.BlockSpec((None, LANES, seq), lambda b, i: (b, 0, 0))],
        out_specs=pl.BlockSpec((qw, hd), lambda b, i: (b * nq + i, 0)),
        out_shape=jax.ShapeDtypeStruct((nb * seq, hd), BF16),
        scratch_shapes=[pltpu.VMEM((seq, qw), I32),
                        pltpu.VMEM((seq, qw), I16),
                        pltpu.VMEM((seq, qw), I16),
                        pltpu.VMEM((LANES, IDX_HEADS * qw), BF16),
                        pltpu.VMEM((LANES, N_HEADS * qw), BF16),
                        pltpu.VMEM((N_KV_HEADS, HEAD_DIM + PACK, KV_GROUP * qw), F32),
                        pltpu.SMEM((1,), F32)],
        compiler_params=pltpu.CompilerParams(dimension_semantics=("arbitrary", "arbitrary"),
                                             vmem_limit_bytes=VMEM_LIMIT),
        name="prompt_attn",
    )(qit, wit, qt, kib, kb, vt)


def _mlp_kernel(h_ref, a_ref, b_ref, pe_ref, wo_ref, g1_ref, b1_ref, wf1_ref, wf2_ref, wg_ref, wp_ref,
                g2_ref, b2_ref, out_ref, *, alpha, ff_chunk):
    mix = _dot(a_ref[...], wo_ref[0:A_WIDTH, :]) + _dot(b_ref[...], wo_ref[A_WIDTH:2 * A_WIDTH, :])
    h1 = _layer_norm(alpha * h_ref[...] + mix, g1_ref[...], b1_ref[...])
    h1b = h1.astype(BF16)
    ff = None
    for c in range(D_FF // ff_chunk):
        hid = jnp.maximum(_dot(h1b, wf1_ref[:, c * ff_chunk:(c + 1) * ff_chunk]), 0.0)
        part = _dot((hid * hid).astype(BF16), wf2_ref[c * ff_chunk:(c + 1) * ff_chunk, :])
        ff = part if ff is None else ff + part
    ple = jax.nn.sigmoid(_dot(h1b, wg_ref[...])) * _dot(pe_ref[...].astype(BF16), wp_ref[...])
    out_ref[...] = _layer_norm(alpha * h1 + ff + ple, g2_ref[...], b2_ref[...])


def _mlp_call(h2d, a_out, b_out, pe, layer, consts, alpha, row_tile):
    n = h2d.shape[0]
    row = lambda i: (i, 0)
    kern = functools.partial(_mlp_kernel, alpha=alpha, ff_chunk=1024)
    return pl.pallas_call(
        kern,
        grid=(n // row_tile,),
        in_specs=[pl.BlockSpec((row_tile, D_MODEL), row),
                  pl.BlockSpec((row_tile, A_WIDTH), row),
                  pl.BlockSpec((row_tile, A_WIDTH), row),
                  pl.BlockSpec((None, row_tile, D_PLE), lambda i: (layer, i, 0))]
                 + [_layer_spec(c, layer) for c in consts],
        out_specs=pl.BlockSpec((row_tile, D_MODEL), row),
        out_shape=jax.ShapeDtypeStruct((n, D_MODEL), F32),
        compiler_params=pltpu.CompilerParams(dimension_semantics=("arbitrary",),
                                             vmem_limit_bytes=VMEM_LIMIT),
        name="mlp",
    )(h2d, a_out, b_out, pe, *consts)


def _head_masked_queries(qi_ref):
    lane_head = lax.broadcasted_iota(I32, (CHUNK, LANES), 1) >> 5
    heads_per_col = LANES // IDX_DIM
    out = []
    for h in range(IDX_HEADS):
        col = qi_ref[:, (h // heads_per_col) * LANES:(h // heads_per_col + 1) * LANES]
        out.append(jnp.where(lane_head == (h % heads_per_col), col, jnp.zeros_like(col)))
    return out


def _index_scores(qh, wsc, kc):
    acc = None
    for h in range(IDX_HEADS):
        s = _dot_nt(qh[h], kc)
        t = jnp.maximum(s, 0.0) * wsc[:, h:h + 1]
        acc = t if acc is None else acc + t
    return acc


def _count_ge(keys_ref, nch, cw, cand, strict=False):
    def body(c, acc):
        off = pl.multiple_of(c * cw, cw)
        for j in range(cw // LANES):
            kk = keys_ref[:, pl.ds(off + j * LANES, LANES)]
            hit = (kk > cand) if strict else (kk >= cand)
            acc = acc + jnp.where(hit, 1.0, 0.0)
        return acc
    acc = lax.fori_loop(0, nch, body, jnp.zeros((CHUNK, LANES), F32))
    cnt = jnp.sum(acc, axis=-1, keepdims=True)
    return jnp.broadcast_to(cnt, (CHUNK, LANES))


def _select_threshold(keys_ref, nch, cw, topk):
    def bit_body(i, prefix):
        cand = prefix + lax.shift_left(jnp.int32(1), 31 - i)
        cnt = _count_ge(keys_ref, nch, cw, cand)
        return jnp.where(cnt >= topk, cand, prefix)
    prefix = lax.fori_loop(0, 32, bit_body, jnp.full((CHUNK, LANES), INT_MIN, I32))
    thr = jnp.maximum(prefix, INT_MIN + 1)
    cnt_ge = _count_ge(keys_ref, nch, cw, thr)

    @pl.when(jnp.max(cnt_ge) > topk)
    def _():
        room = topk - _count_ge(keys_ref, nch, cw, thr, strict=True)
        r_i = lax.broadcasted_iota(I32, (LANES, LANES), 0)
        c_i = lax.broadcasted_iota(I32, (LANES, LANES), 1)
        upper = jnp.where(r_i <= c_i, 1.0, 0.0).astype(BF16)

        def body(c, seen):
            off = pl.multiple_of(c * cw, cw)
            for j in range(cw // LANES):
                kk = keys_ref[:, pl.ds(off + j * LANES, LANES)]
                tie = kk == thr
                rank = seen + _dot(jnp.where(tie, 1.0, 0.0).astype(BF16), upper)
                keys_ref[:, pl.ds(off + j * LANES, LANES)] = jnp.where(tie & (rank > room), thr - 1, kk)
                seen = jnp.broadcast_to(rank[:, LANES - 1:LANES], (CHUNK, LANES))
            return seen
        lax.fori_loop(0, nch, body, jnp.zeros((CHUNK, LANES), F32))

    return thr


def _sample_idx_kernel(pt_ref, qs_ref, w_ref, *refs, pages_per_step):
    page_refs, out_ref = refs[:pages_per_step], refs[pages_per_step]
    qs = qs_ref[0]
    w = w_ref[0]
    nt = out_ref.shape[1]
    for j in range(pages_per_step):
        kp = page_refs[j][...].astype(BF16)
        t = jnp.maximum(_dot(qs, kp), 0.0) * w
        for q in range(nt):
            sc = jnp.sum(t[q * IDX_HEADS:(q + 1) * IDX_HEADS, :], axis=0, keepdims=True)
            out_ref[0, q:q + 1, j * LANES:(j + 1) * LANES] = _sortable(sc)


def _sample_idx_call(page_table, qs, wrep, cache_kidx, layer, pps):
    db, npg = page_table.shape
    nt = qs.shape[1] // IDX_HEADS
    page = cache_kidx.shape[3]
    kern = functools.partial(_sample_idx_kernel, pages_per_step=pps)

    def page_spec(j):
        return pl.BlockSpec((None, None, IDX_DIM, page), lambda b, s, pt: (layer, pt[b, s * pps + j], 0, 0))

    grid_spec = pltpu.PrefetchScalarGridSpec(
        num_scalar_prefetch=1,
        grid=(db, npg // pps),
        in_specs=[pl.BlockSpec((1,) + qs.shape[1:], lambda b, s, pt: (b, 0, 0)),
                  pl.BlockSpec((1,) + wrep.shape[1:], lambda b, s, pt: (b, 0, 0))]
                 + [page_spec(j) for j in range(pps)],
        out_specs=pl.BlockSpec((1, nt, pps * page), lambda b, s, pt: (b, 0, s)),
    )
    return pl.pallas_call(
        kern, grid_spec=grid_spec,
        out_shape=jax.ShapeDtypeStruct((db, nt, npg * page), I32),
        compiler_params=pltpu.CompilerParams(dimension_semantics=("arbitrary", "arbitrary")),
        name="sample_idx",
    )(page_table, qs, wrep, *([cache_kidx] * pps))


def _sample_select_kernel(kp_ref, qi_ref, wi_ref, ki_ref, bp_ref, bn_ref, keys_ref, *, topk, cw, nt):
    past = kp_ref.shape[1]
    width = keys_ref.shape[1]
    nch = width // cw
    keys_ref[:, 0:past] = kp_ref[...]
    qh = _head_masked_queries(qi_ref)
    wsc = wi_ref[...] * IDX_SCALE
    key = _sortable(_index_scores(qh, wsc, ki_ref[...].astype(BF16)))
    r_i = lax.broadcasted_iota(I32, (CHUNK, LANES), 0)
    c_i = lax.broadcasted_iota(I32, (CHUNK, LANES), 1)
    visible = ((r_i // nt) == (c_i // nt)) & ((c_i % nt) <= (r_i % nt))
    keys_ref[:, past:past + LANES] = jnp.where(visible, key, INT_MIN)
    if width > past + LANES:
        keys_ref[:, past + LANES:width] = jnp.full((CHUNK, width - past - LANES), INT_MIN, I32)

    thr = _select_threshold(keys_ref, nch, cw, topk)

    for c in range(past // LANES):
        bp_ref[:, c * LANES:(c + 1) * LANES] = jnp.where(
            keys_ref[:, c * LANES:(c + 1) * LANES] >= thr, 0.0, NEG)
    bn_ref[...] = jnp.where(keys_ref[:, past:past + LANES] >= thr, 0.0, NEG)


def _sample_select_call(keys_past, qi, wi, ki4, topk, nt, cw):
    rows, past = keys_past.shape
    width = -(-(past + LANES) // cw) * cw
    kern = functools.partial(_sample_select_kernel, topk=topk, cw=cw, nt=nt)
    return pl.pallas_call(
        kern,
        out_shape=[jax.ShapeDtypeStruct((rows, past), F32), jax.ShapeDtypeStruct((rows, LANES), F32)],
        scratch_shapes=[pltpu.VMEM((rows, width), I32)],
        compiler_params=pltpu.CompilerParams(vmem_limit_bytes=VMEM_LIMIT),
        name="sample_select",
    )(keys_past, qi, wi, ki4)


def _sample_attn_kernel(pt_ref, qz_ref, bp_ref, bn_ref, kn_ref, vn_ref, *refs, pages_per_step, nt):
    pps = pages_per_step
    k_refs, v_refs = refs[:pps], refs[pps:2 * pps]
    o_ref, m_ref, l_ref, acc_ref = refs[2 * pps:]
    step = pl.program_id(1)
    qz = qz_ref[0]
    rows = qz.shape[0]
    hpq = rows // nt

    @pl.when(step == 0)
    def _():
        m_ref[...] = jnp.full(m_ref.shape, NEG, F32)
        l_ref[...] = jnp.zeros(l_ref.shape, F32)
        acc_ref[...] = jnp.zeros(acc_ref.shape, F32)

    def update(s, pv_fn, bias):
        width = s.shape[1]
        s = jnp.concatenate(
            [s[q * hpq:(q + 1) * hpq, :] + bias[q:q + 1, :] for q in range(nt)], axis=0)
        m_old = m_ref[...]
        m_new = jnp.maximum(m_old, jnp.max(s, axis=-1, keepdims=True))
        alpha = jnp.exp(m_old - m_new)
        pr = jnp.exp(s - jnp.concatenate([m_new] * (width // LANES), axis=1))
        l_ref[...] = alpha * l_ref[...] + jnp.sum(pr, axis=-1, keepdims=True)
        m_ref[...] = m_new
        acc_ref[...] = alpha * acc_ref[...] + pv_fn(pr.astype(BF16))

    kb = jnp.concatenate([r[...].astype(BF16) for r in k_refs], axis=1)
    vb = jnp.concatenate([r[...].astype(BF16) for r in v_refs], axis=1)
    update(_dot(qz, kb), lambda pr: _dot_nt(pr, vb), bp_ref[0])

    @pl.when(step == pl.num_programs(1) - 1)
    def _():
        vn = vn_ref[...].astype(BF16)
        update(_dot_nt(qz, kn_ref[...].astype(BF16)), lambda pr: _dot(pr, vn), bn_ref[0])
        o_ref[0] = acc_ref[...] / l_ref[...]


def _sample_attn_call(page_table, qz, bias_past, bias_new, k_new, v_new, cache_k, cache_v, layer, pps, nt):
    db, npg = page_table.shape
    page = cache_k.shape[3]
    rows = qz.shape[1]
    kern = functools.partial(_sample_attn_kernel, pages_per_step=pps, nt=nt)

    def page_spec(j):
        return pl.BlockSpec((None, None, LANES, page), lambda b, s, pt: (layer, pt[b, s * pps + j], 0, 0))

    grid_spec = pltpu.PrefetchScalarGridSpec(
        num_scalar_prefetch=1,
        grid=(db, npg // pps),
        in_specs=[pl.BlockSpec((1, rows, LANES), lambda b, s, pt: (b, 0, 0)),
                  pl.BlockSpec((1, nt, pps * page), lambda b, s, pt: (b, 0, s)),
                  pl.BlockSpec((1, nt, LANES), lambda b, s, pt: (b, 0, 0)),
                  pl.BlockSpec(k_new.shape, lambda b, s, pt: (0, 0)),
                  pl.BlockSpec(v_new.shape, lambda b, s, pt: (0, 0))]
                 + [page_spec(j) for j in range(pps)] * 2,
        out_specs=pl.BlockSpec((1, rows, LANES), lambda b, s, pt: (b, 0, 0)),
        scratch_shapes=[pltpu.VMEM((rows, LANES), F32)] * 3,
    )
    return pl.pallas_call(
        kern, grid_spec=grid_spec,
        out_shape=jax.ShapeDtypeStruct((db, rows, LANES), F32),
        compiler_params=pltpu.CompilerParams(dimension_semantics=("arbitrary", "arbitrary")),
        name="sample_attn",
    )(page_table, qz, bias_past, bias_new, k_new, v_new, *([cache_k] * pps), *([cache_v] * pps))


def _swap_halves(w, head_dim):
    lead = w.shape[:-1]
    nh = w.shape[-1] // head_dim
    return w.reshape(lead + (nh, 2, head_dim // 2))[..., ::-1, :].reshape(lead + (nh * head_dim,))


def _projection_weights(w_in):
    sizes = (A_WIDTH, A_WIDTH, N_HEADS * HEAD_DIM, N_KV_HEADS * HEAD_DIM, N_KV_HEADS * HEAD_DIM,
             IDX_HEADS * IDX_DIM, IDX_DIM, IDX_HEADS)
    pts = [sum(sizes[:i + 1]) for i in range(len(sizes) - 1)]
    au, av, q, k, v, qi, ki, wi = jnp.split(w_in.astype(BF16), pts, axis=-1)
    rep = LANES // IDX_DIM
    q_s, k_s, qi_s, ki_s = (_swap_halves(q, HEAD_DIM), _swap_halves(k, HEAD_DIM),
                            _swap_halves(qi, IDX_DIM), _swap_halves(ki, IDX_DIM))
    ki4, ki4_s = jnp.tile(ki, (1, 1, rep)), jnp.tile(ki_s, (1, 1, rep))
    pad_cols = lambda w, n: jnp.pad(w, ((0, 0), (0, 0), (0, n - w.shape[-1])))
    ext = jnp.concatenate([au, av, q, k, v, qi, ki4, pad_cols(wi, LANES), q_s, k_s, qi_s, ki4_s], axis=-1)
    wn = jnp.concatenate([au, av, k, k_s, ki4, ki4_s], axis=-1)
    wt = jnp.swapaxes(jnp.concatenate([q, q_s, k, k_s, v, qi, qi_s, ki, ki_s, pad_cols(wi, WI_ROWS)], axis=-1),
                      1, 2)
    assert ext.shape[-1] == C_END and wn.shape[-1] == N_END and wt.shape[1] == T_END
    return ext, wn, wt


def _rope_tables(pos, head_dim):
    half = head_dim // 2
    inv = ROPE_THETA ** (-jnp.arange(half, dtype=F32) / half)
    ang = pos.astype(F32)[:, None] * inv[None, :]
    cos, sin = jnp.cos(ang), jnp.sin(ang)
    reps = LANES // head_dim
    return (jnp.tile(jnp.concatenate([cos, cos], axis=-1), (1, reps)),
            jnp.tile(jnp.concatenate([-sin, sin], axis=-1), (1, reps)))


def _pick(n, prefs):
    for p in prefs:
        if n % p == 0:
            return p
    return n


def kernel(x_prompt, x_sample, p_prompt, p_sample, cache_k, cache_v, cache_kidx, page_table, w_in, sgu_ln_g,
           sgu_ln_b, sgu_w, sgu_b, w_o, ln1_g, ln1_b, w_ff1, w_ff2, w_ple_gate, w_ple_proj, ln2_g, ln2_b):
    nb, seq, _ = x_prompt.shape
    db, nt, _ = x_sample.shape
    depth = w_in.shape[0]
    npg = page_table.shape[1]
    page = cache_k.shape[2]
    past = npg * page
    alpha = (2.0 * depth) ** 0.25
    topk_p = min(TOPK_MAX, seq // 4)
    topk_s = min(TOPK_MAX, (past + nt) // 4)
    n_p, n_s = nb * seq, db * nt
    assert n_s == CHUNK and seq % COUNT_CHUNK == 0 and page == LANES

    w_ext, w_n, w_t = _projection_weights(w_in)
    wo_b, wf1_b, wf2_b = w_o.astype(BF16), w_ff1.astype(BF16), w_ff2.astype(BF16)
    wg_b, wp_b = w_ple_gate.astype(BF16), w_ple_proj.astype(BF16)
    tril = jnp.tril(jnp.ones((CHUNK, CHUNK), F32))
    mm_p = (sgu_w * tril).astype(BF16)
    mb_p = jnp.repeat(jnp.swapaxes(sgu_b, 1, 2), A_GW, axis=2)
    r_i = jnp.arange(CHUNK)
    same_req = (r_i[:, None] // nt == r_i[None, :] // nt) & (r_i[None, :] % nt <= r_i[:, None] % nt)
    mm_s = jnp.where(same_req, sgu_w[:, :, r_i % nt][:, :, :, r_i % nt], 0.0).astype(BF16)
    mb_s = jnp.tile(jnp.repeat(jnp.swapaxes(sgu_b[:, :, :nt], 1, 2), A_GW, axis=2), (1, CHUNK // nt, 1))
    lng = sgu_ln_g.reshape(depth, 1, A_WIDTH)
    lnb = sgu_ln_b.reshape(depth, 1, A_WIDTH)
    mlp_w = (wo_b, ln1_g.reshape(depth, 1, D_MODEL), ln1_b.reshape(depth, 1, D_MODEL), wf1_b, wf2_b, wg_b, wp_b,
             ln2_g.reshape(depth, 1, D_MODEL), ln2_b.reshape(depth, 1, D_MODEL))
    pe_p = p_prompt.reshape(depth, n_p, D_PLE)
    pe_s = p_sample.reshape(depth, n_s, D_PLE)

    pos_p = jnp.arange(seq, dtype=I32)
    pos_s = past + (jnp.arange(n_s, dtype=I32) % nt)
    tabs_p = _rope_tables(pos_p, HEAD_DIM) + _rope_tables(pos_p, IDX_DIM)
    tabs_pt = tuple(t.T for t in tabs_p)
    tabs_s = _rope_tables(pos_s, HEAD_DIM) + _rope_tables(pos_s, IDX_DIM)

    cki = jnp.swapaxes(cache_kidx, 2, 3)
    ck = jnp.transpose(cache_k, (0, 1, 3, 4, 2)).reshape(cache_k.shape[:2] + (LANES, page))
    cv = jnp.transpose(cache_v, (0, 1, 3, 4, 2)).reshape(cache_v.shape[:2] + (LANES, page))

    rt_proj = _pick(seq, (512, 256, 128))
    rt_mlp = _pick(n_p, (512, 256, 128))
    qw_p = _pick(seq, (256, 128))
    cw_s = 256
    pps = _pick(npg, (64, 32, 16, 8, 4, 2, 1))
    head_lo = (jnp.arange(N_HEADS) // KV_GROUP) * HEAD_DIM

    hp = x_prompt.reshape(n_p, D_MODEL)
    hs = x_sample.reshape(n_s, D_MODEL)
    outs = [[] for _ in range(8)]
    for i in range(depth):
        a_out, vn, qt, kb, kt, vt, qit, kib, kit, wit = _proj_prompt_call(
            hp, i, w_n, w_t, mm_p, mb_p, lng, lnb, tabs_p, tabs_pt, nb, seq, rt_proj)
        b_out = _prompt_attn_call(qit, wit, qt, kib, kb, vt, nb, seq, topk_p, qw_p)
        hp = _mlp_call(hp, a_out, b_out, pe_p, i, mlp_w, alpha, rt_mlp)
        to_tokens = lambda a: jnp.transpose(a.reshape(nb, N_KV_HEADS, HEAD_DIM, seq), (0, 3, 1, 2))
        outs[0].append(to_tokens(kt))
        outs[1].append(to_tokens(vt))
        outs[2].append(jnp.swapaxes(kit, 1, 2))
        outs[3].append(vn)

        a_out, vn, q, k, v, qi, ki4, wi = _proj_call(hs, i, w_ext, mm_s, mb_s, lng, lnb, tabs_s)
        qs = qi.reshape(db, nt * IDX_HEADS, IDX_DIM)
        wrep = jnp.broadcast_to((wi[:, :IDX_HEADS] * IDX_SCALE).reshape(db, nt * IDX_HEADS, 1),
                                (db, nt * IDX_HEADS, LANES))
        keys_past = _sample_idx_call(page_table, qs, wrep, cki, i, pps)
        bias_past, bias_new = _sample_select_call(keys_past.reshape(n_s, past), qi, wi, ki4, topk_s, nt, cw_s)
        q4 = q.reshape(db, nt, N_HEADS, 1, HEAD_DIM)
        half = (jnp.arange(N_KV_HEADS) * HEAD_DIM)[None, None, None, :, None]
        qz = jnp.where(half == head_lo[None, None, :, None, None], q4, jnp.zeros_like(q4))
        qz = qz.reshape(db, nt * N_HEADS, LANES)
        o = _sample_attn_call(page_table, qz, bias_past.reshape(db, nt, past), bias_new.reshape(db, nt, LANES),
                              k, v, ck, cv, i, pps, nt)
        o = o.reshape(db, nt, N_HEADS, N_KV_HEADS, HEAD_DIM)
        b_out = jnp.concatenate([o[:, :, g * KV_GROUP:(g + 1) * KV_GROUP, g, :] for g in range(N_KV_HEADS)], axis=2)
        b_out = b_out.reshape(n_s, N_HEADS * HEAD_DIM).astype(BF16)
        hs = _mlp_call(hs, a_out, b_out, pe_s, i, mlp_w, alpha, n_s)
        outs[4].append(k.reshape(db, nt, N_KV_HEADS, HEAD_DIM))
        outs[5].append(v.reshape(db, nt, N_KV_HEADS, HEAD_DIM))
        outs[6].append(ki4[:, :IDX_DIM].reshape(db, nt, IDX_DIM))
        outs[7].append(vn.reshape(db, nt, A_WIDTH))

    st = [jnp.stack(o) for o in outs]
    return (hp.reshape(nb, seq, D_MODEL), hs.reshape(db, nt, D_MODEL),
            st[0], st[1], st[2], st[3], st[4], st[5], st[6], st[7])
```

```python
import functools

import jax
import jax.numpy as jnp
from jax import lax
from jax.experimental import pallas as pl
from jax.experimental.pallas import tpu as pltpu

F32 = jnp.float32
BF16 = jnp.bfloat16
I32 = jnp.int32
I16 = jnp.int16

D_MODEL = 1024
CHUNK = 128
A_WIDTH = 512
A_GROUPS = 4
A_GW = A_WIDTH // A_GROUPS
HEAD_DIM = 64
N_HEADS = 8
N_KV_HEADS = 2
KV_GROUP = N_HEADS // N_KV_HEADS
IDX_HEADS = 8
IDX_DIM = 32
TOPK_MAX = 256
ROPE_THETA = 10000.0
D_FF = 4 * D_MODEL
D_PLE = 256
LN_EPS = 1e-5

LANES = 128
PACK = 16
VMEM_LIMIT = 56 * 1024 * 1024

INT_MIN = -(2 ** 31)
I16_MIN = -(2 ** 15)
NEG = -1e30
DENOM_FLOOR = 2.0 ** -100

C_AU, C_AV, C_Q, C_K, C_V, C_QI, C_KI, C_WI = 0, 512, 1024, 1536, 1664, 1792, 2048, 2176
C_QS, C_KS, C_QIS, C_KIS, C_END = 2304, 2816, 2944, 3200, 3328
N_AU, N_AV, N_K, N_KS, N_KI, N_KIS, N_END = 0, 512, 1024, 1152, 1280, 1408, 1536
T_Q, T_QS, T_K, T_KS, T_V, T_QI, T_QIS, T_KI, T_KIS, T_WI, T_END = (
    0, 512, 1024, 1152, 1280, 1408, 1664, 1920, 1952, 1984, 2000)
WI_ROWS = T_END - T_WI
Q_SCALE_LOG2 = (HEAD_DIM ** -0.5) * 1.4426950408889634
IDX_SCALE = (IDX_DIM ** -0.5) * (IDX_HEADS ** -0.5)

KEY_CHUNK = 512
COUNT_CHUNK = 512

NT_DIMS = (((1,), (1,)), ((), ()))


def _dot(a, b):
    return jnp.dot(a, b, preferred_element_type=F32)


def _dot_nt(a, b):
    return lax.dot_general(a, b, NT_DIMS, preferred_element_type=F32)


def _layer_norm(x, g, b):
    mu = jnp.mean(x, axis=-1, keepdims=True)
    xc = x - mu
    var = jnp.mean(xc * xc, axis=-1, keepdims=True)
    return xc * lax.rsqrt(var + LN_EPS) * g + b


def _gelu(x):
    return 0.5 * x * (1.0 + lax.erf(x * (2.0 ** -0.5)))


def _sortable(x):
    bits = pltpu.bitcast(x, I32)
    return bits ^ ((bits >> 31) & 0x7FFFFFFF)


def _layer_spec(stacked, layer):
    shape = stacked.shape[1:]
    return pl.BlockSpec((None,) + shape, lambda *_: (layer,) + (0,) * len(shape),
                        pipeline_mode=pl.Buffered(1))


def _gated_mixer(au, av, g, lng_ref, lnb_ref, mm_ref, mb_ref, vn_ref, aout_ref):
    rows = au.shape[0]
    lo = g * A_GW
    vn = _layer_norm(av, lng_ref[:, lo:lo + A_GW], lnb_ref[:, lo:lo + A_GW])
    vn_ref[:, lo:lo + A_GW] = vn[rows - vn_ref.shape[0]:, :]
    vnb = vn.astype(BF16)
    for c in range(rows // CHUNK):
        r0 = c * CHUNK
        gate = _dot(mm_ref[g], vnb[r0:r0 + CHUNK, :]) + mb_ref[:, lo:lo + A_GW]
        aout_ref[r0:r0 + CHUNK, lo:lo + A_GW] = (au[r0:r0 + CHUNK, :] * gate).astype(BF16)


def _proj_kernel(h_ref, w_ref, mm_ref, mb_ref, lng_ref, lnb_ref, cq_ref, sq_ref, ci_ref, si_ref,
                 aout_ref, vn_ref, q_ref, k_ref, v_ref, qi_ref, ki_ref, wi_ref):
    hb = h_ref[...].astype(BF16)

    def proj(lo, width=LANES):
        return _dot(hb, w_ref[:, lo:lo + width])

    for g in range(A_GROUPS):
        lo = g * A_GW
        _gated_mixer(_gelu(proj(C_AU + lo)), _gelu(proj(C_AV + lo)), g,
                     lng_ref, lnb_ref, mm_ref, mb_ref, vn_ref, aout_ref)

    cq, sq = cq_ref[...], sq_ref[...]
    ci, si = ci_ref[...], si_ref[...]
    for j in range(N_HEADS * HEAD_DIM // LANES):
        z = proj(C_Q + j * LANES) * cq + proj(C_QS + j * LANES) * sq
        q_ref[:, j * LANES:(j + 1) * LANES] = (z * (HEAD_DIM ** -0.5)).astype(BF16)
    k_ref[...] = proj(C_K) * cq + proj(C_KS) * sq
    v_ref[...] = proj(C_V)
    for j in range(IDX_HEADS * IDX_DIM // LANES):
        z = proj(C_QI + j * LANES) * ci + proj(C_QIS + j * LANES) * si
        qi_ref[:, j * LANES:(j + 1) * LANES] = z.astype(BF16)
    ki_ref[...] = proj(C_KI) * ci + proj(C_KIS) * si
    wi_ref[...] = proj(C_WI)


def _proj_call(h2d, layer, w_ext, mm, mb, lng, lnb, tabs):
    n = h2d.shape[0]
    whole = lambda a: pl.BlockSpec(a.shape, lambda i: (0, 0))
    out_widths = [(A_WIDTH, BF16), (A_WIDTH, F32), (N_HEADS * HEAD_DIM, BF16), (LANES, F32), (LANES, F32),
                  (IDX_HEADS * IDX_DIM, BF16), (LANES, F32), (LANES, F32)]
    return pl.pallas_call(
        _proj_kernel,
        grid=(1,),
        in_specs=[whole(h2d)] + [_layer_spec(a, layer) for a in (w_ext, mm, mb, lng, lnb)]
                 + [whole(t) for t in tabs],
        out_specs=[pl.BlockSpec((n, w), lambda i: (0, 0)) for w, _ in out_widths],
        out_shape=[jax.ShapeDtypeStruct((n, w), dt) for w, dt in out_widths],
        compiler_params=pltpu.CompilerParams(dimension_semantics=("arbitrary",), vmem_limit_bytes=VMEM_LIMIT),
        name="proj",
    )(h2d, w_ext, mm, mb, lng, lnb, *tabs)


def _proj_prompt_kernel(h_ref, wn_ref, wt_ref, mm_ref, mb_ref, lng_ref, lnb_ref,
                        cq_ref, sq_ref, ci_ref, si_ref, cqt_ref, sqt_ref, cit_ref, sit_ref,
                        aout_ref, vn_ref, qt_ref, kb_ref, kt_ref, vt_ref, qit_ref, kib_ref, kit_ref, wit_ref):
    hb = h_ref[...].astype(BF16)
    wide = 2 * LANES

    for gp in range(A_GROUPS // 2):
        au2 = _gelu(_dot(hb, wn_ref[:, N_AU + gp * wide:N_AU + (gp + 1) * wide]))
        av2 = _gelu(_dot(hb, wn_ref[:, N_AV + gp * wide:N_AV + (gp + 1) * wide]))
        for e in range(2):
            _gated_mixer(au2[:, e * A_GW:(e + 1) * A_GW], av2[:, e * A_GW:(e + 1) * A_GW], 2 * gp + e,
                         lng_ref, lnb_ref, mm_ref, mb_ref, vn_ref, aout_ref)

    z = _dot(hb, wn_ref[:, N_K:N_K + wide])
    kb_ref[...] = (z[:, :LANES] * cq_ref[...] + z[:, LANES:] * sq_ref[...]).astype(BF16)
    z = _dot(hb, wn_ref[:, N_KI:N_KI + wide])
    kib_ref[...] = (z[:, :LANES] * ci_ref[...] + z[:, LANES:] * si_ref[...]).astype(BF16)

    zt = _dot_nt(wt_ref[...], hb)
    cqt, sqt = cqt_ref[...], sqt_ref[...]
    cit, sit = cit_ref[...], sit_ref[...]
    for j in range(N_HEADS * HEAD_DIM // LANES):
        r = j * LANES
        q = zt[T_Q + r:T_Q + r + LANES] * cqt + zt[T_QS + r:T_QS + r + LANES] * sqt
        qt_ref[r:r + LANES, :] = (q * Q_SCALE_LOG2).astype(BF16)
    kt_ref[...] = zt[T_K:T_K + LANES] * cqt + zt[T_KS:T_KS + LANES] * sqt
    vt_ref[...] = zt[T_V:T_V + LANES]
    for j in range(IDX_HEADS * IDX_DIM // LANES):
        r = j * LANES
        qi = zt[T_QI + r:T_QI + r + LANES] * cit + zt[T_QIS + r:T_QIS + r + LANES] * sit
        qit_ref[r:r + LANES, :] = qi.astype(BF16)
    kit_ref[...] = zt[T_KI:T_KI + IDX_DIM] * cit[:IDX_DIM] + zt[T_KIS:T_KIS + IDX_DIM] * sit[:IDX_DIM]
    wit_ref[...] = zt[T_WI:T_END]


def _proj_prompt_call(h2d, layer, wn, wt, mm, mb, lng, lnb, tabs, tabs_t, nb, seq, row_tile):
    n = nb * seq
    ns = seq // row_tile
    row_map = lambda b, s: (b * ns + s, 0)
    col_map = lambda b, s: (0, b * ns + s)
    bcol_map = lambda b, s: (b, 0, s)
    hd, idd = N_HEADS * HEAD_DIM, IDX_HEADS * IDX_DIM
    out_specs = [pl.BlockSpec((row_tile, A_WIDTH), row_map),
                 pl.BlockSpec((None, CHUNK, A_WIDTH), lambda b, s: (b, 0, 0)),
                 pl.BlockSpec((hd, row_tile), col_map),
                 pl.BlockSpec((row_tile, LANES), row_map),
                 pl.BlockSpec((None, LANES, row_tile), bcol_map),
                 pl.BlockSpec((None, LANES, row_tile), bcol_map),
                 pl.BlockSpec((idd, row_tile), col_map),
                 pl.BlockSpec((row_tile, LANES), row_map),
                 pl.BlockSpec((None, IDX_DIM, row_tile), bcol_map),
                 pl.BlockSpec((WI_ROWS, row_tile), col_map)]
    out_shape = [jax.ShapeDtypeStruct((n, A_WIDTH), BF16), jax.ShapeDtypeStruct((nb, CHUNK, A_WIDTH), F32),
                 jax.ShapeDtypeStruct((hd, n), BF16),
                 jax.ShapeDtypeStruct((n, LANES), BF16),
                 jax.ShapeDtypeStruct((nb, LANES, seq), F32),
                 jax.ShapeDtypeStruct((nb, LANES, seq), F32),
                 jax.ShapeDtypeStruct((idd, n), BF16),
                 jax.ShapeDtypeStruct((n, LANES), BF16),
                 jax.ShapeDtypeStruct((nb, IDX_DIM, seq), F32),
                 jax.ShapeDtypeStruct((WI_ROWS, n), F32)]
    return pl.pallas_call(
        _proj_prompt_kernel,
        grid=(nb, ns),
        in_specs=[pl.BlockSpec((row_tile, D_MODEL), row_map)]
                 + [_layer_spec(a, layer) for a in (wn, wt, mm, mb, lng, lnb)]
                 + [pl.BlockSpec((row_tile, LANES), lambda b, s: (s, 0))] * 4
                 + [pl.BlockSpec((LANES, row_tile), lambda b, s: (0, s))] * 4,
        out_specs=out_specs,
        out_shape=out_shape,
        compiler_params=pltpu.CompilerParams(dimension_semantics=("arbitrary", "arbitrary"),
                                             vmem_limit_bytes=VMEM_LIMIT),
        name="proj_prompt",
    )(h2d, wn, wt, mm, mb, lng, lnb, *tabs, *tabs_t)


def _count16(ref, nsteps, cand, strict=False):
    one, zero = jnp.ones(cand.shape, I16), jnp.zeros(cand.shape, I16)
    n_acc = 4

    def body(c, accs):
        off = pl.multiple_of(c * COUNT_CHUNK, COUNT_CHUNK)
        blk = ref[pl.ds(off, COUNT_CHUNK), :]
        accs = list(accs)
        for j in range(COUNT_CHUNK // PACK):
            kk = blk[j * PACK:(j + 1) * PACK, :]
            hit = (kk > cand) if strict else (kk >= cand)
            accs[j % n_acc] = accs[j % n_acc] + jnp.where(hit, one, zero)
        return tuple(accs)
    accs = lax.fori_loop(0, nsteps, body, (zero,) * n_acc)
    tot = (accs[0] + accs[1]) + (accs[2] + accs[3])
    return jnp.sum(tot.astype(F32), axis=0, keepdims=True)


def _search16(ref, nsteps, target):
    qw = ref.shape[1]

    def bit_body(i, prefix):
        cand = prefix + lax.shift_left(jnp.int32(1), 15 - i)
        cand16 = jnp.broadcast_to(cand, (PACK, qw)).astype(I16)
        return jnp.where(_count16(ref, nsteps, cand16) >= target, cand, prefix)
    return lax.fori_loop(0, 16, bit_body, jnp.full((1, qw), I16_MIN, I32))


def _prompt_attn_kernel(qit_ref, wit_ref, qt_ref, kib_ref, kb_ref, vt_ref, o_ref,
                        keys_ref, hi_ref, lo_ref, rhs_ref, qz_ref, acc_ref, kmax_ref, *, topk):
    qw = o_ref.shape[0]
    qb = pl.program_id(1)
    nch = (qb * qw + qw + KEY_CHUNK - 1) // KEY_CHUNK
    nsteps = (nch * KEY_CHUNK + COUNT_CHUNK - 1) // COUNT_CHUNK
    qpos = qb * qw + lax.broadcasted_iota(I32, (KEY_CHUNK, qw), 1)
    krow = lax.broadcasted_iota(I32, (KEY_CHUNK, qw), 0)

    rhs_ref[...] = jnp.zeros(rhs_ref.shape, BF16)
    qz_ref[...] = jnp.zeros(qz_ref.shape, BF16)
    slots = LANES // IDX_DIM
    for h in range(IDX_HEADS):
        r = (h % slots) * IDX_DIM
        rhs_ref[r:r + IDX_DIM, h * qw:(h + 1) * qw] = qit_ref[h * IDX_DIM:(h + 1) * IDX_DIM, :]
    for h in range(N_HEADS):
        r = (h // KV_GROUP) * HEAD_DIM
        qz_ref[r:r + HEAD_DIM, h * qw:(h + 1) * qw] = qt_ref[h * HEAD_DIM:(h + 1) * HEAD_DIM, :]
    wsc = wit_ref[...] * IDX_SCALE

    def score_chunk(c, causal_edge):
        off = pl.multiple_of(c * KEY_CHUNK, KEY_CHUNK)
        kc = kib_ref[pl.ds(off, KEY_CHUNK), :]
        acc = None
        for h in range(IDX_HEADS):
            s = _dot(kc, rhs_ref[:, h * qw:(h + 1) * qw])
            t = jnp.maximum(s, 0.0) * wsc[h:h + 1, :]
            acc = t if acc is None else acc + t
        key = _sortable(acc)
        if causal_edge:
            key = jnp.where(krow + off <= qpos, key, INT_MIN)
        keys_ref[pl.ds(off, KEY_CHUNK), :] = key
        hi_ref[pl.ds(off, KEY_CHUNK), :] = (key >> 16).astype(I16)
        lo_ref[pl.ds(off, KEY_CHUNK), :] = (key ^ 0x8000).astype(I16)

    def score_body(c, carry):
        score_chunk(c, False)
        return carry
    lax.fori_loop(0, nch - 1, score_body, 0)
    score_chunk(nch - 1, True)

    @pl.when(nch * KEY_CHUNK < nsteps * COUNT_CHUNK)
    def _():
        off = pl.multiple_of(nch * KEY_CHUNK, KEY_CHUNK)
        fill = jnp.full((KEY_CHUNK, qw), I16_MIN, I16)
        hi_ref[pl.ds(off, KEY_CHUNK), :] = fill
        lo_ref[pl.ds(off, KEY_CHUNK), :] = fill

    p_hi = _search16(hi_ref, nsteps, float(topk))
    p16 = jnp.broadcast_to(p_hi, (PACK, qw)).astype(I16)
    room = topk - _count16(hi_ref, nsteps, p16, strict=True)

    def low_body(c, carry):
        off = pl.multiple_of(c * COUNT_CHUNK, COUNT_CHUNK)
        lowest = jnp.full((PACK, qw), I16_MIN, I16)
        his, los = hi_ref[pl.ds(off, COUNT_CHUNK), :], lo_ref[pl.ds(off, COUNT_CHUNK), :]
        lo_ref[pl.ds(off, COUNT_CHUNK), :] = jnp.concatenate(
            [jnp.where(his[j * PACK:(j + 1) * PACK, :] == p16, los[j * PACK:(j + 1) * PACK, :], lowest)
             for j in range(COUNT_CHUNK // PACK)], axis=0)
        return carry
    lax.fori_loop(0, nsteps, low_body, 0)
    p_lo = _search16(lo_ref, nsteps, room)
    thr = jnp.maximum(p_hi * 65536 + (p_lo + 2 ** 15), INT_MIN + 1)

    def count32(strict):
        def body(c, acc):
            off = pl.multiple_of(c * KEY_CHUNK, KEY_CHUNK)
            kk = keys_ref[pl.ds(off, KEY_CHUNK), :]
            hit = (kk > thr) if strict else (kk >= thr)
            return acc + jnp.sum(jnp.where(hit, 1.0, 0.0), axis=0, keepdims=True)
        return lax.fori_loop(0, nch, body, jnp.zeros((1, qw), F32))

    @pl.when(jnp.max(count32(False)) > topk)
    def _():
        spare = topk - count32(True)
        r_i = lax.broadcasted_iota(I32, (KEY_CHUNK, KEY_CHUNK), 0)
        c_i = lax.broadcasted_iota(I32, (KEY_CHUNK, KEY_CHUNK), 1)
        lower = jnp.where(c_i <= r_i, 1.0, 0.0).astype(BF16)

        def body(c, seen):
            off = pl.multiple_of(c * KEY_CHUNK, KEY_CHUNK)
            kk = keys_ref[pl.ds(off, KEY_CHUNK), :]
            tie = kk == thr
            rank = seen + _dot(lower, jnp.where(tie, 1.0, 0.0).astype(BF16))
            keys_ref[pl.ds(off, KEY_CHUNK), :] = jnp.where(tie & (rank > spare), thr - 1, kk)
            return rank[KEY_CHUNK - 1:KEY_CHUNK, :]
        lax.fori_loop(0, nch, body, jnp.zeros((1, qw), F32))

    ones_rows = jnp.where(lax.broadcasted_iota(I32, (PACK, KEY_CHUNK), 0) == 0, 1.0, 0.0).astype(BF16)

    @pl.when(qb == 0)
    def _():
        kf = kb_ref[...].astype(F32)
        kmax_ref[0] = jnp.max(jnp.sum(kf * kf, axis=1, keepdims=True))
    shifts = []
    for h in range(N_HEADS):
        qf = qt_ref[h * HEAD_DIM:(h + 1) * HEAD_DIM, :].astype(F32)
        shifts.append(jnp.sqrt(jnp.sum(qf * qf, axis=0, keepdims=True) * kmax_ref[0]))

    def attn_chunk(c, shift_fn):
        off = pl.multiple_of(c * KEY_CHUNK, KEY_CHUNK)
        sel = keys_ref[pl.ds(off, KEY_CHUNK), :] >= thr
        kb = kb_ref[pl.ds(off, KEY_CHUNK), :]
        vt = vt_ref[:, pl.ds(off, KEY_CHUNK)].astype(BF16)
        for g in range(N_KV_HEADS):
            ps, alphas = [], []
            for j in range(KV_GROUP):
                h = g * KV_GROUP + j
                s = jnp.where(sel, _dot(kb, qz_ref[:, h * qw:(h + 1) * qw]), NEG)
                shift, alpha = shift_fn(h, s)
                alphas.append(alpha)
                ps.append(jnp.exp2(s - shift).astype(BF16))
            vg = jnp.concatenate([vt[g * HEAD_DIM:(g + 1) * HEAD_DIM, :], ones_rows], axis=0)
            pv = _dot(vg, jnp.concatenate(ps, axis=1))
            if alphas[0] is None:
                acc_ref[g] = acc_ref[g] + pv
            else:
                acc_ref[g] = acc_ref[g] * jnp.concatenate(alphas, axis=1) + pv

    acc_ref[...] = jnp.zeros(acc_ref.shape, F32)

    def fast_body(c, carry):
        attn_chunk(c, lambda h, s: (shifts[h], None))
        return carry
    lax.fori_loop(0, nch, fast_body, 0)

    denoms = [acc_ref[g][HEAD_DIM:HEAD_DIM + 1, :] for g in range(N_KV_HEADS)]
    denom_min = jnp.min(functools.reduce(jnp.minimum, denoms))

    @pl.when(jnp.logical_not(denom_min > DENOM_FLOOR))
    def _():
        acc_ref[...] = jnp.zeros(acc_ref.shape, F32)

        def online_body(c, ms):
            ms = list(ms)

            def running_max(h, s):
                m_new = jnp.maximum(ms[h], jnp.max(s, axis=0, keepdims=True))
                alpha = jnp.exp2(ms[h] - m_new)
                ms[h] = m_new
                return m_new, alpha
            attn_chunk(c, running_max)
            return tuple(ms)
        lax.fori_loop(0, nch, online_body, (jnp.full((1, qw), NEG, F32),) * N_HEADS)

    for p in range(N_HEADS // 2):
        halves = []
        for h in (2 * p, 2 * p + 1):
            g, j = h // KV_GROUP, h % KV_GROUP
            acc = acc_ref[g][:, j * qw:(j + 1) * qw]
            halves.append(acc[:HEAD_DIM, :] / acc[HEAD_DIM:HEAD_DIM + 1, :])
        o_ref[:, p * LANES:(p + 1) * LANES] = jnp.concatenate(halves, axis=0).T.astype(BF16)


def _prompt_attn_call(qit, wit, qt, kib, kb, vt, nb, seq, topk, qw):
    nq = seq // qw
    hd, idd = N_HEADS * HEAD_DIM, IDX_HEADS * IDX_DIM
    col = lambda b, i: (0, b * nq + i)
    kern = functools.partial(_prompt_attn_kernel, topk=topk)
    return pl.pallas_call(
        kern,
        grid=(nb, nq),
        in_specs=[pl.BlockSpec((idd, qw), col),
                  pl.BlockSpec((WI_ROWS, qw), col),
                  pl.BlockSpec((hd, qw), col),
                  pl.BlockSpec((seq, LANES), lambda b, i: (b, 0)),
                  pl.BlockSpec((seq, LANES), lambda b, i: (b, 0)),
                  pl.BlockSpec((None, LANES, seq), lambda b, i: (b, 0, 0))],
        out_specs=pl.BlockSpec((qw, hd), lambda b, i: (b * nq + i, 0)),
        out_shape=jax.ShapeDtypeStruct((nb * seq, hd), BF16),
        scratch_shapes=[pltpu.VMEM((seq, qw), I32),
                        pltpu.VMEM((seq, qw), I16),
                        pltpu.VMEM((seq, qw), I16),
                        pltpu.VMEM((LANES, IDX_HEADS * qw), BF16),
                        pltpu.VMEM((LANES, N_HEADS * qw), BF16),
                        pltpu.VMEM((N_KV_HEADS, HEAD_DIM + PACK, KV_GROUP * qw), F32),
                        pltpu.SMEM((1,), F32)],
        compiler_params=pltpu.CompilerParams(dimension_semantics=("arbitrary", "arbitrary"),
                                             vmem_limit_bytes=VMEM_LIMIT),
        name="prompt_attn",
    )(qit, wit, qt, kib, kb, vt)


def _mlp_kernel(h_ref, a_ref, b_ref, pe_ref, wo_ref, g1_ref, b1_ref, wf1_ref, wf2_ref, wg_ref, wp_ref,
                g2_ref, b2_ref, out_ref, *, alpha, ff_chunk):
    mix = _dot(a_ref[...], wo_ref[0:A_WIDTH, :]) + _dot(b_ref[...], wo_ref[A_WIDTH:2 * A_WIDTH, :])
    h1 = _layer_norm(alpha * h_ref[...] + mix, g1_ref[...], b1_ref[...])
    h1b = h1.astype(BF16)
    ff = None
    for c in range(D_FF // ff_chunk):
        hid = jnp.maximum(_dot(h1b, wf1_ref[:, c * ff_chunk:(c + 1) * ff_chunk]), 0.0)
        part = _dot((hid * hid).astype(BF16), wf2_ref[c * ff_chunk:(c + 1) * ff_chunk, :])
        ff = part if ff is None else ff + part
    ple = jax.nn.sigmoid(_dot(h1b, wg_ref[...])) * _dot(pe_ref[...].astype(BF16), wp_ref[...])
    out_ref[...] = _layer_norm(alpha * h1 + ff + ple, g2_ref[...], b2_ref[...])


def _mlp_call(h2d, a_out, b_out, pe, layer, consts, alpha, row_tile):
    n = h2d.shape[0]
    row = lambda i: (i, 0)
    kern = functools.partial(_mlp_kernel, alpha=alpha, ff_chunk=1024)
    return pl.pallas_call(
        kern,
        grid=(n // row_tile,),
        in_specs=[pl.BlockSpec((row_tile, D_MODEL), row),
                  pl.BlockSpec((row_tile, A_WIDTH), row),
                  pl.BlockSpec((row_tile, A_WIDTH), row),
                  pl.BlockSpec((None, row_tile, D_PLE), lambda i: (layer, i, 0))]
                 + [_layer_spec(c, layer) for c in consts],
        out_specs=pl.BlockSpec((row_tile, D_MODEL), row),
        out_shape=jax.ShapeDtypeStruct((n, D_MODEL), F32),
        compiler_params=pltpu.CompilerParams(dimension_semantics=("arbitrary",),
                                             vmem_limit_bytes=VMEM_LIMIT),
        name="mlp",
    )(h2d, a_out, b_out, pe, *consts)


def _head_masked_queries(qi_ref):
    lane_head = lax.broadcasted_iota(I32, (CHUNK, LANES), 1) >> 5
    heads_per_col = LANES // IDX_DIM
    out = []
    for h in range(IDX_HEADS):
        col = qi_ref[:, (h // heads_per_col) * LANES:(h // heads_per_col + 1) * LANES]
        out.append(jnp.where(lane_head == (h % heads_per_col), col, jnp.zeros_like(col)))
    return out


def _index_scores(qh, wsc, kc):
    acc = None
    for h in range(IDX_HEADS):
        s = _dot_nt(qh[h], kc)
        t = jnp.maximum(s, 0.0) * wsc[:, h:h + 1]
        acc = t if acc is None else acc + t
    return acc


def _count_ge(keys_ref, nch, cw, cand, strict=False):
    def body(c, acc):
        off = pl.multiple_of(c * cw, cw)
        for j in range(cw // LANES):
            kk = keys_ref[:, pl.ds(off + j * LANES, LANES)]
            hit = (kk > cand) if strict else (kk >= cand)
            acc = acc + jnp.where(hit, 1.0, 0.0)
        return acc
    acc = lax.fori_loop(0, nch, body, jnp.zeros((CHUNK, LANES), F32))
    cnt = jnp.sum(acc, axis=-1, keepdims=True)
    return jnp.broadcast_to(cnt, (CHUNK, LANES))


def _select_threshold(keys_ref, nch, cw, topk):
    def bit_body(i, prefix):
        cand = prefix + lax.shift_left(jnp.int32(1), 31 - i)
        cnt = _count_ge(keys_ref, nch, cw, cand)
        return jnp.where(cnt >= topk, cand, prefix)
    prefix = lax.fori_loop(0, 32, bit_body, jnp.full((CHUNK, LANES), INT_MIN, I32))
    thr = jnp.maximum(prefix, INT_MIN + 1)
    cnt_ge = _count_ge(keys_ref, nch, cw, thr)

    @pl.when(jnp.max(cnt_ge) > topk)
    def _():
        room = topk - _count_ge(keys_ref, nch, cw, thr, strict=True)
        r_i = lax.broadcasted_iota(I32, (LANES, LANES), 0)
        c_i = lax.broadcasted_iota(I32, (LANES, LANES), 1)
        upper = jnp.where(r_i <= c_i, 1.0, 0.0).astype(BF16)

        def body(c, seen):
            off = pl.multiple_of(c * cw, cw)
            for j in range(cw // LANES):
                kk = keys_ref[:, pl.ds(off + j * LANES, LANES)]
                tie = kk == thr
                rank = seen + _dot(jnp.where(tie, 1.0, 0.0).astype(BF16), upper)
                keys_ref[:, pl.ds(off + j * LANES, LANES)] = jnp.where(tie & (rank > room), thr - 1, kk)
                seen = jnp.broadcast_to(rank[:, LANES - 1:LANES], (CHUNK, LANES))
            return seen
        lax.fori_loop(0, nch, body, jnp.zeros((CHUNK, LANES), F32))

    return thr


def _sample_idx_kernel(pt_ref, qs_ref, w_ref, *refs, pages_per_step):
    page_refs, out_ref = refs[:pages_per_step], refs[pages_per_step]
    qs = qs_ref[0]
    w = w_ref[0]
    nt = out_ref.shape[1]
    for j in range(pages_per_step):
        kp = page_refs[j][...].astype(BF16)
        t = jnp.maximum(_dot(qs, kp), 0.0) * w
        for q in range(nt):
            sc = jnp.sum(t[q * IDX_HEADS:(q + 1) * IDX_HEADS, :], axis=0, keepdims=True)
            out_ref[0, q:q + 1, j * LANES:(j + 1) * LANES] = _sortable(sc)


def _sample_idx_call(page_table, qs, wrep, cache_kidx, layer, pps):
    db, npg = page_table.shape
    nt = qs.shape[1] // IDX_HEADS
    page = cache_kidx.shape[3]
    kern = functools.partial(_sample_idx_kernel, pages_per_step=pps)

    def page_spec(j):
        return pl.BlockSpec((None, None, IDX_DIM, page), lambda b, s, pt: (layer, pt[b, s * pps + j], 0, 0))

    grid_spec = pltpu.PrefetchScalarGridSpec(
        num_scalar_prefetch=1,
        grid=(db, npg // pps),
        in_specs=[pl.BlockSpec((1,) + qs.shape[1:], lambda b, s, pt: (b, 0, 0)),
                  pl.BlockSpec((1,) + wrep.shape[1:], lambda b, s, pt: (b, 0, 0))]
                 + [page_spec(j) for j in range(pps)],
        out_specs=pl.BlockSpec((1, nt, pps * page), lambda b, s, pt: (b, 0, s)),
    )
    return pl.pallas_call(
        kern, grid_spec=grid_spec,
        out_shape=jax.ShapeDtypeStruct((db, nt, npg * page), I32),
        compiler_params=pltpu.CompilerParams(dimension_semantics=("arbitrary", "arbitrary")),
        name="sample_idx",
    )(page_table, qs, wrep, *([cache_kidx] * pps))


def _sample_select_kernel(kp_ref, qi_ref, wi_ref, ki_ref, bp_ref, bn_ref, keys_ref, *, topk, cw, nt):
    past = kp_ref.shape[1]
    width = keys_ref.shape[1]
    nch = width // cw
    keys_ref[:, 0:past] = kp_ref[...]
    qh = _head_masked_queries(qi_ref)
    wsc = wi_ref[...] * IDX_SCALE
    key = _sortable(_index_scores(qh, wsc, ki_ref[...].astype(BF16)))
    r_i = lax.broadcasted_iota(I32, (CHUNK, LANES), 0)
    c_i = lax.broadcasted_iota(I32, (CHUNK, LANES), 1)
    visible = ((r_i // nt) == (c_i // nt)) & ((c_i % nt) <= (r_i % nt))
    keys_ref[:, past:past + LANES] = jnp.where(visible, key, INT_MIN)
    if width > past + LANES:
        keys_ref[:, past + LANES:width] = jnp.full((CHUNK, width - past - LANES), INT_MIN, I32)

    thr = _select_threshold(keys_ref, nch, cw, topk)

    for c in range(past // LANES):
        bp_ref[:, c * LANES:(c + 1) * LANES] = jnp.where(
            keys_ref[:, c * LANES:(c + 1) * LANES] >= thr, 0.0, NEG)
    bn_ref[...] = jnp.where(keys_ref[:, past:past + LANES] >= thr, 0.0, NEG)


def _sample_select_call(keys_past, qi, wi, ki4, topk, nt, cw):
    rows, past = keys_past.shape
    width = -(-(past + LANES) // cw) * cw
    kern = functools.partial(_sample_select_kernel, topk=topk, cw=cw, nt=nt)
    return pl.pallas_call(
        kern,
        out_shape=[jax.ShapeDtypeStruct((rows, past), F32), jax.ShapeDtypeStruct((rows, LANES), F32)],
        scratch_shapes=[pltpu.VMEM((rows, width), I32)],
        compiler_params=pltpu.CompilerParams(vmem_limit_bytes=VMEM_LIMIT),
        name="sample_select",
    )(keys_past, qi, wi, ki4)


def _sample_attn_kernel(pt_ref, qz_ref, bp_ref, bn_ref, kn_ref, vn_ref, *refs, pages_per_step, nt):
    pps = pages_per_step
    k_refs, v_refs = refs[:pps], refs[pps:2 * pps]
    o_ref, m_ref, l_ref, acc_ref = refs[2 * pps:]
    step = pl.program_id(1)
    qz = qz_ref[0]
    rows = qz.shape[0]
    hpq = rows // nt

    @pl.when(step == 0)
    def _():
        m_ref[...] = jnp.full(m_ref.shape, NEG, F32)
        l_ref[...] = jnp.zeros(l_ref.shape, F32)
        acc_ref[...] = jnp.zeros(acc_ref.shape, F32)

    def update(s, pv_fn, bias):
        width = s.shape[1]
        s = jnp.concatenate(
            [s[q * hpq:(q + 1) * hpq, :] + bias[q:q + 1, :] for q in range(nt)], axis=0)
        m_old = m_ref[...]
        m_new = jnp.maximum(m_old, jnp.max(s, axis=-1, keepdims=True))
        alpha = jnp.exp(m_old - m_new)
        pr = jnp.exp(s - jnp.concatenate([m_new] * (width // LANES), axis=1))
        l_ref[...] = alpha * l_ref[...] + jnp.sum(pr, axis=-1, keepdims=True)
        m_ref[...] = m_new
        acc_ref[...] = alpha * acc_ref[...] + pv_fn(pr.astype(BF16))

    kb = jnp.concatenate([r[...].astype(BF16) for r in k_refs], axis=1)
    vb = jnp.concatenate([r[...].astype(BF16) for r in v_refs], axis=1)
    update(_dot(qz, kb), lambda pr: _dot_nt(pr, vb), bp_ref[0])

    @pl.when(step == pl.num_programs(1) - 1)
    def _():
        vn = vn_ref[...].astype(BF16)
        update(_dot_nt(qz, kn_ref[...].astype(BF16)), lambda pr: _dot(pr, vn), bn_ref[0])
        o_ref[0] = acc_ref[...] / l_ref[...]


def _sample_attn_call(page_table, qz, bias_past, bias_new, k_new, v_new, cache_k, cache_v, layer, pps, nt):
    db, npg = page_table.shape
    page = cache_k.shape[3]
    rows = qz.shape[1]
    kern = functools.partial(_sample_attn_kernel, pages_per_step=pps, nt=nt)

    def page_spec(j):
        return pl.BlockSpec((None, None, LANES, page), lambda b, s, pt: (layer, pt[b, s * pps + j], 0, 0))

    grid_spec = pltpu.PrefetchScalarGridSpec(
        num_scalar_prefetch=1,
        grid=(db, npg // pps),
        in_specs=[pl.BlockSpec((1, rows, LANES), lambda b, s, pt: (b, 0, 0)),
                  pl.BlockSpec((1, nt, pps * page), lambda b, s, pt: (b, 0, s)),
                  pl.BlockSpec((1, nt, LANES), lambda b, s, pt: (b, 0, 0)),
                  pl.BlockSpec(k_new.shape, lambda b, s, pt: (0, 0)),
                  pl.BlockSpec(v_new.shape, lambda b, s, pt: (0, 0))]
                 + [page_spec(j) for j in range(pps)] * 2,
        out_specs=pl.BlockSpec((1, rows, LANES), lambda b, s, pt: (b, 0, 0)),
        scratch_shapes=[pltpu.VMEM((rows, LANES), F32)] * 3,
    )
    return pl.pallas_call(
        kern, grid_spec=grid_spec,
        out_shape=jax.ShapeDtypeStruct((db, rows, LANES), F32),
        compiler_params=pltpu.CompilerParams(dimension_semantics=("arbitrary", "arbitrary")),
        name="sample_attn",
    )(page_table, qz, bias_past, bias_new, k_new, v_new, *([cache_k] * pps), *([cache_v] * pps))


def _swap_halves(w, head_dim):
    lead = w.shape[:-1]
    nh = w.shape[-1] // head_dim
    return w.reshape(lead + (nh, 2, head_dim // 2))[..., ::-1, :].reshape(lead + (nh * head_dim,))


def _projection_weights(w_in):
    sizes = (A_WIDTH, A_WIDTH, N_HEADS * HEAD_DIM, N_KV_HEADS * HEAD_DIM, N_KV_HEADS * HEAD_DIM,
             IDX_HEADS * IDX_DIM, IDX_DIM, IDX_HEADS)
    pts = [sum(sizes[:i + 1]) for i in range(len(sizes) - 1)]
    au, av, q, k, v, qi, ki, wi = jnp.split(w_in.astype(BF16), pts, axis=-1)
    rep = LANES // IDX_DIM
    q_s, k_s, qi_s, ki_s = (_swap_halves(q, HEAD_DIM), _swap_halves(k, HEAD_DIM),
                            _swap_halves(qi, IDX_DIM), _swap_halves(ki, IDX_DIM))
    ki4, ki4_s = jnp.tile(ki, (1, 1, rep)), jnp.tile(ki_s, (1, 1, rep))
    pad_cols = lambda w, n: jnp.pad(w, ((0, 0), (0, 0), (0, n - w.shape[-1])))
    ext = jnp.concatenate([au, av, q, k, v, qi, ki4, pad_cols(wi, LANES), q_s, k_s, qi_s, ki4_s], axis=-1)
    wn = jnp.concatenate([au, av, k, k_s, ki4, ki4_s], axis=-1)
    wt = jnp.swapaxes(jnp.concatenate([q, q_s, k, k_s, v, qi, qi_s, ki, ki_s, pad_cols(wi, WI_ROWS)], axis=-1),
                      1, 2)
    assert ext.shape[-1] == C_END and wn.shape[-1] == N_END and wt.shape[1] == T_END
    return ext, wn, wt


def _rope_tables(pos, head_dim):
    half = head_dim // 2
    inv = ROPE_THETA ** (-jnp.arange(half, dtype=F32) / half)
    ang = pos.astype(F32)[:, None] * inv[None, :]
    cos, sin = jnp.cos(ang), jnp.sin(ang)
    reps = LANES // head_dim
    return (jnp.tile(jnp.concatenate([cos, cos], axis=-1), (1, reps)),
            jnp.tile(jnp.concatenate([-sin, sin], axis=-1), (1, reps)))


def _pick(n, prefs):
    for p in prefs:
        if n % p == 0:
            return p
    return n


def kernel(x_prompt, x_sample, p_prompt, p_sample, cache_k, cache_v, cache_kidx, page_table, w_in, sgu_ln_g,
           sgu_ln_b, sgu_w, sgu_b, w_o, ln1_g, ln1_b, w_ff1, w_ff2, w_ple_gate, w_ple_proj, ln2_g, ln2_b):
    nb, seq, _ = x_prompt.shape
    db, nt, _ = x_sample.shape
    depth = w_in.shape[0]
    npg = page_table.shape[1]
    page = cache_k.shape[2]
    past = npg * page
    alpha = (2.0 * depth) ** 0.25
    topk_p = min(TOPK_MAX, seq // 4)
    topk_s = min(TOPK_MAX, (past + nt) // 4)
    n_p, n_s = nb * seq, db * nt
    assert n_s == CHUNK and seq % COUNT_CHUNK == 0 and page == LANES

    w_ext, w_n, w_t = _projection_weights(w_in)
    wo_b, wf1_b, wf2_b = w_o.astype(BF16), w_ff1.astype(BF16), w_ff2.astype(BF16)
    wg_b, wp_b = w_ple_gate.astype(BF16), w_ple_proj.astype(BF16)
    tril = jnp.tril(jnp.ones((CHUNK, CHUNK), F32))
    mm_p = (sgu_w * tril).astype(BF16)
    mb_p = jnp.repeat(jnp.swapaxes(sgu_b, 1, 2), A_GW, axis=2)
    r_i = jnp.arange(CHUNK)
    same_req = (r_i[:, None] // nt == r_i[None, :] // nt) & (r_i[None, :] % nt <= r_i[:, None] % nt)
    mm_s = jnp.where(same_req, sgu_w[:, :, r_i % nt][:, :, :, r_i % nt], 0.0).astype(BF16)
    mb_s = jnp.tile(jnp.repeat(jnp.swapaxes(sgu_b[:, :, :nt], 1, 2), A_GW, axis=2), (1, CHUNK // nt, 1))
    lng = sgu_ln_g.reshape(depth, 1, A_WIDTH)
    lnb = sgu_ln_b.reshape(depth, 1, A_WIDTH)
    mlp_w = (wo_b, ln1_g.reshape(depth, 1, D_MODEL), ln1_b.reshape(depth, 1, D_MODEL), wf1_b, wf2_b, wg_b, wp_b,
             ln2_g.reshape(depth, 1, D_MODEL), ln2_b.reshape(depth, 1, D_MODEL))
    pe_p = p_prompt.reshape(depth, n_p, D_PLE)
    pe_s = p_sample.reshape(depth, n_s, D_PLE)

    pos_p = jnp.arange(seq, dtype=I32)
    pos_s = past + (jnp.arange(n_s, dtype=I32) % nt)
    tabs_p = _rope_tables(pos_p, HEAD_DIM) + _rope_tables(pos_p, IDX_DIM)
    tabs_pt = tuple(t.T for t in tabs_p)
    tabs_s = _rope_tables(pos_s, HEAD_DIM) + _rope_tables(pos_s, IDX_DIM)

    cki = jnp.swapaxes(cache_kidx, 2, 3)
    ck = jnp.transpose(cache_k, (0, 1, 3, 4, 2)).reshape(cache_k.shape[:2] + (LANES, page))
    cv = jnp.transpose(cache_v, (0, 1, 3, 4, 2)).reshape(cache_v.shape[:2] + (LANES, page))

    rt_proj = _pick(seq, (512, 256, 128))
    rt_mlp = _pick(n_p, (512, 256, 128))
    qw_p = _pick(seq, (256, 128))
    cw_s = 256
    pps = _pick(npg, (64, 32, 16, 8, 4, 2, 1))
    head_lo = (jnp.arange(N_HEADS) // KV_GROUP) * HEAD_DIM

    hp = x_prompt.reshape(n_p, D_MODEL)
    hs = x_sample.reshape(n_s, D_MODEL)
    outs = [[] for _ in range(8)]
    for i in range(depth):
        a_out, vn, qt, kb, kt, vt, qit, kib, kit, wit = _proj_prompt_call(
            hp, i, w_n, w_t, mm_p, mb_p, lng, lnb, tabs_p, tabs_pt, nb, seq, rt_proj)
        b_out = _prompt_attn_call(qit, wit, qt, kib, kb, vt, nb, seq, topk_p, qw_p)
        hp = _mlp_call(hp, a_out, b_out, pe_p, i, mlp_w, alpha, rt_mlp)
        to_tokens = lambda a: jnp.transpose(a.reshape(nb, N_KV_HEADS, HEAD_DIM, seq), (0, 3, 1, 2))
        outs[0].append(to_tokens(kt))
        outs[1].append(to_tokens(vt))
        outs[2].append(jnp.swapaxes(kit, 1, 2))
        outs[3].append(vn)

        a_out, vn, q, k, v, qi, ki4, wi = _proj_call(hs, i, w_ext, mm_s, mb_s, lng, lnb, tabs_s)
        qs = qi.reshape(db, nt * IDX_HEADS, IDX_DIM)
        wrep = jnp.broadcast_to((wi[:, :IDX_HEADS] * IDX_SCALE).reshape(db, nt * IDX_HEADS, 1),
                                (db, nt * IDX_HEADS, LANES))
        keys_past = _sample_idx_call(page_table, qs, wrep, cki, i, pps)
        bias_past, bias_new = _sample_select_call(keys_past.reshape(n_s, past), qi, wi, ki4, topk_s, nt, cw_s)
        q4 = q.reshape(db, nt, N_HEADS, 1, HEAD_DIM)
        half = (jnp.arange(N_KV_HEADS) * HEAD_DIM)[None, None, None, :, None]
        qz = jnp.where(half == head_lo[None, None, :, None, None], q4, jnp.zeros_like(q4))
        qz = qz.reshape(db, nt * N_HEADS, LANES)
        o = _sample_attn_call(page_table, qz, bias_past.reshape(db, nt, past), bias_new.reshape(db, nt, LANES),
                              k, v, ck, cv, i, pps, nt)
        o = o.reshape(db, nt, N_HEADS, N_KV_HEADS, HEAD_DIM)
        b_out = jnp.concatenate([o[:, :, g * KV_GROUP:(g + 1) * KV_GROUP, g, :] for g in range(N_KV_HEADS)], axis=2)
        b_out = b_out.reshape(n_s, N_HEADS * HEAD_DIM).astype(BF16)
        hs = _mlp_call(hs, a_out, b_out, pe_s, i, mlp_w, alpha, n_s)
        outs[4].append(k.reshape(db, nt, N_KV_HEADS, HEAD_DIM))
        outs[5].append(v.reshape(db, nt, N_KV_HEADS, HEAD_DIM))
        outs[6].append(ki4[:, :IDX_DIM].reshape(db, nt, IDX_DIM))
        outs[7].append(vn.reshape(db, nt, A_WIDTH))

    st = [jnp.stack(o) for o in outs]
    return (hp.reshape(nb, seq, D_MODEL), hs.reshape(db, nt, D_MODEL),
            st[0], st[1], st[2], st[3], st[4], st[5], st[6], st[7])
```

```python
import functools

import jax
import jax.numpy as jnp
from jax import lax
from jax.experimental import pallas as pl
from jax.experimental.pallas import tpu as pltpu

F32 = jnp.float32
BF16 = jnp.bfloat16
I32 = jnp.int32
I16 = jnp.int16

D_MODEL = 1024
CHUNK = 128
A_WIDTH = 512
A_GROUPS = 4
A_GW = A_WIDTH // A_GROUPS
HEAD_DIM = 64
N_HEADS = 8
N_KV_HEADS = 2
KV_GROUP = N_HEADS // N_KV_HEADS
IDX_HEADS = 8
IDX_DIM = 32
TOPK_MAX = 256
ROPE_THETA = 10000.0
D_FF = 4 * D_MODEL
D_PLE = 256
LN_EPS = 1e-5

LANES = 128
PACK = 16
VMEM_LIMIT = 56 * 1024 * 1024

INT_MIN = -(2 ** 31)
I16_MIN = -(2 ** 15)
NEG = -1e30
DENOM_FLOOR = 2.0 ** -100

C_AU, C_AV, C_Q, C_K, C_V, C_QI, C_KI, C_WI = 0, 512, 1024, 1536, 1664, 1792, 2048, 2176
C_QS, C_KS, C_QIS, C_KIS, C_END = 2304, 2816, 2944, 3200, 3328
N_AU, N_AV, N_K, N_KS, N_KI, N_KIS, N_END = 0, 512, 1024, 1152, 1280, 1408, 1536
T_Q, T_QS, T_K, T_KS, T_V, T_QI, T_QIS, T_KI, T_KIS, T_WI, T_END = (
    0, 512, 1024, 1152, 1280, 1408, 1664, 1920, 1952, 1984, 2000)
WI_ROWS = T_END - T_WI
Q_SCALE_LOG2 = (HEAD_DIM ** -0.5) * 1.4426950408889634
IDX_SCALE = (IDX_DIM ** -0.5) * (IDX_HEADS ** -0.5)

KEY_CHUNK = 512
COUNT_CHUNK = 512

NT_DIMS = (((1,), (1,)), ((), ()))


def _dot(a, b):
    return jnp.dot(a, b, preferred_element_type=F32)


def _dot_nt(a, b):
    return lax.dot_general(a, b, NT_DIMS, preferred_element_type=F32)


def _layer_norm(x, g, b):
    mu = jnp.mean(x, axis=-1, keepdims=True)
    xc = x - mu
    var = jnp.mean(xc * xc, axis=-1, keepdims=True)
    return xc * lax.rsqrt(var + LN_EPS) * g + b


def _gelu(x):
    return 0.5 * x * (1.0 + lax.erf(x * (2.0 ** -0.5)))


def _sortable(x):
    bits = pltpu.bitcast(x, I32)
    return bits ^ ((bits >> 31) & 0x7FFFFFFF)


def _layer_spec(stacked, layer):
    shape = stacked.shape[1:]
    return pl.BlockSpec((None,) + shape, lambda *_: (layer,) + (0,) * len(shape),
                        pipeline_mode=pl.Buffered(1))


def _gated_mixer(au, av, g, lng_ref, lnb_ref, mm_ref, mb_ref, vn_ref, aout_ref):
    rows = au.shape[0]
    lo = g * A_GW
    vn = _layer_norm(av, lng_ref[:, lo:lo + A_GW], lnb_ref[:, lo:lo + A_GW])
    vn_ref[:, lo:lo + A_GW] = vn[rows - vn_ref.shape[0]:, :]
    vnb = vn.astype(BF16)
    for c in range(rows // CHUNK):
        r0 = c * CHUNK
        gate = _dot(mm_ref[g], vnb[r0:r0 + CHUNK, :]) + mb_ref[:, lo:lo + A_GW]
        aout_ref[r0:r0 + CHUNK, lo:lo + A_GW] = (au[r0:r0 + CHUNK, :] * gate).astype(BF16)


def _proj_kernel(h_ref, w_ref, mm_ref, mb_ref, lng_ref, lnb_ref, cq_ref, sq_ref, ci_ref, si_ref,
                 aout_ref, vn_ref, q_ref, k_ref, v_ref, qi_ref, ki_ref, wi_ref):
    hb = h_ref[...].astype(BF16)

    def proj(lo, width=LANES):
        return _dot(hb, w_ref[:, lo:lo + width])

    for g in range(A_GROUPS):
        lo = g * A_GW
        _gated_mixer(_gelu(proj(C_AU + lo)), _gelu(proj(C_AV + lo)), g,
                     lng_ref, lnb_ref, mm_ref, mb_ref, vn_ref, aout_ref)

    cq, sq = cq_ref[...], sq_ref[...]
    ci, si = ci_ref[...], si_ref[...]
    for j in range(N_HEADS * HEAD_DIM // LANES):
        z = proj(C_Q + j * LANES) * cq + proj(C_QS + j * LANES) * sq
        q_ref[:, j * LANES:(j + 1) * LANES] = (z * (HEAD_DIM ** -0.5)).astype(BF16)
    k_ref[...] = proj(C_K) * cq + proj(C_KS) * sq
    v_ref[...] = proj(C_V)
    for j in range(IDX_HEADS * IDX_DIM // LANES):
        z = proj(C_QI + j * LANES) * ci + proj(C_QIS + j * LANES) * si
        qi_ref[:, j * LANES:(j + 1) * LANES] = z.astype(BF16)
    ki_ref[...] = proj(C_KI) * ci + proj(C_KIS) * si
    wi_ref[...] = proj(C_WI)


def _proj_call(h2d, layer, w_ext, mm, mb, lng, lnb, tabs):
    n = h2d.shape[0]
    whole = lambda a: pl.BlockSpec(a.shape, lambda i: (0, 0))
    out_widths = [(A_WIDTH, BF16), (A_WIDTH, F32), (N_HEADS * HEAD_DIM, BF16), (LANES, F32), (LANES, F32),
                  (IDX_HEADS * IDX_DIM, BF16), (LANES, F32), (LANES, F32)]
    return pl.pallas_call(
        _proj_kernel,
        grid=(1,),
        in_specs=[whole(h2d)] + [_layer_spec(a, layer) for a in (w_ext, mm, mb, lng, lnb)]
                 + [whole(t) for t in tabs],
        out_specs=[pl.BlockSpec((n, w), lambda i: (0, 0)) for w, _ in out_widths],
        out_shape=[jax.ShapeDtypeStruct((n, w), dt) for w, dt in out_widths],
        compiler_params=pltpu.CompilerParams(dimension_semantics=("arbitrary",), vmem_limit_bytes=VMEM_LIMIT),
        name="proj",
    )(h2d, w_ext, mm, mb, lng, lnb, *tabs)


def _proj_prompt_kernel(h_ref, wn_ref, wt_ref, mm_ref, mb_ref, lng_ref, lnb_ref,
                        cq_ref, sq_ref, ci_ref, si_ref, cqt_ref, sqt_ref, cit_ref, sit_ref,
                        aout_ref, vn_ref, qt_ref, kb_ref, kt_ref, vt_ref, qit_ref, kib_ref, kit_ref, wit_ref):
    hb = h_ref[...].astype(BF16)
    wide = 2 * LANES

    for gp in range(A_GROUPS // 2):
        au2 = _gelu(_dot(hb, wn_ref[:, N_AU + gp * wide:N_AU + (gp + 1) * wide]))
        av2 = _gelu(_dot(hb, wn_ref[:, N_AV + gp * wide:N_AV + (gp + 1) * wide]))
        for e in range(2):
            _gated_mixer(au2[:, e * A_GW:(e + 1) * A_GW], av2[:, e * A_GW:(e + 1) * A_GW], 2 * gp + e,
                         lng_ref, lnb_ref, mm_ref, mb_ref, vn_ref, aout_ref)

    z = _dot(hb, wn_ref[:, N_K:N_K + wide])
    kb_ref[...] = (z[:, :LANES] * cq_ref[...] + z[:, LANES:] * sq_ref[...]).astype(BF16)
    z = _dot(hb, wn_ref[:, N_KI:N_KI + wide])
    kib_ref[...] = (z[:, :LANES] * ci_ref[...] + z[:, LANES:] * si_ref[...]).astype(BF16)

    zt = _dot_nt(wt_ref[...], hb)
    cqt, sqt = cqt_ref[...], sqt_ref[...]
    cit, sit = cit_ref[...], sit_ref[...]
    for j in range(N_HEADS * HEAD_DIM // LANES):
        r = j * LANES
        q = zt[T_Q + r:T_Q + r + LANES] * cqt + zt[T_QS + r:T_QS + r + LANES] * sqt
        qt_ref[r:r + LANES, :] = (q * Q_SCALE_LOG2).astype(BF16)
    kt_ref[...] = zt[T_K:T_K + LANES] * cqt + zt[T_KS:T_KS + LANES] * sqt
    vt_ref[...] = zt[T_V:T_V + LANES]
    for j in range(IDX_HEADS * IDX_DIM // LANES):
        r = j * LANES
        qi = zt[T_QI + r:T_QI + r + LANES] * cit + zt[T_QIS + r:T_QIS + r + LANES] * sit
        qit_ref[r:r + LANES, :] = qi.astype(BF16)
    kit_ref[...] = zt[T_KI:T_KI + IDX_DIM] * cit[:IDX_DIM] + zt[T_KIS:T_KIS + IDX_DIM] * sit[:IDX_DIM]
    wit_ref[...] = zt[T_WI:T_END]


def _proj_prompt_call(h2d, layer, wn, wt, mm, mb, lng, lnb, tabs, tabs_t, nb, seq, row_tile):
    n = nb * seq
    ns = seq // row_tile
    row_map = lambda b, s: (b * ns + s, 0)
    col_map = lambda b, s: (0, b * ns + s)
    bcol_map = lambda b, s: (b, 0, s)
    hd, idd = N_HEADS * HEAD_DIM, IDX_HEADS * IDX_DIM
    out_specs = [pl.BlockSpec((row_tile, A_WIDTH), row_map),
                 pl.BlockSpec((None, CHUNK, A_WIDTH), lambda b, s: (b, 0, 0)),
                 pl.BlockSpec((hd, row_tile), col_map),
                 pl.BlockSpec((row_tile, LANES), row_map),
                 pl.BlockSpec((None, LANES, row_tile), bcol_map),
                 pl.BlockSpec((None, LANES, row_tile), bcol_map),
                 pl.BlockSpec((idd, row_tile), col_map),
                 pl.BlockSpec((row_tile, LANES), row_map),
                 pl.BlockSpec((None, IDX_DIM, row_tile), bcol_map),
                 pl.BlockSpec((WI_ROWS, row_tile), col_map)]
    out_shape = [jax.ShapeDtypeStruct((n, A_WIDTH), BF16), jax.ShapeDtypeStruct((nb, CHUNK, A_WIDTH), F32),
                 jax.ShapeDtypeStruct((hd, n), BF16),
                 jax.ShapeDtypeStruct((n, LANES), BF16),
                 jax.ShapeDtypeStruct((nb, LANES, seq), F32),
                 jax.ShapeDtypeStruct((nb, LANES, seq), F32),
                 jax.ShapeDtypeStruct((idd, n), BF16),
                 jax.ShapeDtypeStruct((n, LANES), BF16),
                 jax.ShapeDtypeStruct((nb, IDX_DIM, seq), F32),
                 jax.ShapeDtypeStruct((WI_ROWS, n), F32)]
    return pl.pallas_call(
        _proj_prompt_kernel,
        grid=(nb, ns),
        in_specs=[pl.BlockSpec((row_tile, D_MODEL), row_map)]
                 + [_layer_spec(a, layer) for a in (wn, wt, mm, mb, lng, lnb)]
                 + [pl.BlockSpec((row_tile, LANES), lambda b, s: (s, 0))] * 4
                 + [pl.BlockSpec((LANES, row_tile), lambda b, s: (0, s))] * 4,
        out_specs=out_specs,
        out_shape=out_shape,
        compiler_params=pltpu.CompilerParams(dimension_semantics=("arbitrary", "arbitrary"),
                                             vmem_limit_bytes=VMEM_LIMIT),
        name="proj_prompt",
    )(h2d, wn, wt, mm, mb, lng, lnb, *tabs, *tabs_t)


def _count16(ref, nsteps, cand, strict=False):
    one, zero = jnp.ones(cand.shape, I16), jnp.zeros(cand.shape, I16)
    n_acc = 4

    def body(c, accs):
        off = pl.multiple_of(c * COUNT_CHUNK, COUNT_CHUNK)
        blk = ref[pl.ds(off, COUNT_CHUNK), :]
        accs = list(accs)
        for j in range(COUNT_CHUNK // PACK):
            kk = blk[j * PACK:(j + 1) * PACK, :]
            hit = (kk > cand) if strict else (kk >= cand)
            accs[j % n_acc] = accs[j % n_acc] + jnp.where(hit, one, zero)
        return tuple(accs)
    accs = lax.fori_loop(0, nsteps, body, (zero,) * n_acc)
    tot = (accs[0] + accs[1]) + (accs[2] + accs[3])
    return jnp.sum(tot.astype(F32), axis=0, keepdims=True)


def _search16(ref, nsteps, target):
    qw = ref.shape[1]

    def bit_body(i, prefix):
        cand = prefix + lax.shift_left(jnp.int32(1), 15 - i)
        cand16 = jnp.broadcast_to(cand, (PACK, qw)).astype(I16)
        return jnp.where(_count16(ref, nsteps, cand16) >= target, cand, prefix)
    return lax.fori_loop(0, 16, bit_body, jnp.full((1, qw), I16_MIN, I32))


def _prompt_attn_kernel(qit_ref, wit_ref, qt_ref, kib_ref, kb_ref, vt_ref, o_ref,
                        keys_ref, hi_ref, lo_ref, rhs_ref, qz_ref, acc_ref, kmax_ref, *, topk):
    qw = o_ref.shape[0]
    qb = pl.program_id(1)
    nch = (qb * qw + qw + KEY_CHUNK - 1) // KEY_CHUNK
    nsteps = (nch * KEY_CHUNK + COUNT_CHUNK - 1) // COUNT_CHUNK
    qpos = qb * qw + lax.broadcasted_iota(I32, (KEY_CHUNK, qw), 1)
    krow = lax.broadcasted_iota(I32, (KEY_CHUNK, qw), 0)

    rhs_ref[...] = jnp.zeros(rhs_ref.shape, BF16)
    qz_ref[...] = jnp.zeros(qz_ref.shape, BF16)
    slots = LANES // IDX_DIM
    for h in range(IDX_HEADS):
        r = (h % slots) * IDX_DIM
        rhs_ref[r:r + IDX_DIM, h * qw:(h + 1) * qw] = qit_ref[h * IDX_DIM:(h + 1) * IDX_DIM, :]
    for h in range(N_HEADS):
        r = (h // KV_GROUP) * HEAD_DIM
        qz_ref[r:r + HEAD_DIM, h * qw:(h + 1) * qw] = qt_ref[h * HEAD_DIM:(h + 1) * HEAD_DIM, :]
    wsc = wit_ref[...] * IDX_SCALE

    def score_chunk(c, causal_edge):
        off = pl.multiple_of(c * KEY_CHUNK, KEY_CHUNK)
        kc = kib_ref[pl.ds(off, KEY_CHUNK), :]
        acc = None
        for h in range(IDX_HEADS):
            s = _dot(kc, rhs_ref[:, h * qw:(h + 1) * qw])
            t = jnp.maximum(s, 0.0) * wsc[h:h + 1, :]
            acc = t if acc is None else acc + t
        key = _sortable(acc)
        if causal_edge:
            key = jnp.where(krow + off <= qpos, key, INT_MIN)
        keys_ref[pl.ds(off, KEY_CHUNK), :] = key
        hi_ref[pl.ds(off, KEY_CHUNK), :] = (key >> 16).astype(I16)
        lo_ref[pl.ds(off, KEY_CHUNK), :] = (key ^ 0x8000).astype(I16)

    def score_body(c, carry):
        score_chunk(c, False)
        return carry
    lax.fori_loop(0, nch - 1, score_body, 0)
    score_chunk(nch - 1, True)

    @pl.when(nch * KEY_CHUNK < nsteps * COUNT_CHUNK)
    def _():
        off = pl.multiple_of(nch * KEY_CHUNK, KEY_CHUNK)
        fill = jnp.full((KEY_CHUNK, qw), I16_MIN, I16)
        hi_ref[pl.ds(off, KEY_CHUNK), :] = fill
        lo_ref[pl.ds(off, KEY_CHUNK), :] = fill

    p_hi = _search16(hi_ref, nsteps, float(topk))
    p16 = jnp.broadcast_to(p_hi, (PACK, qw)).astype(I16)
    room = topk - _count16(hi_ref, nsteps, p16, strict=True)

    def low_body(c, carry):
        off = pl.multiple_of(c * COUNT_CHUNK, COUNT_CHUNK)
        lowest = jnp.full((PACK, qw), I16_MIN, I16)
        his, los = hi_ref[pl.ds(off, COUNT_CHUNK), :], lo_ref[pl.ds(off, COUNT_CHUNK), :]
        lo_ref[pl.ds(off, COUNT_CHUNK), :] = jnp.concatenate(
            [jnp.where(his[j * PACK:(j + 1) * PACK, :] == p16, los[j * PACK:(j + 1) * PACK, :], lowest)
             for j in range(COUNT_CHUNK // PACK)], axis=0)
        return carry
    lax.fori_loop(0, nsteps, low_body, 0)
    p_lo = _search16(lo_ref, nsteps, room)
    thr = jnp.maximum(p_hi * 65536 + (p_lo + 2 ** 15), INT_MIN + 1)

    def count32(strict):
        def body(c, acc):
            off = pl.multiple_of(c * KEY_CHUNK, KEY_CHUNK)
            kk = keys_ref[pl.ds(off, KEY_CHUNK), :]
            hit = (kk > thr) if strict else (kk >= thr)
            return acc + jnp.sum(jnp.where(hit, 1.0, 0.0), axis=0, keepdims=True)
        return lax.fori_loop(0, nch, body, jnp.zeros((1, qw), F32))

    @pl.when(jnp.max(count32(False)) > topk)
    def _():
        spare = topk - count32(True)
        r_i = lax.broadcasted_iota(I32, (KEY_CHUNK, KEY_CHUNK), 0)
        c_i = lax.broadcasted_iota(I32, (KEY_CHUNK, KEY_CHUNK), 1)
        lower = jnp.where(c_i <= r_i, 1.0, 0.0).astype(BF16)

        def body(c, seen):
            off = pl.multiple_of(c * KEY_CHUNK, KEY_CHUNK)
            kk = keys_ref[pl.ds(off, KEY_CHUNK), :]
            tie = kk == thr
            rank = seen + _dot(lower, jnp.where(tie, 1.0, 0.0).astype(BF16))
            keys_ref[pl.ds(off, KEY_CHUNK), :] = jnp.where(tie & (rank > spare), thr - 1, kk)
            return rank[KEY_CHUNK - 1:KEY_CHUNK, :]
        lax.fori_loop(0, nch, body, jnp.zeros((1, qw), F32))

    ones_rows = jnp.where(lax.broadcasted_iota(I32, (PACK, KEY_CHUNK), 0) == 0, 1.0, 0.0).astype(BF16)

    @pl.when(qb == 0)
    def _():
        kf = kb_ref[...].astype(F32)
        kmax_ref[0] = jnp.max(jnp.sum(kf * kf, axis=1, keepdims=True))
    shifts = []
    for h in range(N_HEADS):
        qf = qt_ref[h * HEAD_DIM:(h + 1) * HEAD_DIM, :].astype(F32)
        shifts.append(jnp.sqrt(jnp.sum(qf * qf, axis=0, keepdims=True) * kmax_ref[0]))

    def attn_chunk(c, shift_fn):
        off = pl.multiple_of(c * KEY_CHUNK, KEY_CHUNK)
        sel = keys_ref[pl.ds(off, KEY_CHUNK), :] >= thr
        kb = kb_ref[pl.ds(off, KEY_CHUNK), :]
        vt = vt_ref[:, pl.ds(off, KEY_CHUNK)].astype(BF16)
        for g in range(N_KV_HEADS):
            ps, alphas = [], []
            for j in range(KV_GROUP):
                h = g * KV_GROUP + j
                s = jnp.where(sel, _dot(kb, qz_ref[:, h * qw:(h + 1) * qw]), NEG)
                shift, alpha = shift_fn(h, s)
                alphas.append(alpha)
                ps.append(jnp.exp2(s - shift).astype(BF16))
            vg = jnp.concatenate([vt[g * HEAD_DIM:(g + 1) * HEAD_DIM, :], ones_rows], axis=0)
            pv = _dot(vg, jnp.concatenate(ps, axis=1))
            if alphas[0] is None:
                acc_ref[g] = acc_ref[g] + pv
            else:
                acc_ref[g] = acc_ref[g] * jnp.concatenate(alphas, axis=1) + pv

    acc_ref[...] = jnp.zeros(acc_ref.shape, F32)

    def fast_body(c, carry):
        attn_chunk(c, lambda h, s: (shifts[h], None))
        return carry
    lax.fori_loop(0, nch, fast_body, 0)

    denoms = [acc_ref[g][HEAD_DIM:HEAD_DIM + 1, :] for g in range(N_KV_HEADS)]
    denom_min = jnp.min(functools.reduce(jnp.minimum, denoms))

    @pl.when(jnp.logical_not(denom_min > DENOM_FLOOR))
    def _():
        acc_ref[...] = jnp.zeros(acc_ref.shape, F32)

        def online_body(c, ms):
            ms = list(ms)

            def running_max(h, s):
                m_new = jnp.maximum(ms[h], jnp.max(s, axis=0, keepdims=True))
                alpha = jnp.exp2(ms[h] - m_new)
                ms[h] = m_new
                return m_new, alpha
            attn_chunk(c, running_max)
            return tuple(ms)
        lax.fori_loop(0, nch, online_body, (jnp.full((1, qw), NEG, F32),) * N_HEADS)

    for p in range(N_HEADS // 2):
        halves = []
        for h in (2 * p, 2 * p + 1):
            g, j = h // KV_GROUP, h % KV_GROUP
            acc = acc_ref[g][:, j * qw:(j + 1) * qw]
            halves.append(acc[:HEAD_DIM, :] / acc[HEAD_DIM:HEAD_DIM + 1, :])
        o_ref[:, p * LANES:(p + 1) * LANES] = jnp.concatenate(halves, axis=0).T.astype(BF16)


def _prompt_attn_call(qit, wit, qt, kib, kb, vt, nb, seq, topk, qw):
    nq = seq // qw
    hd, idd = N_HEADS * HEAD_DIM, IDX_HEADS * IDX_DIM
    col = lambda b, i: (0, b * nq + i)
    kern = functools.partial(_prompt_attn_kernel, topk=topk)
    return pl.pallas_call(
        kern,
        grid=(nb, nq),
        in_specs=[pl.BlockSpec((idd, qw), col),
                  pl.BlockSpec((WI_ROWS, qw), col),
                  pl.BlockSpec((hd, qw), col),
                  pl.BlockSpec((seq, LANES), lambda b, i: (b, 0)),
                  pl.BlockSpec((seq, LANES), lambda b, i: (b, 0)),
                  pl.BlockSpec((None, LANES, seq), lambda b, i: (b, 0, 0))],
        out_specs=pl.BlockSpec((qw, hd), lambda b, i: (b * nq + i, 0)),
        out_shape=jax.ShapeDtypeStruct((nb * seq, hd), BF16),
        scratch_shapes=[pltpu.VMEM((seq, qw), I32),
                        pltpu.VMEM((seq, qw), I16),
                        pltpu.VMEM((seq, qw), I16),
                        pltpu.VMEM((LANES, IDX_HEADS * qw), BF16),
                        pltpu.VMEM((LANES, N_HEADS * qw), BF16),
                        pltpu.VMEM((N_KV_HEADS, HEAD_DIM + PACK, KV_GROUP * qw), F32),
                        pltpu.SMEM((1,), F32)],
        compiler_params=pltpu.CompilerParams(dimension_semantics=("arbitrary", "arbitrary"),
                                             vmem_limit_bytes=VMEM_LIMIT),
        name="prompt_attn",
    )(qit, wit, qt, kib, kb, vt)


def _mlp_kernel(h_ref, a_ref, b_ref, pe_ref, wo_ref, g1_ref, b1_ref, wf1_ref, wf2_ref, wg_ref, wp_ref,
                g2_ref, b2_ref, out_ref, *, alpha, ff_chunk):
    mix = _dot(a_ref[...], wo_ref[0:A_WIDTH, :]) + _dot(b_ref[...], wo_ref[A_WIDTH:2 * A_WIDTH, :])
    h1 = _layer_norm(alpha * h_ref[...] + mix, g1_ref[...], b1_ref[...])
    h1b = h1.astype(BF16)
    ff = None
    for c in range(D_FF // ff_chunk):
        hid = jnp.maximum(_dot(h1b, wf1_ref[:, c * ff_chunk:(c + 1) * ff_chunk]), 0.0)
        part = _dot((hid * hid).astype(BF16), wf2_ref[c * ff_chunk:(c + 1) * ff_chunk, :])
        ff = part if ff is None else ff + part
    ple = jax.nn.sigmoid(_dot(h1b, wg_ref[...])) * _dot(pe_ref[...].astype(BF16), wp_ref[...])
    out_ref[...] = _layer_norm(alpha * h1 + ff + ple, g2_ref[...], b2_ref[...])


def _mlp_call(h2d, a_out, b_out, pe, layer, consts, alpha, row_tile):
    n = h2d.shape[0]
    row = lambda i: (i, 0)
    kern = functools.partial(_mlp_kernel, alpha=alpha, ff_chunk=1024)
    return pl.pallas_call(
        kern,
        grid=(n // row_tile,),
        in_specs=[pl.BlockSpec((row_tile, D_MODEL), row),
                  pl.BlockSpec((row_tile, A_WIDTH), row),
                  pl.BlockSpec((row_tile, A_WIDTH), row),
                  pl.BlockSpec((None, row_tile, D_PLE), lambda i: (layer, i, 0))]
                 + [_layer_spec(c, layer) for c in consts],
        out_specs=pl.BlockSpec((row_tile, D_MODEL), row),
        out_shape=jax.ShapeDtypeStruct((n, D_MODEL), F32),
        compiler_params=pltpu.CompilerParams(dimension_semantics=("arbitrary",),
                                             vmem_limit_bytes=VMEM_LIMIT),
        name="mlp",
    )(h2d, a_out, b_out, pe, *consts)


def _head_masked_queries(qi_ref):
    lane_head = lax.broadcasted_iota(I32, (CHUNK, LANES), 1) >> 5
    heads_per_col = LANES // IDX_DIM
    out = []
    for h in range(IDX_HEADS):
        col = qi_ref[:, (h // heads_per_col) * LANES:(h // heads_per_col + 1) * LANES]
        out.append(jnp.where(lane_head == (h % heads_per_col), col, jnp.zeros_like(col)))
    return out


def _index_scores(qh, wsc, kc):
    acc = None
    for h in range(IDX_HEADS):
        s = _dot_nt(qh[h], kc)
        t = jnp.maximum(s, 0.0) * wsc[:, h:h + 1]
        acc = t if acc is None else acc + t
    return acc


def _count_ge(keys_ref, nch, cw, cand, strict=False):
    def body(c, acc):
        off = pl.multiple_of(c * cw, cw)
        for j in range(cw // LANES):
            kk = keys_ref[:, pl.ds(off + j * LANES, LANES)]
            hit = (kk > cand) if strict else (kk >= cand)
            acc = acc + jnp.where(hit, 1.0, 0.0)
        return acc
    acc = lax.fori_loop(0, nch, body, jnp.zeros((CHUNK, LANES), F32))
    cnt = jnp.sum(acc, axis=-1, keepdims=True)
    return jnp.broadcast_to(cnt, (CHUNK, LANES))


def _select_threshold(keys_ref, nch, cw, topk):
    def bit_body(i, prefix):
        cand = prefix + lax.shift_left(jnp.int32(1), 31 - i)
        cnt = _count_ge(keys_ref, nch, cw, cand)
        return jnp.where(cnt >= topk, cand, prefix)
    prefix = lax.fori_loop(0, 32, bit_body, jnp.full((CHUNK, LANES), INT_MIN, I32))
    thr = jnp.maximum(prefix, INT_MIN + 1)
    cnt_ge = _count_ge(keys_ref, nch, cw, thr)

    @pl.when(jnp.max(cnt_ge) > topk)
    def _():
        room = topk - _count_ge(keys_ref, nch, cw, thr, strict=True)
        r_i = lax.broadcasted_iota(I32, (LANES, LANES), 0)
        c_i = lax.broadcasted_iota(I32, (LANES, LANES), 1)
        upper = jnp.where(r_i <= c_i, 1.0, 0.0).astype(BF16)

        def body(c, seen):
            off = pl.multiple_of(c * cw, cw)
            for j in range(cw // LANES):
                kk = keys_ref[:, pl.ds(off + j * LANES, LANES)]
                tie = kk == thr
                rank = seen + _dot(jnp.where(tie, 1.0, 0.0).astype(BF16), upper)
                keys_ref[:, pl.ds(off + j * LANES, LANES)] = jnp.where(tie & (rank > room), thr - 1, kk)
                seen = jnp.broadcast_to(rank[:, LANES - 1:LANES], (CHUNK, LANES))
            return seen
        lax.fori_loop(0, nch, body, jnp.zeros((CHUNK, LANES), F32))

    return thr


PAGE_GROUP = 8


def _request_pages(pt_ref, layer, npg, streams):
    b, nreq = pl.program_id(0), pl.num_programs(0)

    def page_copy(stream, req, slot, j):
        cache, buf, sem = stream
        return pltpu.make_async_copy(cache.at[layer, pt_ref[req, j]], buf.at[slot, j], sem.at[slot])

    def start_all(req, slot):
        def body(j, carry):
            for stream in streams:
                page_copy(stream, req, slot, j).start()
            return carry
        lax.fori_loop(0, npg, body, 0)

    @pl.when(b == 0)
    def _():
        start_all(0, 0)

    @pl.when(b + 1 < nreq)
    def _():
        start_all(b + 1, (b + 1) % 2)

    slot = b % 2

    def wait_body(j, carry):
        for stream in streams:
            page_copy(stream, b, slot, j).wait()
        return carry
    lax.fori_loop(0, npg, wait_body, 0)
    return slot


def _sample_idx_kernel(pt_ref, qs_ref, w_ref, cki_ref, out_ref, buf, sem, *, layer):
    npg = buf.shape[1]
    slot = _request_pages(pt_ref, layer, npg, [(cki_ref, buf, sem)])
    qs = qs_ref[0]
    w = jnp.concatenate([w_ref[0]] * PAGE_GROUP, axis=1)
    nt = out_ref.shape[1]
    width = PAGE_GROUP * LANES
    for c in range(npg // PAGE_GROUP):
        kp = jnp.concatenate([buf[slot, c * PAGE_GROUP + i] for i in range(PAGE_GROUP)], axis=1)
        t = jnp.maximum(_dot(qs, kp.astype(BF16)), 0.0) * w
        for q in range(nt):
            sc = jnp.sum(t[q * IDX_HEADS:(q + 1) * IDX_HEADS, :], axis=0, keepdims=True)
            out_ref[0, q:q + 1, c * width:(c + 1) * width] = _sortable(sc)


def _sample_idx_call(page_table, qs, wrep, cache_kidx, layer):
    db, npg = page_table.shape
    nt = qs.shape[1] // IDX_HEADS
    page = cache_kidx.shape[3]
    kern = functools.partial(_sample_idx_kernel, layer=layer)
    grid_spec = pltpu.PrefetchScalarGridSpec(
        num_scalar_prefetch=1,
        grid=(db,),
        in_specs=[pl.BlockSpec((1,) + qs.shape[1:], lambda b, pt: (b, 0, 0)),
                  pl.BlockSpec((1,) + wrep.shape[1:], lambda b, pt: (b, 0, 0)),
                  pl.BlockSpec(memory_space=pl.ANY)],
        out_specs=pl.BlockSpec((1, nt, npg * page), lambda b, pt: (b, 0, 0)),
        scratch_shapes=[pltpu.VMEM((2, npg, IDX_DIM, page), F32), pltpu.SemaphoreType.DMA((2,))],
    )
    return pl.pallas_call(
        kern, grid_spec=grid_spec,
        out_shape=jax.ShapeDtypeStruct((db, nt, npg * page), I32),
        compiler_params=pltpu.CompilerParams(dimension_semantics=("arbitrary",), vmem_limit_bytes=VMEM_LIMIT),
        name="sample_idx",
    )(page_table, qs, wrep, cache_kidx)


def _sample_select_kernel(kp_ref, qi_ref, wi_ref, ki_ref, bp_ref, bn_ref, keys_ref, *, topk, cw, nt):
    past = kp_ref.shape[1]
    width = keys_ref.shape[1]
    nch = width // cw
    keys_ref[:, 0:past] = kp_ref[...]
    qh = _head_masked_queries(qi_ref)
    wsc = wi_ref[...] * IDX_SCALE
    key = _sortable(_index_scores(qh, wsc, ki_ref[...].astype(BF16)))
    r_i = lax.broadcasted_iota(I32, (CHUNK, LANES), 0)
    c_i = lax.broadcasted_iota(I32, (CHUNK, LANES), 1)
    visible = ((r_i // nt) == (c_i // nt)) & ((c_i % nt) <= (r_i % nt))
    keys_ref[:, past:past + LANES] = jnp.where(visible, key, INT_MIN)
    if width > past + LANES:
        keys_ref[:, past + LANES:width] = jnp.full((CHUNK, width - past - LANES), INT_MIN, I32)

    thr = _select_threshold(keys_ref, nch, cw, topk)

    for c in range(past // LANES):
        bp_ref[:, c * LANES:(c + 1) * LANES] = jnp.where(
            keys_ref[:, c * LANES:(c + 1) * LANES] >= thr, 0.0, NEG)
    bn_ref[...] = jnp.where(keys_ref[:, past:past + LANES] >= thr, 0.0, NEG)


def _sample_select_call(keys_past, qi, wi, ki4, topk, nt, cw):
    rows, past = keys_past.shape
    width = -(-(past + LANES) // cw) * cw
    kern = functools.partial(_sample_select_kernel, topk=topk, cw=cw, nt=nt)
    return pl.pallas_call(
        kern,
        out_shape=[jax.ShapeDtypeStruct((rows, past), F32), jax.ShapeDtypeStruct((rows, LANES), F32)],
        scratch_shapes=[pltpu.VMEM((rows, width), I32)],
        compiler_params=pltpu.CompilerParams(vmem_limit_bytes=VMEM_LIMIT),
        name="sample_select",
    )(keys_past, qi, wi, ki4)


def _sample_attn_kernel(pt_ref, qz_ref, bp_ref, bn_ref, kn_ref, vn_ref, ck_ref, cv_ref, o_ref,
                        kbuf, vbuf, ksem, vsem, s_ref, *, layer, nt):
    npg = kbuf.shape[1]
    slot = _request_pages(pt_ref, layer, npg, [(ck_ref, kbuf, ksem), (cv_ref, vbuf, vsem)])
    qz = qz_ref[0]
    rows = qz.shape[0]
    hpq = rows // nt
    width = PAGE_GROUP * LANES
    groups = npg // PAGE_GROUP
    past = npg * LANES

    def lane_blocks(x):
        return [x[:, i * LANES:(i + 1) * LANES] for i in range(x.shape[1] // LANES)]

    def masked(s, bias):
        return jnp.concatenate(
            [s[q * hpq:(q + 1) * hpq, :] + bias[q:q + 1, :] for q in range(nt)], axis=0)

    def pages_of(buf, c):
        return jnp.concatenate([buf[slot, c * PAGE_GROUP + i] for i in range(PAGE_GROUP)], axis=1).astype(BF16)

    bias_past = bp_ref[0]
    m_run = jnp.full((rows, LANES), NEG, F32)
    for c in range(groups):
        s = masked(_dot(qz, pages_of(kbuf, c)), bias_past[:, c * width:(c + 1) * width])
        s_ref[:, c * width:(c + 1) * width] = s
        m_run = functools.reduce(jnp.maximum, lane_blocks(s), m_run)
    s_new = masked(_dot_nt(qz, kn_ref[...].astype(BF16)), bn_ref[0])
    s_ref[:, past:past + LANES] = s_new
    m = jnp.max(jnp.maximum(m_run, s_new), axis=-1, keepdims=True)

    pr = jnp.exp(s_new - m)
    l_run = pr
    acc = _dot(pr.astype(BF16), vn_ref[...].astype(BF16))
    for c in range(groups):
        pr = jnp.exp(s_ref[:, c * width:(c + 1) * width] - m)
        l_run = functools.reduce(jnp.add, lane_blocks(pr), l_run)
        acc = acc + _dot_nt(pr.astype(BF16), pages_of(vbuf, c))
    o_ref[0] = acc / jnp.sum(l_run, axis=-1, keepdims=True)


def _sample_attn_call(page_table, qz, bias_past, bias_new, k_new, v_new, cache_k, cache_v, layer, nt):
    db, npg = page_table.shape
    page = cache_k.shape[3]
    rows = qz.shape[1]
    kern = functools.partial(_sample_attn_kernel, layer=layer, nt=nt)
    grid_spec = pltpu.PrefetchScalarGridSpec(
        num_scalar_prefetch=1,
        grid=(db,),
        in_specs=[pl.BlockSpec((1, rows, LANES), lambda b, pt: (b, 0, 0)),
                  pl.BlockSpec((1, nt, npg * page), lambda b, pt: (b, 0, 0)),
                  pl.BlockSpec((1, nt, LANES), lambda b, pt: (b, 0, 0)),
                  pl.BlockSpec(k_new.shape, lambda b, pt: (0, 0)),
                  pl.BlockSpec(v_new.shape, lambda b, pt: (0, 0)),
                  pl.BlockSpec(memory_space=pl.ANY),
                  pl.BlockSpec(memory_space=pl.ANY)],
        out_specs=pl.BlockSpec((1, rows, LANES), lambda b, pt: (b, 0, 0)),
        scratch_shapes=[pltpu.VMEM((2, npg, LANES, page), F32), pltpu.VMEM((2, npg, LANES, page), F32),
                        pltpu.SemaphoreType.DMA((2,)), pltpu.SemaphoreType.DMA((2,)),
                        pltpu.VMEM((rows, npg * page + LANES), F32)],
    )
    return pl.pallas_call(
        kern, grid_spec=grid_spec,
        out_shape=jax.ShapeDtypeStruct((db, rows, LANES), F32),
        compiler_params=pltpu.CompilerParams(dimension_semantics=("arbitrary",), vmem_limit_bytes=VMEM_LIMIT),
        name="sample_attn",
    )(page_table, qz, bias_past, bias_new, k_new, v_new, cache_k, cache_v)


def _swap_halves(w, head_dim):
    lead = w.shape[:-1]
    nh = w.shape[-1] // head_dim
    return w.reshape(lead + (nh, 2, head_dim // 2))[..., ::-1, :].reshape(lead + (nh * head_dim,))


def _projection_weights(w_in):
    sizes = (A_WIDTH, A_WIDTH, N_HEADS * HEAD_DIM, N_KV_HEADS * HEAD_DIM, N_KV_HEADS * HEAD_DIM,
             IDX_HEADS * IDX_DIM, IDX_DIM, IDX_HEADS)
    pts = [sum(sizes[:i + 1]) for i in range(len(sizes) - 1)]
    au, av, q, k, v, qi, ki, wi = jnp.split(w_in.astype(BF16), pts, axis=-1)
    rep = LANES // IDX_DIM
    q_s, k_s, qi_s, ki_s = (_swap_halves(q, HEAD_DIM), _swap_halves(k, HEAD_DIM),
                            _swap_halves(qi, IDX_DIM), _swap_halves(ki, IDX_DIM))
    ki4, ki4_s = jnp.tile(ki, (1, 1, rep)), jnp.tile(ki_s, (1, 1, rep))
    pad_cols = lambda w, n: jnp.pad(w, ((0, 0), (0, 0), (0, n - w.shape[-1])))
    ext = jnp.concatenate([au, av, q, k, v, qi, ki4, pad_cols(wi, LANES), q_s, k_s, qi_s, ki4_s], axis=-1)
    wn = jnp.concatenate([au, av, k, k_s, ki4, ki4_s], axis=-1)
    wt = jnp.swapaxes(jnp.concatenate([q, q_s, k, k_s, v, qi, qi_s, ki, ki_s, pad_cols(wi, WI_ROWS)], axis=-1),
                      1, 2)
    assert ext.shape[-1] == C_END and wn.shape[-1] == N_END and wt.shape[1] == T_END
    return ext, wn, wt


def _rope_tables(pos, head_dim):
    half = head_dim // 2
    inv = ROPE_THETA ** (-jnp.arange(half, dtype=F32) / half)
    ang = pos.astype(F32)[:, None] * inv[None, :]
    cos, sin = jnp.cos(ang), jnp.sin(ang)
    reps = LANES // head_dim
    return (jnp.tile(jnp.concatenate([cos, cos], axis=-1), (1, reps)),
            jnp.tile(jnp.concatenate([-sin, sin], axis=-1), (1, reps)))


def _pick(n, prefs):
    for p in prefs:
        if n % p == 0:
            return p
    return n


def kernel(x_prompt, x_sample, p_prompt, p_sample, cache_k, cache_v, cache_kidx, page_table, w_in, sgu_ln_g,
           sgu_ln_b, sgu_w, sgu_b, w_o, ln1_g, ln1_b, w_ff1, w_ff2, w_ple_gate, w_ple_proj, ln2_g, ln2_b):
    nb, seq, _ = x_prompt.shape
    db, nt, _ = x_sample.shape
    depth = w_in.shape[0]
    npg = page_table.shape[1]
    page = cache_k.shape[2]
    past = npg * page
    alpha = (2.0 * depth) ** 0.25
    topk_p = min(TOPK_MAX, seq // 4)
    topk_s = min(TOPK_MAX, (past + nt) // 4)
    n_p, n_s = nb * seq, db * nt
    assert n_s == CHUNK and seq % COUNT_CHUNK == 0 and page == LANES

    w_ext, w_n, w_t = _projection_weights(w_in)
    wo_b, wf1_b, wf2_b = w_o.astype(BF16), w_ff1.astype(BF16), w_ff2.astype(BF16)
    wg_b, wp_b = w_ple_gate.astype(BF16), w_ple_proj.astype(BF16)
    tril = jnp.tril(jnp.ones((CHUNK, CHUNK), F32))
    mm_p = (sgu_w * tril).astype(BF16)
    mb_p = jnp.repeat(jnp.swapaxes(sgu_b, 1, 2), A_GW, axis=2)
    r_i = jnp.arange(CHUNK)
    same_req = (r_i[:, None] // nt == r_i[None, :] // nt) & (r_i[None, :] % nt <= r_i[:, None] % nt)
    mm_s = jnp.where(same_req, sgu_w[:, :, r_i % nt][:, :, :, r_i % nt], 0.0).astype(BF16)
    mb_s = jnp.tile(jnp.repeat(jnp.swapaxes(sgu_b[:, :, :nt], 1, 2), A_GW, axis=2), (1, CHUNK // nt, 1))
    lng = sgu_ln_g.reshape(depth, 1, A_WIDTH)
    lnb = sgu_ln_b.reshape(depth, 1, A_WIDTH)
    mlp_w = (wo_b, ln1_g.reshape(depth, 1, D_MODEL), ln1_b.reshape(depth, 1, D_MODEL), wf1_b, wf2_b, wg_b, wp_b,
             ln2_g.reshape(depth, 1, D_MODEL), ln2_b.reshape(depth, 1, D_MODEL))
    pe_p = p_prompt.reshape(depth, n_p, D_PLE)
    pe_s = p_sample.reshape(depth, n_s, D_PLE)

    pos_p = jnp.arange(seq, dtype=I32)
    pos_s = past + (jnp.arange(n_s, dtype=I32) % nt)
    tabs_p = _rope_tables(pos_p, HEAD_DIM) + _rope_tables(pos_p, IDX_DIM)
    tabs_pt = tuple(t.T for t in tabs_p)
    tabs_s = _rope_tables(pos_s, HEAD_DIM) + _rope_tables(pos_s, IDX_DIM)

    cki = jnp.swapaxes(cache_kidx, 2, 3)
    ck = jnp.transpose(cache_k, (0, 1, 3, 4, 2)).reshape(cache_k.shape[:2] + (LANES, page))
    cv = jnp.transpose(cache_v, (0, 1, 3, 4, 2)).reshape(cache_v.shape[:2] + (LANES, page))

    rt_proj = _pick(seq, (512, 256, 128))
    rt_mlp = _pick(n_p, (512, 256, 128))
    qw_p = _pick(seq, (256, 128))
    cw_s = 256
    assert npg % PAGE_GROUP == 0
    head_lo = (jnp.arange(N_HEADS) // KV_GROUP) * HEAD_DIM

    hp = x_prompt.reshape(n_p, D_MODEL)
    hs = x_sample.reshape(n_s, D_MODEL)
    outs = [[] for _ in range(8)]
    for i in range(depth):
        a_out, vn, qt, kb, kt, vt, qit, kib, kit, wit = _proj_prompt_call(
            hp, i, w_n, w_t, mm_p, mb_p, lng, lnb, tabs_p, tabs_pt, nb, seq, rt_proj)
        b_out = _prompt_attn_call(qit, wit, qt, kib, kb, vt, nb, seq, topk_p, qw_p)
        hp = _mlp_call(hp, a_out, b_out, pe_p, i, mlp_w, alpha, rt_mlp)
        to_tokens = lambda a: jnp.transpose(a.reshape(nb, N_KV_HEADS, HEAD_DIM, seq), (0, 3, 1, 2))
        outs[0].append(to_tokens(kt))
        outs[1].append(to_tokens(vt))
        outs[2].append(jnp.swapaxes(kit, 1, 2))
        outs[3].append(vn)

        a_out, vn, q, k, v, qi, ki4, wi = _proj_call(hs, i, w_ext, mm_s, mb_s, lng, lnb, tabs_s)
        qs = qi.reshape(db, nt * IDX_HEADS, IDX_DIM)
        wrep = jnp.broadcast_to((wi[:, :IDX_HEADS] * IDX_SCALE).reshape(db, nt * IDX_HEADS, 1),
                                (db, nt * IDX_HEADS, LANES))
        keys_past = _sample_idx_call(page_table, qs, wrep, cki, i)
        bias_past, bias_new = _sample_select_call(keys_past.reshape(n_s, past), qi, wi, ki4, topk_s, nt, cw_s)
        q4 = q.reshape(db, nt, N_HEADS, 1, HEAD_DIM)
        half = (jnp.arange(N_KV_HEADS) * HEAD_DIM)[None, None, None, :, None]
        qz = jnp.where(half == head_lo[None, None, :, None, None], q4, jnp.zeros_like(q4))
        qz = qz.reshape(db, nt * N_HEADS, LANES)
        o = _sample_attn_call(page_table, qz, bias_past.reshape(db, nt, past), bias_new.reshape(db, nt, LANES),
                              k, v, ck, cv, i, nt)
        o = o.reshape(db, nt, N_HEADS, N_KV_HEADS, HEAD_DIM)
        b_out = jnp.concatenate([o[:, :, g * KV_GROUP:(g + 1) * KV_GROUP, g, :] for g in range(N_KV_HEADS)], axis=2)
        b_out = b_out.reshape(n_s, N_HEADS * HEAD_DIM).astype(BF16)
        hs = _mlp_call(hs, a_out, b_out, pe_s, i, mlp_w, alpha, n_s)
        outs[4].append(k.reshape(db, nt, N_KV_HEADS, HEAD_DIM))
        outs[5].append(v.reshape(db, nt, N_KV_HEADS, HEAD_DIM))
        outs[6].append(ki4[:, :IDX_DIM].reshape(db, nt, IDX_DIM))
        outs[7].append(vn.reshape(db, nt, A_WIDTH))

    st = [jnp.stack(o) for o in outs]
    return (hp.reshape(nb, seq, D_MODEL), hs.reshape(db, nt, D_MODEL),
            st[0], st[1], st[2], st[3], st[4], st[5], st[6], st[7])
```

```python
import functools

import jax
import jax.numpy as jnp
from jax import lax
from jax.experimental import pallas as pl
from jax.experimental.pallas import tpu as pltpu

F32 = jnp.float32
BF16 = jnp.bfloat16
I32 = jnp.int32
I16 = jnp.int16

D_MODEL = 1024
CHUNK = 128
A_WIDTH = 512
A_GROUPS = 4
A_GW = A_WIDTH // A_GROUPS
HEAD_DIM = 64
N_HEADS = 8
N_KV_HEADS = 2
KV_GROUP = N_HEADS // N_KV_HEADS
IDX_HEADS = 8
IDX_DIM = 32
TOPK_MAX = 256
ROPE_THETA = 10000.0
D_FF = 4 * D_MODEL
D_PLE = 256
LN_EPS = 1e-5

LANES = 128
PACK = 16
VMEM_LIMIT = 56 * 1024 * 1024

INT_MIN = -(2 ** 31)
I16_MIN = -(2 ** 15)
NEG = -1e30
DENOM_FLOOR = 2.0 ** -100

C_AU, C_AV, C_Q, C_K, C_V, C_QI, C_KI, C_WI = 0, 512, 1024, 1536, 1664, 1792, 2048, 2176
C_QS, C_KS, C_QIS, C_KIS, C_END = 2304, 2816, 2944, 3200, 3328
N_AU, N_AV, N_END = 0, 512, 1024
T_Q, T_K, T_V, T_QI, T_KI, T_WI, T_END = 0, 512, 640, 768, 1024, 1056, 1072
WI_ROWS = T_END - T_WI
Q_SCALE_LOG2 = (HEAD_DIM ** -0.5) * 1.4426950408889634
IDX_SCALE = (IDX_DIM ** -0.5) * (IDX_HEADS ** -0.5)

KEY_CHUNK = 512
COUNT_CHUNK = 512

NT_DIMS = (((1,), (1,)), ((), ()))


def _dot(a, b):
    return jnp.dot(a, b, preferred_element_type=F32)


def _dot_nt(a, b):
    return lax.dot_general(a, b, NT_DIMS, preferred_element_type=F32)


def _layer_norm(x, g, b):
    mu = jnp.mean(x, axis=-1, keepdims=True)
    xc = x - mu
    var = jnp.mean(xc * xc, axis=-1, keepdims=True)
    return xc * lax.rsqrt(var + LN_EPS) * g + b


def _gelu(x):
    return 0.5 * x * (1.0 + lax.erf(x * (2.0 ** -0.5)))


def _sortable(x):
    bits = pltpu.bitcast(x, I32)
    return bits ^ ((bits >> 31) & 0x7FFFFFFF)


def _layer_spec(stacked, layer):
    shape = stacked.shape[1:]
    return pl.BlockSpec((None,) + shape, lambda *_: (layer,) + (0,) * len(shape),
                        pipeline_mode=pl.Buffered(1))


def _gated_mixer(au, av, g, lng_ref, lnb_ref, mm_ref, mb_ref, vn_ref, aout_ref):
    rows = au.shape[0]
    lo = g * A_GW
    vn = _layer_norm(av, lng_ref[:, lo:lo + A_GW], lnb_ref[:, lo:lo + A_GW])
    vn_ref[:, lo:lo + A_GW] = vn[rows - vn_ref.shape[0]:, :]
    vnb = vn.astype(BF16)
    for c in range(rows // CHUNK):
        r0 = c * CHUNK
        gate = _dot(mm_ref[g], vnb[r0:r0 + CHUNK, :]) + mb_ref[:, lo:lo + A_GW]
        aout_ref[r0:r0 + CHUNK, lo:lo + A_GW] = (au[r0:r0 + CHUNK, :] * gate).astype(BF16)


def _proj_kernel(h_ref, w_ref, mm_ref, mb_ref, lng_ref, lnb_ref, cq_ref, sq_ref, ci_ref, si_ref,
                 aout_ref, vn_ref, q_ref, k_ref, v_ref, qi_ref, ki_ref, wi_ref):
    hb = h_ref[...].astype(BF16)

    def proj(lo, width=LANES):
        return _dot(hb, w_ref[:, lo:lo + width])

    for g in range(A_GROUPS):
        lo = g * A_GW
        _gated_mixer(_gelu(proj(C_AU + lo)), _gelu(proj(C_AV + lo)), g,
                     lng_ref, lnb_ref, mm_ref, mb_ref, vn_ref, aout_ref)

    cq, sq = cq_ref[...], sq_ref[...]
    ci, si = ci_ref[...], si_ref[...]
    for j in range(N_HEADS * HEAD_DIM // LANES):
        z = proj(C_Q + j * LANES) * cq + proj(C_QS + j * LANES) * sq
        q_ref[:, j * LANES:(j + 1) * LANES] = (z * (HEAD_DIM ** -0.5)).astype(BF16)
    k_ref[...] = proj(C_K) * cq + proj(C_KS) * sq
    v_ref[...] = proj(C_V)
    for j in range(IDX_HEADS * IDX_DIM // LANES):
        z = proj(C_QI + j * LANES) * ci + proj(C_QIS + j * LANES) * si
        qi_ref[:, j * LANES:(j + 1) * LANES] = z.astype(BF16)
    ki_ref[...] = proj(C_KI) * ci + proj(C_KIS) * si
    wi_ref[...] = proj(C_WI)


def _proj_call(h2d, layer, w_ext, mm, mb, lng, lnb, tabs):
    n = h2d.shape[0]
    whole = lambda a: pl.BlockSpec(a.shape, lambda i: (0, 0))
    out_widths = [(A_WIDTH, BF16), (A_WIDTH, F32), (N_HEADS * HEAD_DIM, BF16), (LANES, F32), (LANES, F32),
                  (IDX_HEADS * IDX_DIM, BF16), (LANES, F32), (LANES, F32)]
    return pl.pallas_call(
        _proj_kernel,
        grid=(1,),
        in_specs=[whole(h2d)] + [_layer_spec(a, layer) for a in (w_ext, mm, mb, lng, lnb)]
                 + [whole(t) for t in tabs],
        out_specs=[pl.BlockSpec((n, w), lambda i: (0, 0)) for w, _ in out_widths],
        out_shape=[jax.ShapeDtypeStruct((n, w), dt) for w, dt in out_widths],
        compiler_params=pltpu.CompilerParams(dimension_semantics=("arbitrary",), vmem_limit_bytes=VMEM_LIMIT),
        name="proj",
    )(h2d, w_ext, mm, mb, lng, lnb, *tabs)


def _rope_t(x, cos_t, sin_t, head_dim):
    half = head_dim // 2
    parts = []
    for r in range(0, x.shape[0], head_dim):
        parts += [x[r + half:r + head_dim], x[r:r + half]]
    return x * cos_t + jnp.concatenate(parts, axis=0) * sin_t


def _proj_prompt_kernel(h_ref, wn_ref, wt_ref, mm_ref, mb_ref, lng_ref, lnb_ref,
                        cqt_ref, sqt_ref, cit_ref, sit_ref,
                        aout_ref, vn_ref, qt_ref, kb_ref, kt_ref, vt_ref, qit_ref, kib_ref, kit_ref, wit_ref):
    hb = h_ref[...].astype(BF16)
    wide = 2 * LANES

    for gp in range(A_GROUPS // 2):
        au2 = _gelu(_dot(hb, wn_ref[:, N_AU + gp * wide:N_AU + (gp + 1) * wide]))
        av2 = _gelu(_dot(hb, wn_ref[:, N_AV + gp * wide:N_AV + (gp + 1) * wide]))
        for e in range(2):
            _gated_mixer(au2[:, e * A_GW:(e + 1) * A_GW], av2[:, e * A_GW:(e + 1) * A_GW], 2 * gp + e,
                         lng_ref, lnb_ref, mm_ref, mb_ref, vn_ref, aout_ref)

    zt = _dot_nt(wt_ref[...], hb)
    cqt, sqt = cqt_ref[...], sqt_ref[...]
    cit, sit = cit_ref[...], sit_ref[...]
    for j in range(N_HEADS * HEAD_DIM // LANES):
        r = j * LANES
        q = _rope_t(zt[T_Q + r:T_Q + r + LANES], cqt, sqt, HEAD_DIM)
        qt_ref[r:r + LANES, :] = (q * Q_SCALE_LOG2).astype(BF16)
    kt = _rope_t(zt[T_K:T_K + LANES], cqt, sqt, HEAD_DIM)
    kt_ref[...] = kt
    vt_ref[...] = zt[T_V:T_V + LANES]
    for j in range(IDX_HEADS * IDX_DIM // LANES):
        r = j * LANES
        qit_ref[r:r + LANES, :] = _rope_t(zt[T_QI + r:T_QI + r + LANES], cit, sit, IDX_DIM).astype(BF16)
    kit = _rope_t(zt[T_KI:T_KI + IDX_DIM], cit[:IDX_DIM], sit[:IDX_DIM], IDX_DIM)
    kit_ref[...] = kit
    wit_ref[...] = zt[T_WI:T_END]

    kb_ref[...] = kt.T.astype(BF16)
    kib_ref[...] = jnp.concatenate([kit] * (LANES // IDX_DIM), axis=0).T.astype(BF16)


def _proj_prompt_call(h2d, layer, wn, wt, mm, mb, lng, lnb, tabs_t, nb, seq, row_tile):
    n = nb * seq
    ns = seq // row_tile
    row_map = lambda b, s: (b * ns + s, 0)
    col_map = lambda b, s: (0, b * ns + s)
    bcol_map = lambda b, s: (b, 0, s)
    hd, idd = N_HEADS * HEAD_DIM, IDX_HEADS * IDX_DIM
    out_specs = [pl.BlockSpec((row_tile, A_WIDTH), row_map),
                 pl.BlockSpec((None, CHUNK, A_WIDTH), lambda b, s: (b, 0, 0)),
                 pl.BlockSpec((hd, row_tile), col_map),
                 pl.BlockSpec((row_tile, LANES), row_map),
                 pl.BlockSpec((None, LANES, row_tile), bcol_map),
                 pl.BlockSpec((None, LANES, row_tile), bcol_map),
                 pl.BlockSpec((idd, row_tile), col_map),
                 pl.BlockSpec((row_tile, LANES), row_map),
                 pl.BlockSpec((None, IDX_DIM, row_tile), bcol_map),
                 pl.BlockSpec((WI_ROWS, row_tile), col_map)]
    out_shape = [jax.ShapeDtypeStruct((n, A_WIDTH), BF16), jax.ShapeDtypeStruct((nb, CHUNK, A_WIDTH), F32),
                 jax.ShapeDtypeStruct((hd, n), BF16),
                 jax.ShapeDtypeStruct((n, LANES), BF16),
                 jax.ShapeDtypeStruct((nb, LANES, seq), F32),
                 jax.ShapeDtypeStruct((nb, LANES, seq), F32),
                 jax.ShapeDtypeStruct((idd, n), BF16),
                 jax.ShapeDtypeStruct((n, LANES), BF16),
                 jax.ShapeDtypeStruct((nb, IDX_DIM, seq), F32),
                 jax.ShapeDtypeStruct((WI_ROWS, n), F32)]
    return pl.pallas_call(
        _proj_prompt_kernel,
        grid=(nb, ns),
        in_specs=[pl.BlockSpec((row_tile, D_MODEL), row_map)]
                 + [_layer_spec(a, layer) for a in (wn, wt, mm, mb, lng, lnb)]
                 + [pl.BlockSpec((LANES, row_tile), lambda b, s: (0, s))] * 4,
        out_specs=out_specs,
        out_shape=out_shape,
        compiler_params=pltpu.CompilerParams(dimension_semantics=("arbitrary", "arbitrary"),
                                             vmem_limit_bytes=VMEM_LIMIT),
        name="proj_prompt",
    )(h2d, wn, wt, mm, mb, lng, lnb, *tabs_t)


def _count16(ref, nsteps, cand, strict=False):
    one, zero = jnp.ones(cand.shape, I16), jnp.zeros(cand.shape, I16)
    n_acc = 4

    def body(c, accs):
        off = pl.multiple_of(c * COUNT_CHUNK, COUNT_CHUNK)
        blk = ref[pl.ds(off, COUNT_CHUNK), :]
        accs = list(accs)
        for j in range(COUNT_CHUNK // PACK):
            kk = blk[j * PACK:(j + 1) * PACK, :]
            hit = (kk > cand) if strict else (kk >= cand)
            accs[j % n_acc] = accs[j % n_acc] + jnp.where(hit, one, zero)
        return tuple(accs)
    accs = lax.fori_loop(0, nsteps, body, (zero,) * n_acc)
    tot = (accs[0] + accs[1]) + (accs[2] + accs[3])
    return jnp.sum(tot.astype(F32), axis=0, keepdims=True)


def _search16(ref, nsteps, target):
    qw = ref.shape[1]

    def bit_body(i, prefix):
        cand = prefix + lax.shift_left(jnp.int32(1), 15 - i)
        cand16 = jnp.broadcast_to(cand, (PACK, qw)).astype(I16)
        return jnp.where(_count16(ref, nsteps, cand16) >= target, cand, prefix)
    return lax.fori_loop(0, 16, bit_body, jnp.full((1, qw), I16_MIN, I32))


def _prompt_attn_kernel(qit_ref, wit_ref, qt_ref, kib_ref, kb_ref, vt_ref, o_ref,
                        keys_ref, hi_ref, lo_ref, rhs_ref, qz_ref, acc_ref, kmax_ref, *, topk):
    qw = o_ref.shape[0]
    qb = pl.program_id(1)
    nch = (qb * qw + qw + KEY_CHUNK - 1) // KEY_CHUNK
    nsteps = (nch * KEY_CHUNK + COUNT_CHUNK - 1) // COUNT_CHUNK
    qpos = qb * qw + lax.broadcasted_iota(I32, (KEY_CHUNK, qw), 1)
    krow = lax.broadcasted_iota(I32, (KEY_CHUNK, qw), 0)

    rhs_ref[...] = jnp.zeros(rhs_ref.shape, BF16)
    qz_ref[...] = jnp.zeros(qz_ref.shape, BF16)
    slots = LANES // IDX_DIM
    for h in range(IDX_HEADS):
        r = (h % slots) * IDX_DIM
        rhs_ref[r:r + IDX_DIM, h * qw:(h + 1) * qw] = qit_ref[h * IDX_DIM:(h + 1) * IDX_DIM, :]
    for h in range(N_HEADS):
        r = (h // KV_GROUP) * HEAD_DIM
        qz_ref[r:r + HEAD_DIM, h * qw:(h + 1) * qw] = qt_ref[h * HEAD_DIM:(h + 1) * HEAD_DIM, :]
    wsc = wit_ref[...] * IDX_SCALE

    def score_chunk(c, causal_edge):
        off = pl.multiple_of(c * KEY_CHUNK, KEY_CHUNK)
        kc = kib_ref[pl.ds(off, KEY_CHUNK), :]
        acc = None
        for h in range(IDX_HEADS):
            s = _dot(kc, rhs_ref[:, h * qw:(h + 1) * qw])
            t = jnp.maximum(s, 0.0) * wsc[h:h + 1, :]
            acc = t if acc is None else acc + t
        key = _sortable(acc)
        if causal_edge:
            key = jnp.where(krow + off <= qpos, key, INT_MIN)
        keys_ref[pl.ds(off, KEY_CHUNK), :] = key
        hi_ref[pl.ds(off, KEY_CHUNK), :] = (key >> 16).astype(I16)
        lo_ref[pl.ds(off, KEY_CHUNK), :] = (key ^ 0x8000).astype(I16)

    def score_body(c, carry):
        score_chunk(c, False)
        return carry
    lax.fori_loop(0, nch - 1, score_body, 0)
    score_chunk(nch - 1, True)

    @pl.when(nch * KEY_CHUNK < nsteps * COUNT_CHUNK)
    def _():
        off = pl.multiple_of(nch * KEY_CHUNK, KEY_CHUNK)
        fill = jnp.full((KEY_CHUNK, qw), I16_MIN, I16)
        hi_ref[pl.ds(off, KEY_CHUNK), :] = fill
        lo_ref[pl.ds(off, KEY_CHUNK), :] = fill

    p_hi = _search16(hi_ref, nsteps, float(topk))
    p16 = jnp.broadcast_to(p_hi, (PACK, qw)).astype(I16)
    room = topk - _count16(hi_ref, nsteps, p16, strict=True)

    def low_body(c, carry):
        off = pl.multiple_of(c * COUNT_CHUNK, COUNT_CHUNK)
        lowest = jnp.full((PACK, qw), I16_MIN, I16)
        his, los = hi_ref[pl.ds(off, COUNT_CHUNK), :], lo_ref[pl.ds(off, COUNT_CHUNK), :]
        lo_ref[pl.ds(off, COUNT_CHUNK), :] = jnp.concatenate(
            [jnp.where(his[j * PACK:(j + 1) * PACK, :] == p16, los[j * PACK:(j + 1) * PACK, :], lowest)
             for j in range(COUNT_CHUNK // PACK)], axis=0)
        return carry
    lax.fori_loop(0, nsteps, low_body, 0)
    p_lo = _search16(lo_ref, nsteps, room)
    thr = jnp.maximum(p_hi * 65536 + (p_lo + 2 ** 15), INT_MIN + 1)

    def count32(strict):
        def body(c, acc):
            off = pl.multiple_of(c * KEY_CHUNK, KEY_CHUNK)
            kk = keys_ref[pl.ds(off, KEY_CHUNK), :]
            hit = (kk > thr) if strict else (kk >= thr)
            return acc + jnp.sum(jnp.where(hit, 1.0, 0.0), axis=0, keepdims=True)
        return lax.fori_loop(0, nch, body, jnp.zeros((1, qw), F32))

    @pl.when(jnp.max(count32(False)) > topk)
    def _():
        spare = topk - count32(True)
        r_i = lax.broadcasted_iota(I32, (KEY_CHUNK, KEY_CHUNK), 0)
        c_i = lax.broadcasted_iota(I32, (KEY_CHUNK, KEY_CHUNK), 1)
        lower = jnp.where(c_i <= r_i, 1.0, 0.0).astype(BF16)

        def body(c, seen):
            off = pl.multiple_of(c * KEY_CHUNK, KEY_CHUNK)
            kk = keys_ref[pl.ds(off, KEY_CHUNK), :]
            tie = kk == thr
            rank = seen + _dot(lower, jnp.where(tie, 1.0, 0.0).astype(BF16))
            keys_ref[pl.ds(off, KEY_CHUNK), :] = jnp.where(tie & (rank > spare), thr - 1, kk)
            return rank[KEY_CHUNK - 1:KEY_CHUNK, :]
        lax.fori_loop(0, nch, body, jnp.zeros((1, qw), F32))

    ones_rows = jnp.where(lax.broadcasted_iota(I32, (PACK, KEY_CHUNK), 0) == 0, 1.0, 0.0).astype(BF16)

    @pl.when(qb == 0)
    def _():
        kf = kb_ref[...].astype(F32)
        kmax_ref[0] = jnp.max(jnp.sum(kf * kf, axis=1, keepdims=True))
    shifts = []
    for h in range(N_HEADS):
        qf = qt_ref[h * HEAD_DIM:(h + 1) * HEAD_DIM, :].astype(F32)
        shifts.append(jnp.sqrt(jnp.sum(qf * qf, axis=0, keepdims=True) * kmax_ref[0]))

    def attn_chunk(c, shift_fn):
        off = pl.multiple_of(c * KEY_CHUNK, KEY_CHUNK)
        sel = keys_ref[pl.ds(off, KEY_CHUNK), :] >= thr
        kb = kb_ref[pl.ds(off, KEY_CHUNK), :]
        vt = vt_ref[:, pl.ds(off, KEY_CHUNK)].astype(BF16)
        for g in range(N_KV_HEADS):
            ps, alphas = [], []
            for j in range(KV_GROUP):
                h = g * KV_GROUP + j
                s = jnp.where(sel, _dot(kb, qz_ref[:, h * qw:(h + 1) * qw]), NEG)
                shift, alpha = shift_fn(h, s)
                alphas.append(alpha)
                ps.append(jnp.exp2(s - shift).astype(BF16))
            vg = jnp.concatenate([vt[g * HEAD_DIM:(g + 1) * HEAD_DIM, :], ones_rows], axis=0)
            pv = _dot(vg, jnp.concatenate(ps, axis=1))
            if alphas[0] is None:
                acc_ref[g] = acc_ref[g] + pv
            else:
                acc_ref[g] = acc_ref[g] * jnp.concatenate(alphas, axis=1) + pv

    acc_ref[...] = jnp.zeros(acc_ref.shape, F32)

    def fast_body(c, carry):
        attn_chunk(c, lambda h, s: (shifts[h], None))
        return carry
    lax.fori_loop(0, nch, fast_body, 0)

    denoms = [acc_ref[g][HEAD_DIM:HEAD_DIM + 1, :] for g in range(N_KV_HEADS)]
    denom_min = jnp.min(functools.reduce(jnp.minimum, denoms))

    @pl.when(jnp.logical_not(denom_min > DENOM_FLOOR))
    def _():
        acc_ref[...] = jnp.zeros(acc_ref.shape, F32)

        def online_body(c, ms):
            ms = list(ms)

            def running_max(h, s):
                m_new = jnp.maximum(ms[h], jnp.max(s, axis=0, keepdims=True))
                alpha = jnp.exp2(ms[h] - m_new)
                ms[h] = m_new
                return m_new, alpha
            attn_chunk(c, running_max)
            return tuple(ms)
        lax.fori_loop(0, nch, online_body, (jnp.full((1, qw), NEG, F32),) * N_HEADS)

    for p in range(N_HEADS // 2):
        halves = []
        for h in (2 * p, 2 * p + 1):
            g, j = h // KV_GROUP, h % KV_GROUP
            acc = acc_ref[g][:, j * qw:(j + 1) * qw]
            halves.append(acc[:HEAD_DIM, :] / acc[HEAD_DIM:HEAD_DIM + 1, :])
        o_ref[:, p * LANES:(p + 1) * LANES] = jnp.concatenate(halves, axis=0).T.astype(BF16)


def _prompt_attn_call(qit, wit, qt, kib, kb, vt, nb, seq, topk, qw):
    nq = seq // qw
    hd, idd = N_HEADS * HEAD_DIM, IDX_HEADS * IDX_DIM
    col = lambda b, i: (0, b * nq + i)
    kern = functools.partial(_prompt_attn_kernel, topk=topk)
    return pl.pallas_call(
        kern,
        grid=(nb, nq),
        in_specs=[pl.BlockSpec((idd, qw), col),
                  pl.BlockSpec((WI_ROWS, qw), col),
                  pl.BlockSpec((hd, qw), col),
                  pl.BlockSpec((seq, LANES), lambda b, i: (b, 0)),
                  pl.BlockSpec((seq, LANES), lambda b, i: (b, 0)),
                  pl.BlockSpec((None, LANES, seq), lambda b, i: (b, 0, 0))],
        out_specs=pl.BlockSpec((qw, hd), lambda b, i: (b * nq + i, 0)),
        out_shape=jax.ShapeDtypeStruct((nb * seq, hd), BF16),
        scratch_shapes=[pltpu.VMEM((seq, qw), I32),
                        pltpu.VMEM((seq, qw), I16),
                        pltpu.VMEM((seq, qw), I16),
                        pltpu.VMEM((LANES, IDX_HEADS * qw), BF16),
                        pltpu.VMEM((LANES, N_HEADS * qw), BF16),
                        pltpu.VMEM((N_KV_HEADS, HEAD_DIM + PACK, KV_GROUP * qw), F32),
                        pltpu.SMEM((1,), F32)],
        compiler_params=pltpu.CompilerParams(dimension_semantics=("arbitrary", "arbitrary"),
                                             vmem_limit_bytes=VMEM_LIMIT),
        name="prompt_attn",
    )(qit, wit, qt, kib, kb, vt)


def _mlp_kernel(h_ref, a_ref, b_ref, pe_ref, wo_ref, g1_ref, b1_ref, wf1_ref, wf2_ref, wg_ref, wp_ref,
                g2_ref, b2_ref, out_ref, *, alpha, ff_chunk):
    mix = _dot(a_ref[...], wo_ref[0:A_WIDTH, :]) + _dot(b_ref[...], wo_ref[A_WIDTH:2 * A_WIDTH, :])
    h1 = _layer_norm(alpha * h_ref[...] + mix, g1_ref[...], b1_ref[...])
    h1b = h1.astype(BF16)
    ff = None
    for c in range(D_FF // ff_chunk):
        hid = jnp.maximum(_dot(h1b, wf1_ref[:, c * ff_chunk:(c + 1) * ff_chunk]), 0.0)
        part = _dot((hid * hid).astype(BF16), wf2_ref[c * ff_chunk:(c + 1) * ff_chunk, :])
        ff = part if ff is None else ff + part
    ple = jax.nn.sigmoid(_dot(h1b, wg_ref[...])) * _dot(pe_ref[...].astype(BF16), wp_ref[...])
    out_ref[...] = _layer_norm(alpha * h1 + ff + ple, g2_ref[...], b2_ref[...])


def _mlp_call(h2d, a_out, b_out, pe, layer, consts, alpha, row_tile):
    n = h2d.shape[0]
    row = lambda i: (i, 0)
    kern = functools.partial(_mlp_kernel, alpha=alpha, ff_chunk=1024)
    return pl.pallas_call(
        kern,
        grid=(n // row_tile,),
        in_specs=[pl.BlockSpec((row_tile, D_MODEL), row),
                  pl.BlockSpec((row_tile, A_WIDTH), row),
                  pl.BlockSpec((row_tile, A_WIDTH), row),
                  pl.BlockSpec((None, row_tile, D_PLE), lambda i: (layer, i, 0))]
                 + [_layer_spec(c, layer) for c in consts],
        out_specs=pl.BlockSpec((row_tile, D_MODEL), row),
        out_shape=jax.ShapeDtypeStruct((n, D_MODEL), F32),
        compiler_params=pltpu.CompilerParams(dimension_semantics=("arbitrary",),
                                             vmem_limit_bytes=VMEM_LIMIT),
        name="mlp",
    )(h2d, a_out, b_out, pe, *consts)


def _head_masked_queries(qi_ref):
    lane_head = lax.broadcasted_iota(I32, (CHUNK, LANES), 1) >> 5
    heads_per_col = LANES // IDX_DIM
    out = []
    for h in range(IDX_HEADS):
        col = qi_ref[:, (h // heads_per_col) * LANES:(h // heads_per_col + 1) * LANES]
        out.append(jnp.where(lane_head == (h % heads_per_col), col, jnp.zeros_like(col)))
    return out


def _index_scores(qh, wsc, kc):
    acc = None
    for h in range(IDX_HEADS):
        s = _dot_nt(qh[h], kc)
        t = jnp.maximum(s, 0.0) * wsc[:, h:h + 1]
        acc = t if acc is None else acc + t
    return acc


def _count_ge(keys_ref, nch, cw, cand, strict=False):
    def body(c, acc):
        off = pl.multiple_of(c * cw, cw)
        for j in range(cw // LANES):
            kk = keys_ref[:, pl.ds(off + j * LANES, LANES)]
            hit = (kk > cand) if strict else (kk >= cand)
            acc = acc + jnp.where(hit, 1.0, 0.0)
        return acc
    acc = lax.fori_loop(0, nch, body, jnp.zeros((CHUNK, LANES), F32))
    cnt = jnp.sum(acc, axis=-1, keepdims=True)
    return jnp.broadcast_to(cnt, (CHUNK, LANES))


def _select_threshold(keys_ref, nch, cw, topk):
    def bit_body(i, prefix):
        cand = prefix + lax.shift_left(jnp.int32(1), 31 - i)
        cnt = _count_ge(keys_ref, nch, cw, cand)
        return jnp.where(cnt >= topk, cand, prefix)
    prefix = lax.fori_loop(0, 32, bit_body, jnp.full((CHUNK, LANES), INT_MIN, I32))
    thr = jnp.maximum(prefix, INT_MIN + 1)
    cnt_ge = _count_ge(keys_ref, nch, cw, thr)

    @pl.when(jnp.max(cnt_ge) > topk)
    def _():
        room = topk - _count_ge(keys_ref, nch, cw, thr, strict=True)
        r_i = lax.broadcasted_iota(I32, (LANES, LANES), 0)
        c_i = lax.broadcasted_iota(I32, (LANES, LANES), 1)
        upper = jnp.where(r_i <= c_i, 1.0, 0.0).astype(BF16)

        def body(c, seen):
            off = pl.multiple_of(c * cw, cw)
            for j in range(cw // LANES):
                kk = keys_ref[:, pl.ds(off + j * LANES, LANES)]
                tie = kk == thr
                rank = seen + _dot(jnp.where(tie, 1.0, 0.0).astype(BF16), upper)
                keys_ref[:, pl.ds(off + j * LANES, LANES)] = jnp.where(tie & (rank > room), thr - 1, kk)
                seen = jnp.broadcast_to(rank[:, LANES - 1:LANES], (CHUNK, LANES))
            return seen
        lax.fori_loop(0, nch, body, jnp.zeros((CHUNK, LANES), F32))

    return thr


PAGE_GROUP = 8


def _request_pages(pt_ref, layer, npg, streams):
    b, nreq = pl.program_id(0), pl.num_programs(0)

    def page_copy(stream, req, slot, j):
        cache, buf, sem = stream
        return pltpu.make_async_copy(cache.at[layer, pt_ref[req, j]], buf.at[slot, j], sem.at[slot])

    def start_all(req, slot):
        def body(j, carry):
            for stream in streams:
                page_copy(stream, req, slot, j).start()
            return carry
        lax.fori_loop(0, npg, body, 0)

    @pl.when(b == 0)
    def _():
        start_all(0, 0)

    @pl.when(b + 1 < nreq)
    def _():
        start_all(b + 1, (b + 1) % 2)

    slot = b % 2

    def wait_body(j, carry):
        for stream in streams:
            page_copy(stream, b, slot, j).wait()
        return carry
    lax.fori_loop(0, npg, wait_body, 0)
    return slot


def _sample_idx_kernel(pt_ref, qs_ref, w_ref, cki_ref, out_ref, buf, sem, *, layer):
    npg = buf.shape[1]
    slot = _request_pages(pt_ref, layer, npg, [(cki_ref, buf, sem)])
    qs = qs_ref[0]
    w = jnp.concatenate([w_ref[0]] * PAGE_GROUP, axis=1)
    nt = out_ref.shape[1]
    width = PAGE_GROUP * LANES
    for c in range(npg // PAGE_GROUP):
        kp = jnp.concatenate([buf[slot, c * PAGE_GROUP + i] for i in range(PAGE_GROUP)], axis=1)
        t = jnp.maximum(_dot(qs, kp.astype(BF16)), 0.0) * w
        for q in range(nt):
            sc = jnp.sum(t[q * IDX_HEADS:(q + 1) * IDX_HEADS, :], axis=0, keepdims=True)
            out_ref[0, q:q + 1, c * width:(c + 1) * width] = _sortable(sc)


def _sample_idx_call(page_table, qs, wrep, cache_kidx, layer):
    db, npg = page_table.shape
    nt = qs.shape[1] // IDX_HEADS
    page = cache_kidx.shape[3]
    kern = functools.partial(_sample_idx_kernel, layer=layer)
    grid_spec = pltpu.PrefetchScalarGridSpec(
        num_scalar_prefetch=1,
        grid=(db,),
        in_specs=[pl.BlockSpec((1,) + qs.shape[1:], lambda b, pt: (b, 0, 0)),
                  pl.BlockSpec((1,) + wrep.shape[1:], lambda b, pt: (b, 0, 0)),
                  pl.BlockSpec(memory_space=pl.ANY)],
        out_specs=pl.BlockSpec((1, nt, npg * page), lambda b, pt: (b, 0, 0)),
        scratch_shapes=[pltpu.VMEM((2, npg, IDX_DIM, page), F32), pltpu.SemaphoreType.DMA((2,))],
    )
    return pl.pallas_call(
        kern, grid_spec=grid_spec,
        out_shape=jax.ShapeDtypeStruct((db, nt, npg * page), I32),
        compiler_params=pltpu.CompilerParams(dimension_semantics=("arbitrary",), vmem_limit_bytes=VMEM_LIMIT),
        name="sample_idx",
    )(page_table, qs, wrep, cache_kidx)


def _sample_select_kernel(kp_ref, qi_ref, wi_ref, ki_ref, bp_ref, bn_ref, keys_ref, *, topk, cw, nt):
    past = kp_ref.shape[1]
    width = keys_ref.shape[1]
    nch = width // cw
    keys_ref[:, 0:past] = kp_ref[...]
    qh = _head_masked_queries(qi_ref)
    wsc = wi_ref[...] * IDX_SCALE
    key = _sortable(_index_scores(qh, wsc, ki_ref[...].astype(BF16)))
    r_i = lax.broadcasted_iota(I32, (CHUNK, LANES), 0)
    c_i = lax.broadcasted_iota(I32, (CHUNK, LANES), 1)
    visible = ((r_i // nt) == (c_i // nt)) & ((c_i % nt) <= (r_i % nt))
    keys_ref[:, past:past + LANES] = jnp.where(visible, key, INT_MIN)
    if width > past + LANES:
        keys_ref[:, past + LANES:width] = jnp.full((CHUNK, width - past - LANES), INT_MIN, I32)

    thr = _select_threshold(keys_ref, nch, cw, topk)

    for c in range(past // LANES):
        bp_ref[:, c * LANES:(c + 1) * LANES] = jnp.where(
            keys_ref[:, c * LANES:(c + 1) * LANES] >= thr, 0.0, NEG)
    bn_ref[...] = jnp.where(keys_ref[:, past:past + LANES] >= thr, 0.0, NEG)


def _sample_select_call(keys_past, qi, wi, ki4, topk, nt, cw):
    rows, past = keys_past.shape
    width = -(-(past + LANES) // cw) * cw
    kern = functools.partial(_sample_select_kernel, topk=topk, cw=cw, nt=nt)
    return pl.pallas_call(
        kern,
        out_shape=[jax.ShapeDtypeStruct((rows, past), F32), jax.ShapeDtypeStruct((rows, LANES), F32)],
        scratch_shapes=[pltpu.VMEM((rows, width), I32)],
        compiler_params=pltpu.CompilerParams(vmem_limit_bytes=VMEM_LIMIT),
        name="sample_select",
    )(keys_past, qi, wi, ki4)


def _sample_attn_kernel(pt_ref, qz_ref, bp_ref, bn_ref, kn_ref, vn_ref, ck_ref, cv_ref, o_ref,
                        kbuf, vbuf, ksem, vsem, s_ref, *, layer, nt):
    npg = kbuf.shape[1]
    slot = _request_pages(pt_ref, layer, npg, [(ck_ref, kbuf, ksem), (cv_ref, vbuf, vsem)])
    qz = qz_ref[0]
    rows = qz.shape[0]
    hpq = rows // nt
    width = PAGE_GROUP * LANES
    groups = npg // PAGE_GROUP
    past = npg * LANES

    def lane_blocks(x):
        return [x[:, i * LANES:(i + 1) * LANES] for i in range(x.shape[1] // LANES)]

    def masked(s, bias):
        return jnp.concatenate(
            [s[q * hpq:(q + 1) * hpq, :] + bias[q:q + 1, :] for q in range(nt)], axis=0)

    def pages_of(buf, c):
        return jnp.concatenate([buf[slot, c * PAGE_GROUP + i] for i in range(PAGE_GROUP)], axis=1).astype(BF16)

    bias_past = bp_ref[0]
    m_run = jnp.full((rows, LANES), NEG, F32)
    for c in range(groups):
        s = masked(_dot(qz, pages_of(kbuf, c)), bias_past[:, c * width:(c + 1) * width])
        s_ref[:, c * width:(c + 1) * width] = s
        m_run = functools.reduce(jnp.maximum, lane_blocks(s), m_run)
    s_new = masked(_dot_nt(qz, kn_ref[...].astype(BF16)), bn_ref[0])
    s_ref[:, past:past + LANES] = s_new
    m = jnp.max(jnp.maximum(m_run, s_new), axis=-1, keepdims=True)

    pr = jnp.exp(s_new - m)
    l_run = pr
    acc = _dot(pr.astype(BF16), vn_ref[...].astype(BF16))
    for c in range(groups):
        pr = jnp.exp(s_ref[:, c * width:(c + 1) * width] - m)
        l_run = functools.reduce(jnp.add, lane_blocks(pr), l_run)
        acc = acc + _dot_nt(pr.astype(BF16), pages_of(vbuf, c))
    o_ref[0] = acc / jnp.sum(l_run, axis=-1, keepdims=True)


def _sample_attn_call(page_table, qz, bias_past, bias_new, k_new, v_new, cache_k, cache_v, layer, nt):
    db, npg = page_table.shape
    page = cache_k.shape[3]
    rows = qz.shape[1]
    kern = functools.partial(_sample_attn_kernel, layer=layer, nt=nt)
    grid_spec = pltpu.PrefetchScalarGridSpec(
        num_scalar_prefetch=1,
        grid=(db,),
        in_specs=[pl.BlockSpec((1, rows, LANES), lambda b, pt: (b, 0, 0)),
                  pl.BlockSpec((1, nt, npg * page), lambda b, pt: (b, 0, 0)),
                  pl.BlockSpec((1, nt, LANES), lambda b, pt: (b, 0, 0)),
                  pl.BlockSpec(k_new.shape, lambda b, pt: (0, 0)),
                  pl.BlockSpec(v_new.shape, lambda b, pt: (0, 0)),
                  pl.BlockSpec(memory_space=pl.ANY),
                  pl.BlockSpec(memory_space=pl.ANY)],
        out_specs=pl.BlockSpec((1, rows, LANES), lambda b, pt: (b, 0, 0)),
        scratch_shapes=[pltpu.VMEM((2, npg, LANES, page), F32), pltpu.VMEM((2, npg, LANES, page), F32),
                        pltpu.SemaphoreType.DMA((2,)), pltpu.SemaphoreType.DMA((2,)),
                        pltpu.VMEM((rows, npg * page + LANES), F32)],
    )
    return pl.pallas_call(
        kern, grid_spec=grid_spec,
        out_shape=jax.ShapeDtypeStruct((db, rows, LANES), F32),
        compiler_params=pltpu.CompilerParams(dimension_semantics=("arbitrary",), vmem_limit_bytes=VMEM_LIMIT),
        name="sample_attn",
    )(page_table, qz, bias_past, bias_new, k_new, v_new, cache_k, cache_v)


def _swap_halves(w, head_dim):
    lead = w.shape[:-1]
    nh = w.shape[-1] // head_dim
    return w.reshape(lead + (nh, 2, head_dim // 2))[..., ::-1, :].reshape(lead + (nh * head_dim,))


def _projection_weights(w_in):
    sizes = (A_WIDTH, A_WIDTH, N_HEADS * HEAD_DIM, N_KV_HEADS * HEAD_DIM, N_KV_HEADS * HEAD_DIM,
             IDX_HEADS * IDX_DIM, IDX_DIM, IDX_HEADS)
    pts = [sum(sizes[:i + 1]) for i in range(len(sizes) - 1)]
    au, av, q, k, v, qi, ki, wi = jnp.split(w_in.astype(BF16), pts, axis=-1)
    rep = LANES // IDX_DIM
    q_s, k_s, qi_s, ki_s = (_swap_halves(q, HEAD_DIM), _swap_halves(k, HEAD_DIM),
                            _swap_halves(qi, IDX_DIM), _swap_halves(ki, IDX_DIM))
    ki4, ki4_s = jnp.tile(ki, (1, 1, rep)), jnp.tile(ki_s, (1, 1, rep))
    pad_cols = lambda w, n: jnp.pad(w, ((0, 0), (0, 0), (0, n - w.shape[-1])))
    ext = jnp.concatenate([au, av, q, k, v, qi, ki4, pad_cols(wi, LANES), q_s, k_s, qi_s, ki4_s], axis=-1)
    wn = jnp.concatenate([au, av], axis=-1)
    wt = jnp.swapaxes(jnp.concatenate([q, k, v, qi, ki, pad_cols(wi, WI_ROWS)], axis=-1), 1, 2)
    assert ext.shape[-1] == C_END and wn.shape[-1] == N_END and wt.shape[1] == T_END
    return ext, wn, wt


def _rope_tables(pos, head_dim):
    half = head_dim // 2
    inv = ROPE_THETA ** (-jnp.arange(half, dtype=F32) / half)
    ang = pos.astype(F32)[:, None] * inv[None, :]
    cos, sin = jnp.cos(ang), jnp.sin(ang)
    reps = LANES // head_dim
    return (jnp.tile(jnp.concatenate([cos, cos], axis=-1), (1, reps)),
            jnp.tile(jnp.concatenate([-sin, sin], axis=-1), (1, reps)))


def _pick(n, prefs):
    for p in prefs:
        if n % p == 0:
            return p
    return n


def kernel(x_prompt, x_sample, p_prompt, p_sample, cache_k, cache_v, cache_kidx, page_table, w_in, sgu_ln_g,
           sgu_ln_b, sgu_w, sgu_b, w_o, ln1_g, ln1_b, w_ff1, w_ff2, w_ple_gate, w_ple_proj, ln2_g, ln2_b):
    nb, seq, _ = x_prompt.shape
    db, nt, _ = x_sample.shape
    depth = w_in.shape[0]
    npg = page_table.shape[1]
    page = cache_k.shape[2]
    past = npg * page
    alpha = (2.0 * depth) ** 0.25
    topk_p = min(TOPK_MAX, seq // 4)
    topk_s = min(TOPK_MAX, (past + nt) // 4)
    n_p, n_s = nb * seq, db * nt
    assert n_s == CHUNK and seq % COUNT_CHUNK == 0 and page == LANES

    w_ext, w_n, w_t = _projection_weights(w_in)
    wo_b, wf1_b, wf2_b = w_o.astype(BF16), w_ff1.astype(BF16), w_ff2.astype(BF16)
    wg_b, wp_b = w_ple_gate.astype(BF16), w_ple_proj.astype(BF16)
    tril = jnp.tril(jnp.ones((CHUNK, CHUNK), F32))
    mm_p = (sgu_w * tril).astype(BF16)
    mb_p = jnp.repeat(jnp.swapaxes(sgu_b, 1, 2), A_GW, axis=2)
    r_i = jnp.arange(CHUNK)
    same_req = (r_i[:, None] // nt == r_i[None, :] // nt) & (r_i[None, :] % nt <= r_i[:, None] % nt)
    mm_s = jnp.where(same_req, sgu_w[:, :, r_i % nt][:, :, :, r_i % nt], 0.0).astype(BF16)
    mb_s = jnp.tile(jnp.repeat(jnp.swapaxes(sgu_b[:, :, :nt], 1, 2), A_GW, axis=2), (1, CHUNK // nt, 1))
    lng = sgu_ln_g.reshape(depth, 1, A_WIDTH)
    lnb = sgu_ln_b.reshape(depth, 1, A_WIDTH)
    mlp_w = (wo_b, ln1_g.reshape(depth, 1, D_MODEL), ln1_b.reshape(depth, 1, D_MODEL), wf1_b, wf2_b, wg_b, wp_b,
             ln2_g.reshape(depth, 1, D_MODEL), ln2_b.reshape(depth, 1, D_MODEL))
    pe_p = p_prompt.reshape(depth, n_p, D_PLE)
    pe_s = p_sample.reshape(depth, n_s, D_PLE)

    pos_p = jnp.arange(seq, dtype=I32)
    pos_s = past + (jnp.arange(n_s, dtype=I32) % nt)
    tabs_p = _rope_tables(pos_p, HEAD_DIM) + _rope_tables(pos_p, IDX_DIM)
    tabs_pt = tuple(t.T for t in tabs_p)
    tabs_s = _rope_tables(pos_s, HEAD_DIM) + _rope_tables(pos_s, IDX_DIM)

    cki = jnp.swapaxes(cache_kidx, 2, 3)
    ck = jnp.transpose(cache_k, (0, 1, 3, 4, 2)).reshape(cache_k.shape[:2] + (LANES, page))
    cv = jnp.transpose(cache_v, (0, 1, 3, 4, 2)).reshape(cache_v.shape[:2] + (LANES, page))

    rt_proj = _pick(seq, (512, 256, 128))
    rt_mlp = _pick(n_p, (512, 256, 128))
    qw_p = _pick(seq, (256, 128))
    cw_s = 256
    assert npg % PAGE_GROUP == 0
    head_lo = (jnp.arange(N_HEADS) // KV_GROUP) * HEAD_DIM

    hp = x_prompt.reshape(n_p, D_MODEL)
    hs = x_sample.reshape(n_s, D_MODEL)
    outs = [[] for _ in range(8)]
    for i in range(depth):
        a_out, vn, qt, kb, kt, vt, qit, kib, kit, wit = _proj_prompt_call(
            hp, i, w_n, w_t, mm_p, mb_p, lng, lnb, tabs_pt, nb, seq, rt_proj)
        b_out = _prompt_attn_call(qit, wit, qt, kib, kb, vt, nb, seq, topk_p, qw_p)
        hp = _mlp_call(hp, a_out, b_out, pe_p, i, mlp_w, alpha, rt_mlp)
        to_tokens = lambda a: jnp.transpose(a.reshape(nb, N_KV_HEADS, HEAD_DIM, seq), (0, 3, 1, 2))
        outs[0].append(to_tokens(kt))
        outs[1].append(to_tokens(vt))
        outs[2].append(jnp.swapaxes(kit, 1, 2))
        outs[3].append(vn)

        a_out, vn, q, k, v, qi, ki4, wi = _proj_call(hs, i, w_ext, mm_s, mb_s, lng, lnb, tabs_s)
        qs = qi.reshape(db, nt * IDX_HEADS, IDX_DIM)
        wrep = jnp.broadcast_to((wi[:, :IDX_HEADS] * IDX_SCALE).reshape(db, nt * IDX_HEADS, 1),
                                (db, nt * IDX_HEADS, LANES))
        keys_past = _sample_idx_call(page_table, qs, wrep, cki, i)
        bias_past, bias_new = _sample_select_call(keys_past.reshape(n_s, past), qi, wi, ki4, topk_s, nt, cw_s)
        q4 = q.reshape(db, nt, N_HEADS, 1, HEAD_DIM)
        half = (jnp.arange(N_KV_HEADS) * HEAD_DIM)[None, None, None, :, None]
        qz = jnp.where(half == head_lo[None, None, :, None, None], q4, jnp.zeros_like(q4))
        qz = qz.reshape(db, nt * N_HEADS, LANES)
        o = _sample_attn_call(page_table, qz, bias_past.reshape(db, nt, past), bias_new.reshape(db, nt, LANES),
                              k, v, ck, cv, i, nt)
        o = o.reshape(db, nt, N_HEADS, N_KV_HEADS, HEAD_DIM)
        b_out = jnp.concatenate([o[:, :, g * KV_GROUP:(g + 1) * KV_GROUP, g, :] for g in range(N_KV_HEADS)], axis=2)
        b_out = b_out.reshape(n_s, N_HEADS * HEAD_DIM).astype(BF16)
        hs = _mlp_call(hs, a_out, b_out, pe_s, i, mlp_w, alpha, n_s)
        outs[4].append(k.reshape(db, nt, N_KV_HEADS, HEAD_DIM))
        outs[5].append(v.reshape(db, nt, N_KV_HEADS, HEAD_DIM))
        outs[6].append(ki4[:, :IDX_DIM].reshape(db, nt, IDX_DIM))
        outs[7].append(vn.reshape(db, nt, A_WIDTH))

    st = [jnp.stack(o) for o in outs]
    return (hp.reshape(nb, seq, D_MODEL), hs.reshape(db, nt, D_MODEL),
            st[0], st[1], st[2], st[3], st[4], st[5], st[6], st[7])
```

```python
import functools

import jax
import jax.numpy as jnp
from jax import lax
from jax.experimental import pallas as pl
from jax.experimental.pallas import tpu as pltpu

F32 = jnp.float32
BF16 = jnp.bfloat16
I32 = jnp.int32
I16 = jnp.int16

D_MODEL = 1024
CHUNK = 128
A_WIDTH = 512
A_GROUPS = 4
A_GW = A_WIDTH // A_GROUPS
HEAD_DIM = 64
N_HEADS = 8
N_KV_HEADS = 2
KV_GROUP = N_HEADS // N_KV_HEADS
IDX_HEADS = 8
IDX_DIM = 32
TOPK_MAX = 256
ROPE_THETA = 10000.0
D_FF = 4 * D_MODEL
D_PLE = 256
LN_EPS = 1e-5

LANES = 128
PACK = 16
VMEM_LIMIT = 56 * 1024 * 1024

INT_MIN = -(2 ** 31)
I16_MIN = -(2 ** 15)
NEG = -1e30
DENOM_FLOOR = 2.0 ** -100

C_AU, C_AV, C_Q, C_K, C_V, C_QI, C_KI, C_WI = 0, 512, 1024, 1536, 1664, 1792, 2048, 2176
C_QS, C_KS, C_QIS, C_KIS, C_END = 2304, 2816, 2944, 3200, 3328
N_AU, N_AV, N_END = 0, 512, 1024
T_Q, T_K, T_V, T_QI, T_KI, T_WI, T_END = 0, 512, 640, 768, 1024, 1056, 1072
WI_ROWS = T_END - T_WI
Q_SCALE_LOG2 = (HEAD_DIM ** -0.5) * 1.4426950408889634
IDX_SCALE = (IDX_DIM ** -0.5) * (IDX_HEADS ** -0.5)

KEY_CHUNK = 512
COUNT_CHUNK = 512

NT_DIMS = (((1,), (1,)), ((), ()))


def _dot(a, b):
    return jnp.dot(a, b, preferred_element_type=F32)


def _dot_nt(a, b):
    return lax.dot_general(a, b, NT_DIMS, preferred_element_type=F32)


def _layer_norm(x, g, b):
    mu = jnp.mean(x, axis=-1, keepdims=True)
    xc = x - mu
    var = jnp.mean(xc * xc, axis=-1, keepdims=True)
    return xc * lax.rsqrt(var + LN_EPS) * g + b


def _gelu(x):
    return 0.5 * x * (1.0 + lax.erf(x * (2.0 ** -0.5)))


def _sortable(x):
    bits = pltpu.bitcast(x, I32)
    return bits ^ ((bits >> 31) & 0x7FFFFFFF)


def _layer_spec(stacked, layer):
    shape = stacked.shape[1:]
    return pl.BlockSpec((None,) + shape, lambda *_: (layer,) + (0,) * len(shape),
                        pipeline_mode=pl.Buffered(1))


def _gated_mixer(au, av, g, lng_ref, lnb_ref, mm_ref, mb_ref, vn_ref, aout_ref):
    rows = au.shape[0]
    lo = g * A_GW
    vn = _layer_norm(av, lng_ref[:, lo:lo + A_GW], lnb_ref[:, lo:lo + A_GW])
    vn_ref[:, lo:lo + A_GW] = vn[rows - vn_ref.shape[0]:, :]
    vnb = vn.astype(BF16)
    for c in range(rows // CHUNK):
        r0 = c * CHUNK
        gate = _dot(mm_ref[g], vnb[r0:r0 + CHUNK, :]) + mb_ref[:, lo:lo + A_GW]
        aout_ref[r0:r0 + CHUNK, lo:lo + A_GW] = (au[r0:r0 + CHUNK, :] * gate).astype(BF16)


def _proj_kernel(h_ref, w_ref, mm_ref, mb_ref, lng_ref, lnb_ref, cq_ref, sq_ref, ci_ref, si_ref,
                 aout_ref, vn_ref, q_ref, k_ref, v_ref, qi_ref, ki_ref, wi_ref):
    hb = h_ref[...].astype(BF16)

    def proj(lo, width=LANES):
        return _dot(hb, w_ref[:, lo:lo + width])

    for g in range(A_GROUPS):
        lo = g * A_GW
        _gated_mixer(_gelu(proj(C_AU + lo)), _gelu(proj(C_AV + lo)), g,
                     lng_ref, lnb_ref, mm_ref, mb_ref, vn_ref, aout_ref)

    cq, sq = cq_ref[...], sq_ref[...]
    ci, si = ci_ref[...], si_ref[...]
    for j in range(N_HEADS * HEAD_DIM // LANES):
        z = proj(C_Q + j * LANES) * cq + proj(C_QS + j * LANES) * sq
        q_ref[:, j * LANES:(j + 1) * LANES] = (z * (HEAD_DIM ** -0.5)).astype(BF16)
    k_ref[...] = proj(C_K) * cq + proj(C_KS) * sq
    v_ref[...] = proj(C_V)
    for j in range(IDX_HEADS * IDX_DIM // LANES):
        z = proj(C_QI + j * LANES) * ci + proj(C_QIS + j * LANES) * si
        qi_ref[:, j * LANES:(j + 1) * LANES] = z.astype(BF16)
    ki_ref[...] = proj(C_KI) * ci + proj(C_KIS) * si
    wi_ref[...] = proj(C_WI)


def _proj_call(h2d, layer, w_ext, mm, mb, lng, lnb, tabs):
    n = h2d.shape[0]
    whole = lambda a: pl.BlockSpec(a.shape, lambda i: (0, 0))
    out_widths = [(A_WIDTH, BF16), (A_WIDTH, F32), (N_HEADS * HEAD_DIM, BF16), (LANES, F32), (LANES, F32),
                  (IDX_HEADS * IDX_DIM, BF16), (LANES, F32), (LANES, F32)]
    return pl.pallas_call(
        _proj_kernel,
        grid=(1,),
        in_specs=[whole(h2d)] + [_layer_spec(a, layer) for a in (w_ext, mm, mb, lng, lnb)]
                 + [whole(t) for t in tabs],
        out_specs=[pl.BlockSpec((n, w), lambda i: (0, 0)) for w, _ in out_widths],
        out_shape=[jax.ShapeDtypeStruct((n, w), dt) for w, dt in out_widths],
        compiler_params=pltpu.CompilerParams(dimension_semantics=("arbitrary",), vmem_limit_bytes=VMEM_LIMIT),
        name="proj",
    )(h2d, w_ext, mm, mb, lng, lnb, *tabs)


def _rope_t(x, cos_t, sin_t, head_dim):
    half = head_dim // 2
    parts = []
    for r in range(0, x.shape[0], head_dim):
        parts += [x[r + half:r + head_dim], x[r:r + half]]
    return x * cos_t + jnp.concatenate(parts, axis=0) * sin_t


def _proj_prompt_kernel(h_ref, wn_ref, wt_ref, mm_ref, mb_ref, lng_ref, lnb_ref,
                        cqt_ref, sqt_ref, cit_ref, sit_ref,
                        aout_ref, vn_ref, qt_ref, kb_ref, kt_ref, vt_ref, qit_ref, kib_ref, kit_ref, wit_ref):
    hb = h_ref[...].astype(BF16)
    wide = 2 * LANES

    for gp in range(A_GROUPS // 2):
        au2 = _gelu(_dot(hb, wn_ref[:, N_AU + gp * wide:N_AU + (gp + 1) * wide]))
        av2 = _gelu(_dot(hb, wn_ref[:, N_AV + gp * wide:N_AV + (gp + 1) * wide]))
        for e in range(2):
            _gated_mixer(au2[:, e * A_GW:(e + 1) * A_GW], av2[:, e * A_GW:(e + 1) * A_GW], 2 * gp + e,
                         lng_ref, lnb_ref, mm_ref, mb_ref, vn_ref, aout_ref)

    zt = _dot_nt(wt_ref[...], hb)
    cqt, sqt = cqt_ref[...], sqt_ref[...]
    cit, sit = cit_ref[...], sit_ref[...]
    for j in range(N_HEADS * HEAD_DIM // LANES):
        r = j * LANES
        q = _rope_t(zt[T_Q + r:T_Q + r + LANES], cqt, sqt, HEAD_DIM)
        qt_ref[r:r + LANES, :] = (q * Q_SCALE_LOG2).astype(BF16)
    kt = _rope_t(zt[T_K:T_K + LANES], cqt, sqt, HEAD_DIM)
    kt_ref[...] = kt
    vt_ref[...] = zt[T_V:T_V + LANES]
    for j in range(IDX_HEADS * IDX_DIM // LANES):
        r = j * LANES
        qit_ref[r:r + LANES, :] = _rope_t(zt[T_QI + r:T_QI + r + LANES], cit, sit, IDX_DIM).astype(BF16)
    kit = _rope_t(zt[T_KI:T_KI + IDX_DIM], cit[:IDX_DIM], sit[:IDX_DIM], IDX_DIM)
    kit_ref[...] = kit
    wit_ref[...] = zt[T_WI:T_END]

    kb_ref[...] = kt.T.astype(BF16)
    kib_ref[...] = jnp.concatenate([kit] * (LANES // IDX_DIM), axis=0).T.astype(BF16)


def _proj_prompt_call(h2d, layer, wn, wt, mm, mb, lng, lnb, tabs_t, nb, seq, row_tile):
    n = nb * seq
    ns = seq // row_tile
    row_map = lambda b, s: (b * ns + s, 0)
    col_map = lambda b, s: (0, b * ns + s)
    bcol_map = lambda b, s: (b, 0, s)
    hd, idd = N_HEADS * HEAD_DIM, IDX_HEADS * IDX_DIM
    out_specs = [pl.BlockSpec((row_tile, A_WIDTH), row_map),
                 pl.BlockSpec((None, CHUNK, A_WIDTH), lambda b, s: (b, 0, 0)),
                 pl.BlockSpec((hd, row_tile), col_map),
                 pl.BlockSpec((row_tile, LANES), row_map),
                 pl.BlockSpec((None, LANES, row_tile), bcol_map),
                 pl.BlockSpec((None, LANES, row_tile), bcol_map),
                 pl.BlockSpec((idd, row_tile), col_map),
                 pl.BlockSpec((row_tile, LANES), row_map),
                 pl.BlockSpec((None, IDX_DIM, row_tile), bcol_map),
                 pl.BlockSpec((WI_ROWS, row_tile), col_map)]
    out_shape = [jax.ShapeDtypeStruct((n, A_WIDTH), BF16), jax.ShapeDtypeStruct((nb, CHUNK, A_WIDTH), F32),
                 jax.ShapeDtypeStruct((hd, n), BF16),
                 jax.ShapeDtypeStruct((n, LANES), BF16),
                 jax.ShapeDtypeStruct((nb, LANES, seq), F32),
                 jax.ShapeDtypeStruct((nb, LANES, seq), F32),
                 jax.ShapeDtypeStruct((idd, n), BF16),
                 jax.ShapeDtypeStruct((n, LANES), BF16),
                 jax.ShapeDtypeStruct((nb, IDX_DIM, seq), F32),
                 jax.ShapeDtypeStruct((WI_ROWS, n), F32)]
    return pl.pallas_call(
        _proj_prompt_kernel,
        grid=(nb, ns),
        in_specs=[pl.BlockSpec((row_tile, D_MODEL), row_map)]
                 + [_layer_spec(a, layer) for a in (wn, wt, mm, mb, lng, lnb)]
                 + [pl.BlockSpec((LANES, row_tile), lambda b, s: (0, s))] * 4,
        out_specs=out_specs,
        out_shape=out_shape,
        compiler_params=pltpu.CompilerParams(dimension_semantics=("arbitrary", "arbitrary"),
                                             vmem_limit_bytes=VMEM_LIMIT),
        name="proj_prompt",
    )(h2d, wn, wt, mm, mb, lng, lnb, *tabs_t)


def _count16(ref, nsteps, cand, strict=False):
    one, zero = jnp.ones(cand.shape, I16), jnp.zeros(cand.shape, I16)
    n_acc = 4

    def body(c, accs):
        off = pl.multiple_of(c * COUNT_CHUNK, COUNT_CHUNK)
        blk = ref[pl.ds(off, COUNT_CHUNK), :]
        accs = list(accs)
        for j in range(COUNT_CHUNK // PACK):
            kk = blk[j * PACK:(j + 1) * PACK, :]
            hit = (kk > cand) if strict else (kk >= cand)
            accs[j % n_acc] = accs[j % n_acc] + jnp.where(hit, one, zero)
        return tuple(accs)
    accs = lax.fori_loop(0, nsteps, body, (zero,) * n_acc)
    tot = (accs[0] + accs[1]) + (accs[2] + accs[3])
    return jnp.sum(tot.astype(F32), axis=0, keepdims=True)


def _search16(ref, nsteps, target):
    qw = ref.shape[1]

    def bit_body(i, prefix):
        cand = prefix + lax.shift_left(jnp.int32(1), 15 - i)
        cand16 = jnp.broadcast_to(cand, (PACK, qw)).astype(I16)
        return jnp.where(_count16(ref, nsteps, cand16) >= target, cand, prefix)
    return lax.fori_loop(0, 16, bit_body, jnp.full((1, qw), I16_MIN, I32))


def _prompt_attn_kernel(qit_ref, wit_ref, qt_ref, kib_ref, kb_ref, vt_ref, o_ref,
                        keys_ref, hi_ref, lo_ref, rhs_ref, qz_ref, acc_ref, kmax_ref, *, topk):
    qw = o_ref.shape[0]
    qb = pl.program_id(1)
    nch = (qb * qw + qw + KEY_CHUNK - 1) // KEY_CHUNK
    nsteps = (nch * KEY_CHUNK + COUNT_CHUNK - 1) // COUNT_CHUNK
    qpos = qb * qw + lax.broadcasted_iota(I32, (KEY_CHUNK, qw), 1)
    krow = lax.broadcasted_iota(I32, (KEY_CHUNK, qw), 0)

    @pl.when((pl.program_id(0) == 0) & (qb == 0))
    def _():
        rhs_ref[...] = jnp.zeros(rhs_ref.shape, BF16)
        qz_ref[...] = jnp.zeros(qz_ref.shape, BF16)
    slots = LANES // IDX_DIM
    for h in range(IDX_HEADS):
        r = (h % slots) * IDX_DIM
        rhs_ref[r:r + IDX_DIM, h * qw:(h + 1) * qw] = qit_ref[h * IDX_DIM:(h + 1) * IDX_DIM, :]
    for h in range(N_HEADS):
        r = (h // KV_GROUP) * HEAD_DIM
        qz_ref[r:r + HEAD_DIM, h * qw:(h + 1) * qw] = qt_ref[h * HEAD_DIM:(h + 1) * HEAD_DIM, :]
    wsc = wit_ref[...] * IDX_SCALE

    def score_chunk(c, causal_edge):
        off = pl.multiple_of(c * KEY_CHUNK, KEY_CHUNK)
        kc = kib_ref[pl.ds(off, KEY_CHUNK), :]
        acc = None
        for h in range(IDX_HEADS):
            s = _dot(kc, rhs_ref[:, h * qw:(h + 1) * qw])
            t = jnp.maximum(s, 0.0) * wsc[h:h + 1, :]
            acc = t if acc is None else acc + t
        key = _sortable(acc)
        if causal_edge:
            key = jnp.where(krow + off <= qpos, key, INT_MIN)
        keys_ref[pl.ds(off, KEY_CHUNK), :] = key
        hi_ref[pl.ds(off, KEY_CHUNK), :] = (key >> 16).astype(I16)
        lo_ref[pl.ds(off, KEY_CHUNK), :] = (key ^ 0x8000).astype(I16)

    def score_body(c, carry):
        score_chunk(c, False)
        return carry
    lax.fori_loop(0, nch - 1, score_body, 0)
    score_chunk(nch - 1, True)

    @pl.when(nch * KEY_CHUNK < nsteps * COUNT_CHUNK)
    def _():
        off = pl.multiple_of(nch * KEY_CHUNK, KEY_CHUNK)
        fill = jnp.full((KEY_CHUNK, qw), I16_MIN, I16)
        hi_ref[pl.ds(off, KEY_CHUNK), :] = fill
        lo_ref[pl.ds(off, KEY_CHUNK), :] = fill

    p_hi = _search16(hi_ref, nsteps, float(topk))
    p16 = jnp.broadcast_to(p_hi, (PACK, qw)).astype(I16)
    room = topk - _count16(hi_ref, nsteps, p16, strict=True)

    def low_body(c, carry):
        off = pl.multiple_of(c * COUNT_CHUNK, COUNT_CHUNK)
        lowest = jnp.full((PACK, qw), I16_MIN, I16)
        his, los = hi_ref[pl.ds(off, COUNT_CHUNK), :], lo_ref[pl.ds(off, COUNT_CHUNK), :]
        lo_ref[pl.ds(off, COUNT_CHUNK), :] = jnp.concatenate(
            [jnp.where(his[j * PACK:(j + 1) * PACK, :] == p16, los[j * PACK:(j + 1) * PACK, :], lowest)
             for j in range(COUNT_CHUNK // PACK)], axis=0)
        return carry
    lax.fori_loop(0, nsteps, low_body, 0)
    p_lo = _search16(lo_ref, nsteps, room)
    thr = jnp.maximum(p_hi * 65536 + (p_lo + 2 ** 15), INT_MIN + 1)

    def count32(strict):
        def body(c, acc):
            off = pl.multiple_of(c * KEY_CHUNK, KEY_CHUNK)
            kk = keys_ref[pl.ds(off, KEY_CHUNK), :]
            hit = (kk > thr) if strict else (kk >= thr)
            return acc + jnp.sum(jnp.where(hit, 1.0, 0.0), axis=0, keepdims=True)
        return lax.fori_loop(0, nch, body, jnp.zeros((1, qw), F32))

    @pl.when(jnp.max(count32(False)) > topk)
    def _():
        spare = topk - count32(True)
        r_i = lax.broadcasted_iota(I32, (KEY_CHUNK, KEY_CHUNK), 0)
        c_i = lax.broadcasted_iota(I32, (KEY_CHUNK, KEY_CHUNK), 1)
        lower = jnp.where(c_i <= r_i, 1.0, 0.0).astype(BF16)

        def body(c, seen):
            off = pl.multiple_of(c * KEY_CHUNK, KEY_CHUNK)
            kk = keys_ref[pl.ds(off, KEY_CHUNK), :]
            tie = kk == thr
            rank = seen + _dot(lower, jnp.where(tie, 1.0, 0.0).astype(BF16))
            keys_ref[pl.ds(off, KEY_CHUNK), :] = jnp.where(tie & (rank > spare), thr - 1, kk)
            return rank[KEY_CHUNK - 1:KEY_CHUNK, :]
        lax.fori_loop(0, nch, body, jnp.zeros((1, qw), F32))

    ones_rows = jnp.where(lax.broadcasted_iota(I32, (PACK, KEY_CHUNK), 0) == 0, 1.0, 0.0).astype(BF16)

    @pl.when(qb == 0)
    def _():
        kf = kb_ref[...].astype(F32)
        kmax_ref[0] = jnp.max(jnp.sum(kf * kf, axis=1, keepdims=True))
    shifts = []
    for h in range(N_HEADS):
        qf = qt_ref[h * HEAD_DIM:(h + 1) * HEAD_DIM, :].astype(F32)
        shifts.append(jnp.sqrt(jnp.sum(qf * qf, axis=0, keepdims=True) * kmax_ref[0]))

    def attn_chunk(c, shift_fn):
        off = pl.multiple_of(c * KEY_CHUNK, KEY_CHUNK)
        sel = keys_ref[pl.ds(off, KEY_CHUNK), :] >= thr
        kb = kb_ref[pl.ds(off, KEY_CHUNK), :]
        vt = vt_ref[:, pl.ds(off, KEY_CHUNK)].astype(BF16)
        for g in range(N_KV_HEADS):
            ps, alphas = [], []
            for j in range(KV_GROUP):
                h = g * KV_GROUP + j
                s = jnp.where(sel, _dot(kb, qz_ref[:, h * qw:(h + 1) * qw]), NEG)
                shift, alpha = shift_fn(h, s)
                alphas.append(alpha)
                ps.append(jnp.exp2(s - shift).astype(BF16))
            vg = jnp.concatenate([vt[g * HEAD_DIM:(g + 1) * HEAD_DIM, :], ones_rows], axis=0)
            pv = _dot(vg, jnp.concatenate(ps, axis=1))
            if alphas[0] is None:
                acc_ref[g] = acc_ref[g] + pv
            else:
                acc_ref[g] = acc_ref[g] * jnp.concatenate(alphas, axis=1) + pv

    acc_ref[...] = jnp.zeros(acc_ref.shape, F32)

    def fast_body(c, carry):
        attn_chunk(c, lambda h, s: (shifts[h], None))
        return carry
    lax.fori_loop(0, nch, fast_body, 0)

    denoms = [acc_ref[g][HEAD_DIM:HEAD_DIM + 1, :] for g in range(N_KV_HEADS)]
    denom_min = jnp.min(functools.reduce(jnp.minimum, denoms))

    @pl.when(jnp.logical_not(denom_min > DENOM_FLOOR))
    def _():
        acc_ref[...] = jnp.zeros(acc_ref.shape, F32)

        def online_body(c, ms):
            ms = list(ms)

            def running_max(h, s):
                m_new = jnp.maximum(ms[h], jnp.max(s, axis=0, keepdims=True))
                alpha = jnp.exp2(ms[h] - m_new)
                ms[h] = m_new
                return m_new, alpha
            attn_chunk(c, running_max)
            return tuple(ms)
        lax.fori_loop(0, nch, online_body, (jnp.full((1, qw), NEG, F32),) * N_HEADS)

    for p in range(N_HEADS // 2):
        halves = []
        for h in (2 * p, 2 * p + 1):
            g, j = h // KV_GROUP, h % KV_GROUP
            acc = acc_ref[g][:, j * qw:(j + 1) * qw]
            halves.append(acc[:HEAD_DIM, :] / acc[HEAD_DIM:HEAD_DIM + 1, :])
        o_ref[:, p * LANES:(p + 1) * LANES] = jnp.concatenate(halves, axis=0).T.astype(BF16)


def _prompt_attn_call(qit, wit, qt, kib, kb, vt, nb, seq, topk, qw):
    nq = seq // qw
    hd, idd = N_HEADS * HEAD_DIM, IDX_HEADS * IDX_DIM
    col = lambda b, i: (0, b * nq + i)
    kern = functools.partial(_prompt_attn_kernel, topk=topk)
    return pl.pallas_call(
        kern,
        grid=(nb, nq),
        in_specs=[pl.BlockSpec((idd, qw), col),
                  pl.BlockSpec((WI_ROWS, qw), col),
                  pl.BlockSpec((hd, qw), col),
                  pl.BlockSpec((seq, LANES), lambda b, i: (b, 0)),
                  pl.BlockSpec((seq, LANES), lambda b, i: (b, 0)),
                  pl.BlockSpec((None, LANES, seq), lambda b, i: (b, 0, 0))],
        out_specs=pl.BlockSpec((qw, hd), lambda b, i: (b * nq + i, 0)),
        out_shape=jax.ShapeDtypeStruct((nb * seq, hd), BF16),
        scratch_shapes=[pltpu.VMEM((seq, qw), I32),
                        pltpu.VMEM((seq, qw), I16),
                        pltpu.VMEM((seq, qw), I16),
                        pltpu.VMEM((LANES, IDX_HEADS * qw), BF16),
                        pltpu.VMEM((LANES, N_HEADS * qw), BF16),
                        pltpu.VMEM((N_KV_HEADS, HEAD_DIM + PACK, KV_GROUP * qw), F32),
                        pltpu.SMEM((1,), F32)],
        compiler_params=pltpu.CompilerParams(dimension_semantics=("arbitrary", "arbitrary"),
                                             vmem_limit_bytes=VMEM_LIMIT),
        name="prompt_attn",
    )(qit, wit, qt, kib, kb, vt)


def _mlp_kernel(h_ref, a_ref, b_ref, pe_ref, wo_ref, g1_ref, b1_ref, wf1_ref, wf2_ref, wg_ref, wp_ref,
                g2_ref, b2_ref, out_ref, *, alpha, ff_chunk):
    mix = _dot(a_ref[...], wo_ref[0:A_WIDTH, :]) + _dot(b_ref[...], wo_ref[A_WIDTH:2 * A_WIDTH, :])
    h1 = _layer_norm(alpha * h_ref[...] + mix, g1_ref[...], b1_ref[...])
    h1b = h1.astype(BF16)
    ff = None
    for c in range(D_FF // ff_chunk):
        hid = jnp.maximum(_dot(h1b, wf1_ref[:, c * ff_chunk:(c + 1) * ff_chunk]), 0.0)
        part = _dot((hid * hid).astype(BF16), wf2_ref[c * ff_chunk:(c + 1) * ff_chunk, :])
        ff = part if ff is None else ff + part
    ple = jax.nn.sigmoid(_dot(h1b, wg_ref[...])) * _dot(pe_ref[...].astype(BF16), wp_ref[...])
    out_ref[...] = _layer_norm(alpha * h1 + ff + ple, g2_ref[...], b2_ref[...])


def _mlp_call(h2d, a_out, b_out, pe, layer, consts, alpha, row_tile):
    n = h2d.shape[0]
    row = lambda i: (i, 0)
    kern = functools.partial(_mlp_kernel, alpha=alpha, ff_chunk=1024)
    return pl.pallas_call(
        kern,
        grid=(n // row_tile,),
        in_specs=[pl.BlockSpec((row_tile, D_MODEL), row),
                  pl.BlockSpec((row_tile, A_WIDTH), row),
                  pl.BlockSpec((row_tile, A_WIDTH), row),
                  pl.BlockSpec((None, row_tile, D_PLE), lambda i: (layer, i, 0))]
                 + [_layer_spec(c, layer) for c in consts],
        out_specs=pl.BlockSpec((row_tile, D_MODEL), row),
        out_shape=jax.ShapeDtypeStruct((n, D_MODEL), F32),
        compiler_params=pltpu.CompilerParams(dimension_semantics=("arbitrary",),
                                             vmem_limit_bytes=VMEM_LIMIT),
        name="mlp",
    )(h2d, a_out, b_out, pe, *consts)


def _head_masked_queries(qi_ref):
    lane_head = lax.broadcasted_iota(I32, (CHUNK, LANES), 1) >> 5
    heads_per_col = LANES // IDX_DIM
    out = []
    for h in range(IDX_HEADS):
        col = qi_ref[:, (h // heads_per_col) * LANES:(h // heads_per_col + 1) * LANES]
        out.append(jnp.where(lane_head == (h % heads_per_col), col, jnp.zeros_like(col)))
    return out


def _index_scores(qh, wsc, kc):
    acc = None
    for h in range(IDX_HEADS):
        s = _dot_nt(qh[h], kc)
        t = jnp.maximum(s, 0.0) * wsc[:, h:h + 1]
        acc = t if acc is None else acc + t
    return acc


def _count_ge(keys_ref, nch, cw, cand, strict=False):
    def body(c, acc):
        off = pl.multiple_of(c * cw, cw)
        for j in range(cw // LANES):
            kk = keys_ref[:, pl.ds(off + j * LANES, LANES)]
            hit = (kk > cand) if strict else (kk >= cand)
            acc = acc + jnp.where(hit, 1.0, 0.0)
        return acc
    acc = lax.fori_loop(0, nch, body, jnp.zeros((CHUNK, LANES), F32))
    cnt = jnp.sum(acc, axis=-1, keepdims=True)
    return jnp.broadcast_to(cnt, (CHUNK, LANES))


def _select_threshold(keys_ref, nch, cw, topk):
    def bit_body(i, prefix):
        cand = prefix + lax.shift_left(jnp.int32(1), 31 - i)
        cnt = _count_ge(keys_ref, nch, cw, cand)
        return jnp.where(cnt >= topk, cand, prefix)
    prefix = lax.fori_loop(0, 32, bit_body, jnp.full((CHUNK, LANES), INT_MIN, I32))
    thr = jnp.maximum(prefix, INT_MIN + 1)
    cnt_ge = _count_ge(keys_ref, nch, cw, thr)

    @pl.when(jnp.max(cnt_ge) > topk)
    def _():
        room = topk - _count_ge(keys_ref, nch, cw, thr, strict=True)
        r_i = lax.broadcasted_iota(I32, (LANES, LANES), 0)
        c_i = lax.broadcasted_iota(I32, (LANES, LANES), 1)
        upper = jnp.where(r_i <= c_i, 1.0, 0.0).astype(BF16)

        def body(c, seen):
            off = pl.multiple_of(c * cw, cw)
            for j in range(cw // LANES):
                kk = keys_ref[:, pl.ds(off + j * LANES, LANES)]
                tie = kk == thr
                rank = seen + _dot(jnp.where(tie, 1.0, 0.0).astype(BF16), upper)
                keys_ref[:, pl.ds(off + j * LANES, LANES)] = jnp.where(tie & (rank > room), thr - 1, kk)
                seen = jnp.broadcast_to(rank[:, LANES - 1:LANES], (CHUNK, LANES))
            return seen
        lax.fori_loop(0, nch, body, jnp.zeros((CHUNK, LANES), F32))

    return thr


PAGE_GROUP = 8


def _request_pages(pt_ref, layer, npg, streams):
    b, nreq = pl.program_id(0), pl.num_programs(0)

    def page_copy(stream, req, slot, j):
        cache, buf, sem = stream
        return pltpu.make_async_copy(cache.at[layer, pt_ref[req, j]], buf.at[slot, j], sem.at[slot])

    def start_all(req, slot):
        def body(i, carry):
            for prio in range(2):
                for stream in streams:
                    page_copy(stream, req, slot, 2 * i + prio).start(priority=prio)
            return carry
        lax.fori_loop(0, npg // 2, body, 0)

    @pl.when(b == 0)
    def _():
        start_all(0, 0)

    @pl.when(b + 1 < nreq)
    def _():
        start_all(b + 1, (b + 1) % 2)

    slot = b % 2

    def wait_body(j, carry):
        for stream in streams:
            page_copy(stream, b, slot, j).wait()
        return carry
    lax.fori_loop(0, npg, wait_body, 0)
    return slot


def _sample_idx_kernel(pt_ref, qs_ref, w_ref, cki_ref, out_ref, buf, sem, *, layer):
    npg = buf.shape[1]
    slot = _request_pages(pt_ref, layer, npg, [(cki_ref, buf, sem)])
    qs = qs_ref[0]
    w = jnp.concatenate([w_ref[0]] * PAGE_GROUP, axis=1)
    nt = out_ref.shape[1]
    width = PAGE_GROUP * LANES
    for c in range(npg // PAGE_GROUP):
        kp = jnp.concatenate([buf[slot, c * PAGE_GROUP + i] for i in range(PAGE_GROUP)], axis=1)
        t = jnp.maximum(_dot(qs, kp.astype(BF16)), 0.0) * w
        for q in range(nt):
            sc = jnp.sum(t[q * IDX_HEADS:(q + 1) * IDX_HEADS, :], axis=0, keepdims=True)
            out_ref[0, q:q + 1, c * width:(c + 1) * width] = _sortable(sc)


def _sample_idx_call(page_table, qs, wrep, cache_kidx, layer):
    db, npg = page_table.shape
    nt = qs.shape[1] // IDX_HEADS
    page = cache_kidx.shape[3]
    kern = functools.partial(_sample_idx_kernel, layer=layer)
    grid_spec = pltpu.PrefetchScalarGridSpec(
        num_scalar_prefetch=1,
        grid=(db,),
        in_specs=[pl.BlockSpec((1,) + qs.shape[1:], lambda b, pt: (b, 0, 0)),
                  pl.BlockSpec((1,) + wrep.shape[1:], lambda b, pt: (b, 0, 0)),
                  pl.BlockSpec(memory_space=pl.ANY)],
        out_specs=pl.BlockSpec((1, nt, npg * page), lambda b, pt: (b, 0, 0)),
        scratch_shapes=[pltpu.VMEM((2, npg, IDX_DIM, page), F32), pltpu.SemaphoreType.DMA((2,))],
    )
    return pl.pallas_call(
        kern, grid_spec=grid_spec,
        out_shape=jax.ShapeDtypeStruct((db, nt, npg * page), I32),
        compiler_params=pltpu.CompilerParams(dimension_semantics=("arbitrary",), vmem_limit_bytes=VMEM_LIMIT),
        name="sample_idx",
    )(page_table, qs, wrep, cache_kidx)


def _sample_select_kernel(kp_ref, qi_ref, wi_ref, ki_ref, bp_ref, bn_ref, keys_ref, *, topk, cw, nt):
    nreq, _, past = kp_ref.shape
    width = keys_ref.shape[1]
    nch = width // cw
    for b in range(nreq):
        keys_ref[b * nt:(b + 1) * nt, 0:past] = kp_ref[b]
    qh = _head_masked_queries(qi_ref)
    wsc = wi_ref[...] * IDX_SCALE
    key = _sortable(_index_scores(qh, wsc, ki_ref[...].astype(BF16)))
    r_i = lax.broadcasted_iota(I32, (CHUNK, LANES), 0)
    c_i = lax.broadcasted_iota(I32, (CHUNK, LANES), 1)
    visible = ((r_i // nt) == (c_i // nt)) & ((c_i % nt) <= (r_i % nt))
    keys_ref[:, past:past + LANES] = jnp.where(visible, key, INT_MIN)
    if width > past + LANES:
        keys_ref[:, past + LANES:width] = jnp.full((CHUNK, width - past - LANES), INT_MIN, I32)

    thr = _select_threshold(keys_ref, nch, cw, topk)

    for b in range(nreq):
        rows = slice(b * nt, (b + 1) * nt)
        bp_ref[b] = jnp.where(keys_ref[rows, 0:past] >= thr[rows, 0:1], 0.0, NEG)
    bn_ref[...] = jnp.where(keys_ref[:, past:past + LANES] >= thr, 0.0, NEG)


def _sample_select_call(keys_past, qi, wi, ki4, topk, nt, cw):
    nreq, _, past = keys_past.shape
    rows = nreq * nt
    width = -(-(past + LANES) // cw) * cw
    kern = functools.partial(_sample_select_kernel, topk=topk, cw=cw, nt=nt)
    return pl.pallas_call(
        kern,
        out_shape=[jax.ShapeDtypeStruct((nreq, nt, past), F32), jax.ShapeDtypeStruct((rows, LANES), F32)],
        scratch_shapes=[pltpu.VMEM((rows, width), I32)],
        compiler_params=pltpu.CompilerParams(vmem_limit_bytes=VMEM_LIMIT),
        name="sample_select",
    )(keys_past, qi, wi, ki4)


def _sample_attn_kernel(pt_ref, qz_ref, bp_ref, bn_ref, kn_ref, vn_ref, ck_ref, cv_ref, o_ref,
                        kbuf, vbuf, ksem, vsem, s_ref, *, layer, nt):
    npg = kbuf.shape[1]
    slot = _request_pages(pt_ref, layer, npg, [(ck_ref, kbuf, ksem), (cv_ref, vbuf, vsem)])
    qz = qz_ref[0]
    rows = qz.shape[0]
    hpq = rows // nt
    width = PAGE_GROUP * LANES
    groups = npg // PAGE_GROUP
    past = npg * LANES

    def lane_blocks(x):
        return [x[:, i * LANES:(i + 1) * LANES] for i in range(x.shape[1] // LANES)]

    def masked(s, bias):
        return jnp.concatenate(
            [s[q * hpq:(q + 1) * hpq, :] + bias[q:q + 1, :] for q in range(nt)], axis=0)

    def pages_of(buf, c):
        return jnp.concatenate([buf[slot, c * PAGE_GROUP + i] for i in range(PAGE_GROUP)], axis=1).astype(BF16)

    bias_past = bp_ref[0]
    m_run = jnp.full((rows, LANES), NEG, F32)
    for c in range(groups):
        s = masked(_dot(qz, pages_of(kbuf, c)), bias_past[:, c * width:(c + 1) * width])
        s_ref[:, c * width:(c + 1) * width] = s
        m_run = functools.reduce(jnp.maximum, lane_blocks(s), m_run)
    s_new = masked(_dot_nt(qz, kn_ref[...].astype(BF16)), bn_ref[0])
    s_ref[:, past:past + LANES] = s_new
    m = jnp.max(jnp.maximum(m_run, s_new), axis=-1, keepdims=True)

    pr = jnp.exp(s_new - m)
    l_run = pr
    acc = _dot(pr.astype(BF16), vn_ref[...].astype(BF16))
    for c in range(groups):
        pr = jnp.exp(s_ref[:, c * width:(c + 1) * width] - m)
        l_run = functools.reduce(jnp.add, lane_blocks(pr), l_run)
        acc = acc + _dot_nt(pr.astype(BF16), pages_of(vbuf, c))
    o_ref[0] = acc / jnp.sum(l_run, axis=-1, keepdims=True)


def _sample_attn_call(page_table, qz, bias_past, bias_new, k_new, v_new, cache_k, cache_v, layer, nt):
    db, npg = page_table.shape
    page = cache_k.shape[3]
    rows = qz.shape[1]
    kern = functools.partial(_sample_attn_kernel, layer=layer, nt=nt)
    grid_spec = pltpu.PrefetchScalarGridSpec(
        num_scalar_prefetch=1,
        grid=(db,),
        in_specs=[pl.BlockSpec((1, rows, LANES), lambda b, pt: (b, 0, 0)),
                  pl.BlockSpec((1, nt, npg * page), lambda b, pt: (b, 0, 0)),
                  pl.BlockSpec((1, nt, LANES), lambda b, pt: (b, 0, 0)),
                  pl.BlockSpec(k_new.shape, lambda b, pt: (0, 0)),
                  pl.BlockSpec(v_new.shape, lambda b, pt: (0, 0)),
                  pl.BlockSpec(memory_space=pl.ANY),
                  pl.BlockSpec(memory_space=pl.ANY)],
        out_specs=pl.BlockSpec((1, rows, LANES), lambda b, pt: (b, 0, 0)),
        scratch_shapes=[pltpu.VMEM((2, npg, LANES, page), F32), pltpu.VMEM((2, npg, LANES, page), F32),
                        pltpu.SemaphoreType.DMA((2,)), pltpu.SemaphoreType.DMA((2,)),
                        pltpu.VMEM((rows, npg * page + LANES), F32)],
    )
    return pl.pallas_call(
        kern, grid_spec=grid_spec,
        out_shape=jax.ShapeDtypeStruct((db, rows, LANES), F32),
        compiler_params=pltpu.CompilerParams(dimension_semantics=("arbitrary",), vmem_limit_bytes=VMEM_LIMIT),
        name="sample_attn",
    )(page_table, qz, bias_past, bias_new, k_new, v_new, cache_k, cache_v)


def _swap_halves(w, head_dim):
    lead = w.shape[:-1]
    nh = w.shape[-1] // head_dim
    return w.reshape(lead + (nh, 2, head_dim // 2))[..., ::-1, :].reshape(lead + (nh * head_dim,))


def _projection_weights(w_in):
    sizes = (A_WIDTH, A_WIDTH, N_HEADS * HEAD_DIM, N_KV_HEADS * HEAD_DIM, N_KV_HEADS * HEAD_DIM,
             IDX_HEADS * IDX_DIM, IDX_DIM, IDX_HEADS)
    pts = [sum(sizes[:i + 1]) for i in range(len(sizes) - 1)]
    au, av, q, k, v, qi, ki, wi = jnp.split(w_in.astype(BF16), pts, axis=-1)
    rep = LANES // IDX_DIM
    q_s, k_s, qi_s, ki_s = (_swap_halves(q, HEAD_DIM), _swap_halves(k, HEAD_DIM),
                            _swap_halves(qi, IDX_DIM), _swap_halves(ki, IDX_DIM))
    ki4, ki4_s = jnp.tile(ki, (1, 1, rep)), jnp.tile(ki_s, (1, 1, rep))
    pad_cols = lambda w, n: jnp.pad(w, ((0, 0), (0, 0), (0, n - w.shape[-1])))
    ext = jnp.concatenate([au, av, q, k, v, qi, ki4, pad_cols(wi, LANES), q_s, k_s, qi_s, ki4_s], axis=-1)
    wn = jnp.concatenate([au, av], axis=-1)
    wt = jnp.swapaxes(jnp.concatenate([q, k, v, qi, ki, pad_cols(wi, WI_ROWS)], axis=-1), 1, 2)
    assert ext.shape[-1] == C_END and wn.shape[-1] == N_END and wt.shape[1] == T_END
    return ext, wn, wt


def _rope_tables(pos, head_dim):
    half = head_dim // 2
    inv = ROPE_THETA ** (-jnp.arange(half, dtype=F32) / half)
    ang = pos.astype(F32)[:, None] * inv[None, :]
    cos, sin = jnp.cos(ang), jnp.sin(ang)
    reps = LANES // head_dim
    return (jnp.tile(jnp.concatenate([cos, cos], axis=-1), (1, reps)),
            jnp.tile(jnp.concatenate([-sin, sin], axis=-1), (1, reps)))


def _pick(n, prefs):
    for p in prefs:
        if n % p == 0:
            return p
    return n


def kernel(x_prompt, x_sample, p_prompt, p_sample, cache_k, cache_v, cache_kidx, page_table, w_in, sgu_ln_g,
           sgu_ln_b, sgu_w, sgu_b, w_o, ln1_g, ln1_b, w_ff1, w_ff2, w_ple_gate, w_ple_proj, ln2_g, ln2_b):
    nb, seq, _ = x_prompt.shape
    db, nt, _ = x_sample.shape
    depth = w_in.shape[0]
    npg = page_table.shape[1]
    page = cache_k.shape[2]
    past = npg * page
    alpha = (2.0 * depth) ** 0.25
    topk_p = min(TOPK_MAX, seq // 4)
    topk_s = min(TOPK_MAX, (past + nt) // 4)
    n_p, n_s = nb * seq, db * nt
    assert n_s == CHUNK and seq % COUNT_CHUNK == 0 and page == LANES

    w_ext, w_n, w_t = _projection_weights(w_in)
    wo_b, wf1_b, wf2_b = w_o.astype(BF16), w_ff1.astype(BF16), w_ff2.astype(BF16)
    wg_b, wp_b = w_ple_gate.astype(BF16), w_ple_proj.astype(BF16)
    tril = jnp.tril(jnp.ones((CHUNK, CHUNK), F32))
    mm_p = (sgu_w * tril).astype(BF16)
    mb_p = jnp.repeat(jnp.swapaxes(sgu_b, 1, 2), A_GW, axis=2)
    r_i = jnp.arange(CHUNK)
    same_req = (r_i[:, None] // nt == r_i[None, :] // nt) & (r_i[None, :] % nt <= r_i[:, None] % nt)
    mm_s = jnp.where(same_req, sgu_w[:, :, r_i % nt][:, :, :, r_i % nt], 0.0).astype(BF16)
    mb_s = jnp.tile(jnp.repeat(jnp.swapaxes(sgu_b[:, :, :nt], 1, 2), A_GW, axis=2), (1, CHUNK // nt, 1))
    lng = sgu_ln_g.reshape(depth, 1, A_WIDTH)
    lnb = sgu_ln_b.reshape(depth, 1, A_WIDTH)
    mlp_w = (wo_b, ln1_g.reshape(depth, 1, D_MODEL), ln1_b.reshape(depth, 1, D_MODEL), wf1_b, wf2_b, wg_b, wp_b,
             ln2_g.reshape(depth, 1, D_MODEL), ln2_b.reshape(depth, 1, D_MODEL))
    pe_p = p_prompt.reshape(depth, n_p, D_PLE)
    pe_s = p_sample.reshape(depth, n_s, D_PLE)

    pos_p = jnp.arange(seq, dtype=I32)
    pos_s = past + (jnp.arange(n_s, dtype=I32) % nt)
    tabs_p = _rope_tables(pos_p, HEAD_DIM) + _rope_tables(pos_p, IDX_DIM)
    tabs_pt = tuple(t.T for t in tabs_p)
    tabs_s = _rope_tables(pos_s, HEAD_DIM) + _rope_tables(pos_s, IDX_DIM)

    cki = jnp.swapaxes(cache_kidx, 2, 3)
    ck = jnp.transpose(cache_k, (0, 1, 3, 4, 2)).reshape(cache_k.shape[:2] + (LANES, page))
    cv = jnp.transpose(cache_v, (0, 1, 3, 4, 2)).reshape(cache_v.shape[:2] + (LANES, page))

    rt_proj = _pick(seq, (512, 256, 128))
    rt_mlp = _pick(n_p, (512, 256, 128))
    qw_p = _pick(seq, (256, 128))
    cw_s = 256
    assert npg % PAGE_GROUP == 0
    head_lo = (jnp.arange(N_HEADS) // KV_GROUP) * HEAD_DIM

    hp = x_prompt.reshape(n_p, D_MODEL)
    hs = x_sample.reshape(n_s, D_MODEL)
    outs = [[] for _ in range(8)]
    for i in range(depth):
        a_out, vn, qt, kb, kt, vt, qit, kib, kit, wit = _proj_prompt_call(
            hp, i, w_n, w_t, mm_p, mb_p, lng, lnb, tabs_pt, nb, seq, rt_proj)
        b_out = _prompt_attn_call(qit, wit, qt, kib, kb, vt, nb, seq, topk_p, qw_p)
        hp = _mlp_call(hp, a_out, b_out, pe_p, i, mlp_w, alpha, rt_mlp)
        to_tokens = lambda a: jnp.transpose(a.reshape(nb, N_KV_HEADS, HEAD_DIM, seq), (0, 3, 1, 2))
        outs[0].append(to_tokens(kt))
        outs[1].append(to_tokens(vt))
        outs[2].append(jnp.swapaxes(kit, 1, 2))
        outs[3].append(vn)

        a_out, vn, q, k, v, qi, ki4, wi = _proj_call(hs, i, w_ext, mm_s, mb_s, lng, lnb, tabs_s)
        qs = qi.reshape(db, nt * IDX_HEADS, IDX_DIM)
        wrep = jnp.broadcast_to((wi[:, :IDX_HEADS] * IDX_SCALE).reshape(db, nt * IDX_HEADS, 1),
                                (db, nt * IDX_HEADS, LANES))
        keys_past = _sample_idx_call(page_table, qs, wrep, cki, i)
        bias_past, bias_new = _sample_select_call(keys_past, qi, wi, ki4, topk_s, nt, cw_s)
        q4 = q.reshape(db, nt, N_HEADS, 1, HEAD_DIM)
        half = (jnp.arange(N_KV_HEADS) * HEAD_DIM)[None, None, None, :, None]
        qz = jnp.where(half == head_lo[None, None, :, None, None], q4, jnp.zeros_like(q4))
        qz = qz.reshape(db, nt * N_HEADS, LANES)
        o = _sample_attn_call(page_table, qz, bias_past, bias_new.reshape(db, nt, LANES),
                              k, v, ck, cv, i, nt)
        o = o.reshape(db, nt, N_HEADS, N_KV_HEADS, HEAD_DIM)
        b_out = jnp.concatenate([o[:, :, g * KV_GROUP:(g + 1) * KV_GROUP, g, :] for g in range(N_KV_HEADS)], axis=2)
        b_out = b_out.reshape(n_s, N_HEADS * HEAD_DIM).astype(BF16)
        hs = _mlp_call(hs, a_out, b_out, pe_s, i, mlp_w, alpha, n_s)
        outs[4].append(k.reshape(db, nt, N_KV_HEADS, HEAD_DIM))
        outs[5].append(v.reshape(db, nt, N_KV_HEADS, HEAD_DIM))
        outs[6].append(ki4[:, :IDX_DIM].reshape(db, nt, IDX_DIM))
        outs[7].append(vn.reshape(db, nt, A_WIDTH))

    st = [jnp.stack(o) for o in outs]
    return (hp.reshape(nb, seq, D_MODEL), hs.reshape(db, nt, D_MODEL),
            st[0], st[1], st[2], st[3], st[4], st[5], st[6], st[7])
```

```python
import functools

import jax
import jax.numpy as jnp
from jax import lax
from jax.experimental import pallas as pl
from jax.experimental.pallas import tpu as pltpu

F32 = jnp.float32
BF16 = jnp.bfloat16
I32 = jnp.int32
I16 = jnp.int16

D_MODEL = 1024
CHUNK = 128
A_WIDTH = 512
A_GROUPS = 4
A_GW = A_WIDTH // A_GROUPS
HEAD_DIM = 64
N_HEADS = 8
N_KV_HEADS = 2
KV_GROUP = N_HEADS // N_KV_HEADS
IDX_HEADS = 8
IDX_DIM = 32
TOPK_MAX = 256
ROPE_THETA = 10000.0
D_FF = 4 * D_MODEL
D_PLE = 256
LN_EPS = 1e-5

LANES = 128
PACK = 16
VMEM_LIMIT = 56 * 1024 * 1024

INT_MIN = -(2 ** 31)
I16_MIN = -(2 ** 15)
NEG = -1e30
DENOM_FLOOR = 2.0 ** -100

C_AU, C_AV, C_Q, C_K, C_V, C_QI, C_KI, C_WI = 0, 512, 1024, 1536, 1664, 1792, 2048, 2176
C_QS, C_KS, C_QIS, C_KIS, C_END = 2304, 2816, 2944, 3200, 3328
N_AU, N_AV, N_END = 0, 512, 1024
T_Q, T_K, T_V, T_QI, T_KI, T_WI, T_END = 0, 512, 640, 768, 1024, 1056, 1072
WI_ROWS = T_END - T_WI
Q_SCALE_LOG2 = (HEAD_DIM ** -0.5) * 1.4426950408889634
IDX_SCALE = (IDX_DIM ** -0.5) * (IDX_HEADS ** -0.5)

KEY_CHUNK = 512
COUNT_CHUNK = 512

NT_DIMS = (((1,), (1,)), ((), ()))


def _dot(a, b):
    return jnp.dot(a, b, preferred_element_type=F32)


def _dot_nt(a, b):
    return lax.dot_general(a, b, NT_DIMS, preferred_element_type=F32)


def _layer_norm(x, g, b):
    mu = jnp.mean(x, axis=-1, keepdims=True)
    xc = x - mu
    var = jnp.mean(xc * xc, axis=-1, keepdims=True)
    return xc * lax.rsqrt(var + LN_EPS) * g + b


def _gelu(x):
    return 0.5 * x * (1.0 + lax.erf(x * (2.0 ** -0.5)))


def _sortable(x):
    bits = pltpu.bitcast(x, I32)
    return bits ^ ((bits >> 31) & 0x7FFFFFFF)


def _layer_spec(stacked, layer):
    shape = stacked.shape[1:]
    return pl.BlockSpec((None,) + shape, lambda *_: (layer,) + (0,) * len(shape),
                        pipeline_mode=pl.Buffered(1))


def _gated_mixer(au, av, g, lng_ref, lnb_ref, mm_ref, mb_ref, vn_ref, aout_ref):
    rows = au.shape[0]
    lo = g * A_GW
    vn = _layer_norm(av, lng_ref[:, lo:lo + A_GW], lnb_ref[:, lo:lo + A_GW])
    vn_ref[:, lo:lo + A_GW] = vn[rows - vn_ref.shape[0]:, :]
    vnb = vn.astype(BF16)
    for c in range(rows // CHUNK):
        r0 = c * CHUNK
        gate = _dot(mm_ref[g], vnb[r0:r0 + CHUNK, :]) + mb_ref[:, lo:lo + A_GW]
        aout_ref[r0:r0 + CHUNK, lo:lo + A_GW] = (au[r0:r0 + CHUNK, :] * gate).astype(BF16)


def _proj_kernel(h_ref, w_ref, mm_ref, mb_ref, lng_ref, lnb_ref, cq_ref, sq_ref, ci_ref, si_ref,
                 aout_ref, vn_ref, q_ref, k_ref, v_ref, qi_ref, ki_ref, wi_ref):
    hb = h_ref[...].astype(BF16)

    def proj(lo, width=LANES):
        return _dot(hb, w_ref[:, lo:lo + width])

    for g in range(A_GROUPS):
        lo = g * A_GW
        _gated_mixer(_gelu(proj(C_AU + lo)), _gelu(proj(C_AV + lo)), g,
                     lng_ref, lnb_ref, mm_ref, mb_ref, vn_ref, aout_ref)

    cq, sq = cq_ref[...], sq_ref[...]
    ci, si = ci_ref[...], si_ref[...]
    for j in range(N_HEADS * HEAD_DIM // LANES):
        z = proj(C_Q + j * LANES) * cq + proj(C_QS + j * LANES) * sq
        q_ref[:, j * LANES:(j + 1) * LANES] = (z * (HEAD_DIM ** -0.5)).astype(BF16)
    k_ref[...] = proj(C_K) * cq + proj(C_KS) * sq
    v_ref[...] = proj(C_V)
    for j in range(IDX_HEADS * IDX_DIM // LANES):
        z = proj(C_QI + j * LANES) * ci + proj(C_QIS + j * LANES) * si
        qi_ref[:, j * LANES:(j + 1) * LANES] = z.astype(BF16)
    ki_ref[...] = proj(C_KI) * ci + proj(C_KIS) * si
    wi_ref[...] = proj(C_WI)


def _proj_call(h2d, layer, w_ext, mm, mb, lng, lnb, tabs):
    n = h2d.shape[0]
    whole = lambda a: pl.BlockSpec(a.shape, lambda i: (0, 0))
    out_widths = [(A_WIDTH, BF16), (A_WIDTH, F32), (N_HEADS * HEAD_DIM, BF16), (LANES, F32), (LANES, F32),
                  (IDX_HEADS * IDX_DIM, BF16), (LANES, F32), (LANES, F32)]
    return pl.pallas_call(
        _proj_kernel,
        grid=(1,),
        in_specs=[whole(h2d)] + [_layer_spec(a, layer) for a in (w_ext, mm, mb, lng, lnb)]
                 + [whole(t) for t in tabs],
        out_specs=[pl.BlockSpec((n, w), lambda i: (0, 0)) for w, _ in out_widths],
        out_shape=[jax.ShapeDtypeStruct((n, w), dt) for w, dt in out_widths],
        compiler_params=pltpu.CompilerParams(dimension_semantics=("arbitrary",), vmem_limit_bytes=VMEM_LIMIT),
        name="proj",
    )(h2d, w_ext, mm, mb, lng, lnb, *tabs)


def _rope_t(x, cos_t, sin_t, head_dim):
    half = head_dim // 2
    parts = []
    for r in range(0, x.shape[0], head_dim):
        parts += [x[r + half:r + head_dim], x[r:r + half]]
    return x * cos_t + jnp.concatenate(parts, axis=0) * sin_t


def _proj_prompt_kernel(h_ref, wn_ref, wt_ref, mm_ref, mb_ref, lng_ref, lnb_ref,
                        cqt_ref, sqt_ref, cit_ref, sit_ref,
                        aout_ref, vn_ref, qt_ref, kb_ref, kt_ref, vt_ref, qit_ref, kib_ref, kit_ref, wit_ref):
    hb = h_ref[...].astype(BF16)
    wide = 2 * LANES

    for gp in range(A_GROUPS // 2):
        au2 = _gelu(_dot(hb, wn_ref[:, N_AU + gp * wide:N_AU + (gp + 1) * wide]))
        av2 = _gelu(_dot(hb, wn_ref[:, N_AV + gp * wide:N_AV + (gp + 1) * wide]))
        for e in range(2):
            _gated_mixer(au2[:, e * A_GW:(e + 1) * A_GW], av2[:, e * A_GW:(e + 1) * A_GW], 2 * gp + e,
                         lng_ref, lnb_ref, mm_ref, mb_ref, vn_ref, aout_ref)

    zt = _dot_nt(wt_ref[...], hb)
    cqt, sqt = cqt_ref[...], sqt_ref[...]
    cit, sit = cit_ref[...], sit_ref[...]
    for j in range(N_HEADS * HEAD_DIM // LANES):
        r = j * LANES
        q = _rope_t(zt[T_Q + r:T_Q + r + LANES], cqt, sqt, HEAD_DIM)
        qt_ref[r:r + LANES, :] = (q * Q_SCALE_LOG2).astype(BF16)
    kt = _rope_t(zt[T_K:T_K + LANES], cqt, sqt, HEAD_DIM)
    kt_ref[...] = kt
    vt_ref[...] = zt[T_V:T_V + LANES]
    for j in range(IDX_HEADS * IDX_DIM // LANES):
        r = j * LANES
        qit_ref[r:r + LANES, :] = _rope_t(zt[T_QI + r:T_QI + r + LANES], cit, sit, IDX_DIM).astype(BF16)
    kit = _rope_t(zt[T_KI:T_KI + IDX_DIM], cit[:IDX_DIM], sit[:IDX_DIM], IDX_DIM)
    kit_ref[...] = kit
    wit_ref[...] = zt[T_WI:T_END]

    kb_ref[...] = kt.T.astype(BF16)
    kib_ref[...] = jnp.concatenate([kit] * (LANES // IDX_DIM), axis=0).T.astype(BF16)


def _proj_prompt_call(h2d, layer, wn, wt, mm, mb, lng, lnb, tabs_t, nb, seq, row_tile):
    n = nb * seq
    ns = seq // row_tile
    row_map = lambda b, s: (b * ns + s, 0)
    col_map = lambda b, s: (0, b * ns + s)
    bcol_map = lambda b, s: (b, 0, s)
    hd, idd = N_HEADS * HEAD_DIM, IDX_HEADS * IDX_DIM
    out_specs = [pl.BlockSpec((row_tile, A_WIDTH), row_map),
                 pl.BlockSpec((None, CHUNK, A_WIDTH), lambda b, s: (b, 0, 0)),
                 pl.BlockSpec((hd, row_tile), col_map),
                 pl.BlockSpec((row_tile, LANES), row_map),
                 pl.BlockSpec((None, LANES, row_tile), bcol_map),
                 pl.BlockSpec((None, LANES, row_tile), bcol_map),
                 pl.BlockSpec((idd, row_tile), col_map),
                 pl.BlockSpec((row_tile, LANES), row_map),
                 pl.BlockSpec((None, IDX_DIM, row_tile), bcol_map),
                 pl.BlockSpec((WI_ROWS, row_tile), col_map)]
    out_shape = [jax.ShapeDtypeStruct((n, A_WIDTH), BF16), jax.ShapeDtypeStruct((nb, CHUNK, A_WIDTH), F32),
                 jax.ShapeDtypeStruct((hd, n), BF16),
                 jax.ShapeDtypeStruct((n, LANES), BF16),
                 jax.ShapeDtypeStruct((nb, LANES, seq), F32),
                 jax.ShapeDtypeStruct((nb, LANES, seq), F32),
                 jax.ShapeDtypeStruct((idd, n), BF16),
                 jax.ShapeDtypeStruct((n, LANES), BF16),
                 jax.ShapeDtypeStruct((nb, IDX_DIM, seq), F32),
                 jax.ShapeDtypeStruct((WI_ROWS, n), F32)]
    return pl.pallas_call(
        _proj_prompt_kernel,
        grid=(nb, ns),
        in_specs=[pl.BlockSpec((row_tile, D_MODEL), row_map)]
                 + [_layer_spec(a, layer) for a in (wn, wt, mm, mb, lng, lnb)]
                 + [pl.BlockSpec((LANES, row_tile), lambda b, s: (0, s))] * 4,
        out_specs=out_specs,
        out_shape=out_shape,
        compiler_params=pltpu.CompilerParams(dimension_semantics=("arbitrary", "arbitrary"),
                                             vmem_limit_bytes=VMEM_LIMIT),
        name="proj_prompt",
    )(h2d, wn, wt, mm, mb, lng, lnb, *tabs_t)


def _count16(ref, nsteps, cand, strict=False):
    one, zero = jnp.ones(cand.shape, I16), jnp.zeros(cand.shape, I16)
    n_acc = 4

    def body(c, accs):
        off = pl.multiple_of(c * COUNT_CHUNK, COUNT_CHUNK)
        blk = ref[pl.ds(off, COUNT_CHUNK), :]
        accs = list(accs)
        for j in range(COUNT_CHUNK // PACK):
            kk = blk[j * PACK:(j + 1) * PACK, :]
            hit = (kk > cand) if strict else (kk >= cand)
            accs[j % n_acc] = accs[j % n_acc] + jnp.where(hit, one, zero)
        return tuple(accs)
    accs = lax.fori_loop(0, nsteps, body, (zero,) * n_acc)
    tot = (accs[0] + accs[1]) + (accs[2] + accs[3])
    return jnp.sum(tot.astype(F32), axis=0, keepdims=True)


def _search16(ref, nsteps, target):
    qw = ref.shape[1]

    def bit_body(i, prefix):
        cand = prefix + lax.shift_left(jnp.int32(1), 15 - i)
        cand16 = jnp.broadcast_to(cand, (PACK, qw)).astype(I16)
        return jnp.where(_count16(ref, nsteps, cand16) >= target, cand, prefix)
    return lax.fori_loop(0, 16, bit_body, jnp.full((1, qw), I16_MIN, I32))


def _kth_threshold(keys_ref, hi_ref, lo_ref, nch, nsteps, topk):
    qw = keys_ref.shape[1]
    p_hi = _search16(hi_ref, nsteps, float(topk))
    p16 = jnp.broadcast_to(p_hi, (PACK, qw)).astype(I16)
    room = topk - _count16(hi_ref, nsteps, p16, strict=True)

    def low_body(c, carry):
        off = pl.multiple_of(c * COUNT_CHUNK, COUNT_CHUNK)
        lowest = jnp.full((PACK, qw), I16_MIN, I16)
        his, los = hi_ref[pl.ds(off, COUNT_CHUNK), :], lo_ref[pl.ds(off, COUNT_CHUNK), :]
        lo_ref[pl.ds(off, COUNT_CHUNK), :] = jnp.concatenate(
            [jnp.where(his[j * PACK:(j + 1) * PACK, :] == p16, los[j * PACK:(j + 1) * PACK, :], lowest)
             for j in range(COUNT_CHUNK // PACK)], axis=0)
        return carry
    lax.fori_loop(0, nsteps, low_body, 0)
    p_lo = _search16(lo_ref, nsteps, room)
    thr = jnp.maximum(p_hi * 65536 + (p_lo + 2 ** 15), INT_MIN + 1)

    def count32(strict):
        def body(c, acc):
            off = pl.multiple_of(c * KEY_CHUNK, KEY_CHUNK)
            kk = keys_ref[pl.ds(off, KEY_CHUNK), :]
            hit = (kk > thr) if strict else (kk >= thr)
            return acc + jnp.sum(jnp.where(hit, 1.0, 0.0), axis=0, keepdims=True)
        return lax.fori_loop(0, nch, body, jnp.zeros((1, qw), F32))

    @pl.when(jnp.max(count32(False)) > topk)
    def _():
        spare = topk - count32(True)
        r_i = lax.broadcasted_iota(I32, (KEY_CHUNK, KEY_CHUNK), 0)
        c_i = lax.broadcasted_iota(I32, (KEY_CHUNK, KEY_CHUNK), 1)
        lower = jnp.where(c_i <= r_i, 1.0, 0.0).astype(BF16)

        def body(c, seen):
            off = pl.multiple_of(c * KEY_CHUNK, KEY_CHUNK)
            kk = keys_ref[pl.ds(off, KEY_CHUNK), :]
            tie = kk == thr
            rank = seen + _dot(lower, jnp.where(tie, 1.0, 0.0).astype(BF16))
            keys_ref[pl.ds(off, KEY_CHUNK), :] = jnp.where(tie & (rank > spare), thr - 1, kk)
            return rank[KEY_CHUNK - 1:KEY_CHUNK, :]
        lax.fori_loop(0, nch, body, jnp.zeros((1, qw), F32))

    return thr


def _prompt_attn_kernel(qit_ref, wit_ref, qt_ref, kib_ref, kb_ref, vt_ref, o_ref,
                        keys_ref, hi_ref, lo_ref, rhs_ref, qz_ref, acc_ref, kmax_ref, *, topk):
    qw = o_ref.shape[0]
    qb = pl.program_id(1)
    nch = (qb * qw + qw + KEY_CHUNK - 1) // KEY_CHUNK
    nsteps = (nch * KEY_CHUNK + COUNT_CHUNK - 1) // COUNT_CHUNK
    qpos = qb * qw + lax.broadcasted_iota(I32, (KEY_CHUNK, qw), 1)
    krow = lax.broadcasted_iota(I32, (KEY_CHUNK, qw), 0)

    @pl.when((pl.program_id(0) == 0) & (qb == 0))
    def _():
        rhs_ref[...] = jnp.zeros(rhs_ref.shape, BF16)
        qz_ref[...] = jnp.zeros(qz_ref.shape, BF16)
    slots = LANES // IDX_DIM
    for h in range(IDX_HEADS):
        r = (h % slots) * IDX_DIM
        rhs_ref[r:r + IDX_DIM, h * qw:(h + 1) * qw] = qit_ref[h * IDX_DIM:(h + 1) * IDX_DIM, :]
    for h in range(N_HEADS):
        r = (h // KV_GROUP) * HEAD_DIM
        qz_ref[r:r + HEAD_DIM, h * qw:(h + 1) * qw] = qt_ref[h * HEAD_DIM:(h + 1) * HEAD_DIM, :]
    wsc = wit_ref[...] * IDX_SCALE

    def score_chunk(c, causal_edge):
        off = pl.multiple_of(c * KEY_CHUNK, KEY_CHUNK)
        kc = kib_ref[pl.ds(off, KEY_CHUNK), :]
        acc = None
        for h in range(IDX_HEADS):
            s = _dot(kc, rhs_ref[:, h * qw:(h + 1) * qw])
            t = jnp.maximum(s, 0.0) * wsc[h:h + 1, :]
            acc = t if acc is None else acc + t
        key = _sortable(acc)
        if causal_edge:
            key = jnp.where(krow + off <= qpos, key, INT_MIN)
        keys_ref[pl.ds(off, KEY_CHUNK), :] = key
        hi_ref[pl.ds(off, KEY_CHUNK), :] = (key >> 16).astype(I16)
        lo_ref[pl.ds(off, KEY_CHUNK), :] = (key ^ 0x8000).astype(I16)

    def score_body(c, carry):
        score_chunk(c, False)
        return carry
    lax.fori_loop(0, nch - 1, score_body, 0)
    score_chunk(nch - 1, True)

    @pl.when(nch * KEY_CHUNK < nsteps * COUNT_CHUNK)
    def _():
        off = pl.multiple_of(nch * KEY_CHUNK, KEY_CHUNK)
        fill = jnp.full((KEY_CHUNK, qw), I16_MIN, I16)
        hi_ref[pl.ds(off, KEY_CHUNK), :] = fill
        lo_ref[pl.ds(off, KEY_CHUNK), :] = fill

    thr = _kth_threshold(keys_ref, hi_ref, lo_ref, nch, nsteps, topk)

    ones_rows = jnp.where(lax.broadcasted_iota(I32, (PACK, KEY_CHUNK), 0) == 0, 1.0, 0.0).astype(BF16)

    @pl.when(qb == 0)
    def _():
        kf = kb_ref[...].astype(F32)
        kmax_ref[0] = jnp.max(jnp.sum(kf * kf, axis=1, keepdims=True))
    shifts = []
    for h in range(N_HEADS):
        qf = qt_ref[h * HEAD_DIM:(h + 1) * HEAD_DIM, :].astype(F32)
        shifts.append(jnp.sqrt(jnp.sum(qf * qf, axis=0, keepdims=True) * kmax_ref[0]))

    def attn_chunk(c, shift_fn):
        off = pl.multiple_of(c * KEY_CHUNK, KEY_CHUNK)
        sel = keys_ref[pl.ds(off, KEY_CHUNK), :] >= thr
        kb = kb_ref[pl.ds(off, KEY_CHUNK), :]
        vt = vt_ref[:, pl.ds(off, KEY_CHUNK)].astype(BF16)
        for g in range(N_KV_HEADS):
            ps, alphas = [], []
            for j in range(KV_GROUP):
                h = g * KV_GROUP + j
                s = jnp.where(sel, _dot(kb, qz_ref[:, h * qw:(h + 1) * qw]), NEG)
                shift, alpha = shift_fn(h, s)
                alphas.append(alpha)
                ps.append(jnp.exp2(s - shift).astype(BF16))
            vg = jnp.concatenate([vt[g * HEAD_DIM:(g + 1) * HEAD_DIM, :], ones_rows], axis=0)
            pv = _dot(vg, jnp.concatenate(ps, axis=1))
            if alphas[0] is None:
                acc_ref[g] = acc_ref[g] + pv
            else:
                acc_ref[g] = acc_ref[g] * jnp.concatenate(alphas, axis=1) + pv

    acc_ref[...] = jnp.zeros(acc_ref.shape, F32)

    def fast_body(c, carry):
        attn_chunk(c, lambda h, s: (shifts[h], None))
        return carry
    lax.fori_loop(0, nch, fast_body, 0)

    denoms = [acc_ref[g][HEAD_DIM:HEAD_DIM + 1, :] for g in range(N_KV_HEADS)]
    denom_min = jnp.min(functools.reduce(jnp.minimum, denoms))

    @pl.when(jnp.logical_not(denom_min > DENOM_FLOOR))
    def _():
        acc_ref[...] = jnp.zeros(acc_ref.shape, F32)

        def online_body(c, ms):
            ms = list(ms)

            def running_max(h, s):
                m_new = jnp.maximum(ms[h], jnp.max(s, axis=0, keepdims=True))
                alpha = jnp.exp2(ms[h] - m_new)
                ms[h] = m_new
                return m_new, alpha
            attn_chunk(c, running_max)
            return tuple(ms)
        lax.fori_loop(0, nch, online_body, (jnp.full((1, qw), NEG, F32),) * N_HEADS)

    for p in range(N_HEADS // 2):
        halves = []
        for h in (2 * p, 2 * p + 1):
            g, j = h // KV_GROUP, h % KV_GROUP
            acc = acc_ref[g][:, j * qw:(j + 1) * qw]
            halves.append(acc[:HEAD_DIM, :] / acc[HEAD_DIM:HEAD_DIM + 1, :])
        o_ref[:, p * LANES:(p + 1) * LANES] = jnp.concatenate(halves, axis=0).T.astype(BF16)


def _prompt_attn_call(qit, wit, qt, kib, kb, vt, nb, seq, topk, qw):
    nq = seq // qw
    hd, idd = N_HEADS * HEAD_DIM, IDX_HEADS * IDX_DIM
    col = lambda b, i: (0, b * nq + i)
    kern = functools.partial(_prompt_attn_kernel, topk=topk)
    return pl.pallas_call(
        kern,
        grid=(nb, nq),
        in_specs=[pl.BlockSpec((idd, qw), col),
                  pl.BlockSpec((WI_ROWS, qw), col),
                  pl.BlockSpec((hd, qw), col),
                  pl.BlockSpec((seq, LANES), lambda b, i: (b, 0)),
                  pl.BlockSpec((seq, LANES), lambda b, i: (b, 0)),
                  pl.BlockSpec((None, LANES, seq), lambda b, i: (b, 0, 0))],
        out_specs=pl.BlockSpec((qw, hd), lambda b, i: (b * nq + i, 0)),
        out_shape=jax.ShapeDtypeStruct((nb * seq, hd), BF16),
        scratch_shapes=[pltpu.VMEM((seq, qw), I32),
                        pltpu.VMEM((seq, qw), I16),
                        pltpu.VMEM((seq, qw), I16),
                        pltpu.VMEM((LANES, IDX_HEADS * qw), BF16),
                        pltpu.VMEM((LANES, N_HEADS * qw), BF16),
                        pltpu.VMEM((N_KV_HEADS, HEAD_DIM + PACK, KV_GROUP * qw), F32),
                        pltpu.SMEM((1,), F32)],
        compiler_params=pltpu.CompilerParams(dimension_semantics=("arbitrary", "arbitrary"),
                                             vmem_limit_bytes=VMEM_LIMIT),
        name="prompt_attn",
    )(qit, wit, qt, kib, kb, vt)


def _mlp_kernel(h_ref, a_ref, b_ref, pe_ref, wo_ref, g1_ref, b1_ref, wf1_ref, wf2_ref, wg_ref, wp_ref,
                g2_ref, b2_ref, out_ref, *, alpha, ff_chunk):
    mix = _dot(a_ref[...], wo_ref[0:A_WIDTH, :]) + _dot(b_ref[...], wo_ref[A_WIDTH:2 * A_WIDTH, :])
    h1 = _layer_norm(alpha * h_ref[...] + mix, g1_ref[...], b1_ref[...])
    h1b = h1.astype(BF16)
    ff = None
    for c in range(D_FF // ff_chunk):
        hid = jnp.maximum(_dot(h1b, wf1_ref[:, c * ff_chunk:(c + 1) * ff_chunk]), 0.0)
        part = _dot((hid * hid).astype(BF16), wf2_ref[c * ff_chunk:(c + 1) * ff_chunk, :])
        ff = part if ff is None else ff + part
    ple = jax.nn.sigmoid(_dot(h1b, wg_ref[...])) * _dot(pe_ref[...].astype(BF16), wp_ref[...])
    out_ref[...] = _layer_norm(alpha * h1 + ff + ple, g2_ref[...], b2_ref[...])


def _mlp_call(h2d, a_out, b_out, pe, layer, consts, alpha, row_tile):
    n = h2d.shape[0]
    row = lambda i: (i, 0)
    kern = functools.partial(_mlp_kernel, alpha=alpha, ff_chunk=1024)
    return pl.pallas_call(
        kern,
        grid=(n // row_tile,),
        in_specs=[pl.BlockSpec((row_tile, D_MODEL), row),
                  pl.BlockSpec((row_tile, A_WIDTH), row),
                  pl.BlockSpec((row_tile, A_WIDTH), row),
                  pl.BlockSpec((None, row_tile, D_PLE), lambda i: (layer, i, 0))]
                 + [_layer_spec(c, layer) for c in consts],
        out_specs=pl.BlockSpec((row_tile, D_MODEL), row),
        out_shape=jax.ShapeDtypeStruct((n, D_MODEL), F32),
        compiler_params=pltpu.CompilerParams(dimension_semantics=("arbitrary",),
                                             vmem_limit_bytes=VMEM_LIMIT),
        name="mlp",
    )(h2d, a_out, b_out, pe, *consts)


def _head_masked_queries(qi_ref):
    lane_head = lax.broadcasted_iota(I32, (CHUNK, LANES), 1) >> 5
    heads_per_col = LANES // IDX_DIM
    out = []
    for h in range(IDX_HEADS):
        col = qi_ref[:, (h // heads_per_col) * LANES:(h // heads_per_col + 1) * LANES]
        out.append(jnp.where(lane_head == (h % heads_per_col), col, jnp.zeros_like(col)))
    return out


def _index_scores(qh, wsc, kc):
    acc = None
    for h in range(IDX_HEADS):
        s = _dot_nt(qh[h], kc)
        t = jnp.maximum(s, 0.0) * wsc[:, h:h + 1]
        acc = t if acc is None else acc + t
    return acc


PAGE_GROUP = 8


def _request_pages(pt_ref, layer, npg, streams):
    b, nreq = pl.program_id(0), pl.num_programs(0)

    def page_copy(stream, req, slot, j):
        cache, buf, sem = stream
        return pltpu.make_async_copy(cache.at[layer, pt_ref[req, j]], buf.at[slot, j], sem.at[slot])

    def start_all(req, slot):
        def body(i, carry):
            for prio in range(2):
                for stream in streams:
                    page_copy(stream, req, slot, 2 * i + prio).start(priority=prio)
            return carry
        lax.fori_loop(0, npg // 2, body, 0)

    @pl.when(b == 0)
    def _():
        start_all(0, 0)

    @pl.when(b + 1 < nreq)
    def _():
        start_all(b + 1, (b + 1) % 2)

    slot = b % 2

    def wait_body(j, carry):
        for stream in streams:
            page_copy(stream, b, slot, j).wait()
        return carry
    lax.fori_loop(0, npg, wait_body, 0)
    return slot


def _sample_idx_kernel(pt_ref, qs_ref, w_ref, cki_ref, out_ref, buf, sem, *, layer):
    npg = buf.shape[1]
    slot = _request_pages(pt_ref, layer, npg, [(cki_ref, buf, sem)])
    qs = qs_ref[0]
    w = jnp.concatenate([w_ref[0]] * PAGE_GROUP, axis=1)
    nt = out_ref.shape[1]
    width = PAGE_GROUP * LANES
    for c in range(npg // PAGE_GROUP):
        kp = jnp.concatenate([buf[slot, c * PAGE_GROUP + i] for i in range(PAGE_GROUP)], axis=1)
        t = jnp.maximum(_dot(qs, kp.astype(BF16)), 0.0) * w
        for q in range(nt):
            sc = jnp.sum(t[q * IDX_HEADS:(q + 1) * IDX_HEADS, :], axis=0, keepdims=True)
            out_ref[0, q:q + 1, c * width:(c + 1) * width] = _sortable(sc)


def _sample_idx_call(page_table, qs, wrep, cache_kidx, layer):
    db, npg = page_table.shape
    nt = qs.shape[1] // IDX_HEADS
    page = cache_kidx.shape[3]
    kern = functools.partial(_sample_idx_kernel, layer=layer)
    grid_spec = pltpu.PrefetchScalarGridSpec(
        num_scalar_prefetch=1,
        grid=(db,),
        in_specs=[pl.BlockSpec((1,) + qs.shape[1:], lambda b, pt: (b, 0, 0)),
                  pl.BlockSpec((1,) + wrep.shape[1:], lambda b, pt: (b, 0, 0)),
                  pl.BlockSpec(memory_space=pl.ANY)],
        out_specs=pl.BlockSpec((1, nt, npg * page), lambda b, pt: (b, 0, 0)),
        scratch_shapes=[pltpu.VMEM((2, npg, IDX_DIM, page), F32), pltpu.SemaphoreType.DMA((2,))],
    )
    return pl.pallas_call(
        kern, grid_spec=grid_spec,
        out_shape=jax.ShapeDtypeStruct((db, nt, npg * page), I32),
        compiler_params=pltpu.CompilerParams(dimension_semantics=("arbitrary",), vmem_limit_bytes=VMEM_LIMIT),
        name="sample_idx",
    )(page_table, qs, wrep, cache_kidx)


def _sample_select_kernel(kp_ref, qi_ref, wi_ref, ki_ref, bp_ref, bn_ref,
                          keys_ref, kt_ref, hi_ref, lo_ref, bias_ref, *, topk, nt):
    nreq, _, past = kp_ref.shape
    width = keys_ref.shape[1]
    for b in range(nreq):
        keys_ref[b * nt:(b + 1) * nt, 0:past] = kp_ref[b]
    qh = _head_masked_queries(qi_ref)
    wsc = wi_ref[...] * IDX_SCALE
    key = _sortable(_index_scores(qh, wsc, ki_ref[...].astype(BF16)))
    r_i = lax.broadcasted_iota(I32, (CHUNK, LANES), 0)
    c_i = lax.broadcasted_iota(I32, (CHUNK, LANES), 1)
    visible = ((r_i // nt) == (c_i // nt)) & ((c_i % nt) <= (r_i % nt))
    keys_ref[:, past:past + LANES] = jnp.where(visible, key, INT_MIN)
    if width > past + LANES:
        keys_ref[:, past + LANES:width] = jnp.full((CHUNK, width - past - LANES), INT_MIN, I32)

    for c in range(width // LANES):
        rows = slice(c * LANES, (c + 1) * LANES)
        blk = keys_ref[:, rows].T
        kt_ref[rows, :] = blk
        hi_ref[rows, :] = (blk >> 16).astype(I16)
        lo_ref[rows, :] = (blk ^ 0x8000).astype(I16)

    thr = _kth_threshold(kt_ref, hi_ref, lo_ref, width // KEY_CHUNK, width // COUNT_CHUNK, topk)

    for c in range(width // LANES):
        rows = slice(c * LANES, (c + 1) * LANES)
        bias_ref[:, rows] = jnp.where(kt_ref[rows, :] >= thr, 0.0, NEG).T
    for b in range(nreq):
        bp_ref[b] = bias_ref[b * nt:(b + 1) * nt, 0:past]
    bn_ref[...] = bias_ref[:, past:past + LANES]


def _sample_select_call(keys_past, qi, wi, ki4, topk, nt):
    nreq, _, past = keys_past.shape
    rows = nreq * nt
    step = max(KEY_CHUNK, COUNT_CHUNK)
    width = -(-(past + LANES) // step) * step
    kern = functools.partial(_sample_select_kernel, topk=topk, nt=nt)
    return pl.pallas_call(
        kern,
        out_shape=[jax.ShapeDtypeStruct((nreq, nt, past), F32), jax.ShapeDtypeStruct((rows, LANES), F32)],
        scratch_shapes=[pltpu.VMEM((rows, width), I32), pltpu.VMEM((width, rows), I32),
                        pltpu.VMEM((width, rows), I16), pltpu.VMEM((width, rows), I16),
                        pltpu.VMEM((rows, width), F32)],
        compiler_params=pltpu.CompilerParams(vmem_limit_bytes=VMEM_LIMIT),
        name="sample_select",
    )(keys_past, qi, wi, ki4)


def _sample_attn_kernel(pt_ref, qz_ref, bp_ref, bn_ref, kn_ref, vn_ref, ck_ref, cv_ref, o_ref,
                        kbuf, vbuf, ksem, vsem, s_ref, *, layer, nt):
    npg = kbuf.shape[1]
    slot = _request_pages(pt_ref, layer, npg, [(ck_ref, kbuf, ksem), (cv_ref, vbuf, vsem)])
    qz = qz_ref[0]
    rows = qz.shape[0]
    hpq = rows // nt
    width = PAGE_GROUP * LANES
    groups = npg // PAGE_GROUP
    past = npg * LANES

    def lane_blocks(x):
        return [x[:, i * LANES:(i + 1) * LANES] for i in range(x.shape[1] // LANES)]

    def masked(s, bias):
        return jnp.concatenate(
            [s[q * hpq:(q + 1) * hpq, :] + bias[q:q + 1, :] for q in range(nt)], axis=0)

    def pages_of(buf, c):
        return jnp.concatenate([buf[slot, c * PAGE_GROUP + i] for i in range(PAGE_GROUP)], axis=1).astype(BF16)

    bias_past = bp_ref[0]
    m_run = jnp.full((rows, LANES), NEG, F32)
    for c in range(groups):
        s = masked(_dot(qz, pages_of(kbuf, c)), bias_past[:, c * width:(c + 1) * width])
        s_ref[:, c * width:(c + 1) * width] = s
        m_run = functools.reduce(jnp.maximum, lane_blocks(s), m_run)
    s_new = masked(_dot_nt(qz, kn_ref[...].astype(BF16)), bn_ref[0])
    s_ref[:, past:past + LANES] = s_new
    m = jnp.max(jnp.maximum(m_run, s_new), axis=-1, keepdims=True)

    pr = jnp.exp(s_new - m)
    l_run = pr
    acc = _dot(pr.astype(BF16), vn_ref[...].astype(BF16))
    for c in range(groups):
        pr = jnp.exp(s_ref[:, c * width:(c + 1) * width] - m)
        l_run = functools.reduce(jnp.add, lane_blocks(pr), l_run)
        acc = acc + _dot_nt(pr.astype(BF16), pages_of(vbuf, c))
    o_ref[0] = acc / jnp.sum(l_run, axis=-1, keepdims=True)


def _sample_attn_call(page_table, qz, bias_past, bias_new, k_new, v_new, cache_k, cache_v, layer, nt):
    db, npg = page_table.shape
    page = cache_k.shape[3]
    rows = qz.shape[1]
    kern = functools.partial(_sample_attn_kernel, layer=layer, nt=nt)
    grid_spec = pltpu.PrefetchScalarGridSpec(
        num_scalar_prefetch=1,
        grid=(db,),
        in_specs=[pl.BlockSpec((1, rows, LANES), lambda b, pt: (b, 0, 0)),
                  pl.BlockSpec((1, nt, npg * page), lambda b, pt: (b, 0, 0)),
                  pl.BlockSpec((1, nt, LANES), lambda b, pt: (b, 0, 0)),
                  pl.BlockSpec(k_new.shape, lambda b, pt: (0, 0)),
                  pl.BlockSpec(v_new.shape, lambda b, pt: (0, 0)),
                  pl.BlockSpec(memory_space=pl.ANY),
                  pl.BlockSpec(memory_space=pl.ANY)],
        out_specs=pl.BlockSpec((1, rows, LANES), lambda b, pt: (b, 0, 0)),
        scratch_shapes=[pltpu.VMEM((2, npg, LANES, page), F32), pltpu.VMEM((2, npg, LANES, page), F32),
                        pltpu.SemaphoreType.DMA((2,)), pltpu.SemaphoreType.DMA((2,)),
                        pltpu.VMEM((rows, npg * page + LANES), F32)],
    )
    return pl.pallas_call(
        kern, grid_spec=grid_spec,
        out_shape=jax.ShapeDtypeStruct((db, rows, LANES), F32),
        compiler_params=pltpu.CompilerParams(dimension_semantics=("arbitrary",), vmem_limit_bytes=VMEM_LIMIT),
        name="sample_attn",
    )(page_table, qz, bias_past, bias_new, k_new, v_new, cache_k, cache_v)


def _swap_halves(w, head_dim):
    lead = w.shape[:-1]
    nh = w.shape[-1] // head_dim
    return w.reshape(lead + (nh, 2, head_dim // 2))[..., ::-1, :].reshape(lead + (nh * head_dim,))


def _projection_weights(w_in):
    sizes = (A_WIDTH, A_WIDTH, N_HEADS * HEAD_DIM, N_KV_HEADS * HEAD_DIM, N_KV_HEADS * HEAD_DIM,
             IDX_HEADS * IDX_DIM, IDX_DIM, IDX_HEADS)
    pts = [sum(sizes[:i + 1]) for i in range(len(sizes) - 1)]
    au, av, q, k, v, qi, ki, wi = jnp.split(w_in.astype(BF16), pts, axis=-1)
    rep = LANES // IDX_DIM
    q_s, k_s, qi_s, ki_s = (_swap_halves(q, HEAD_DIM), _swap_halves(k, HEAD_DIM),
                            _swap_halves(qi, IDX_DIM), _swap_halves(ki, IDX_DIM))
    ki4, ki4_s = jnp.tile(ki, (1, 1, rep)), jnp.tile(ki_s, (1, 1, rep))
    pad_cols = lambda w, n: jnp.pad(w, ((0, 0), (0, 0), (0, n - w.shape[-1])))
    ext = jnp.concatenate([au, av, q, k, v, qi, ki4, pad_cols(wi, LANES), q_s, k_s, qi_s, ki4_s], axis=-1)
    wn = jnp.concatenate([au, av], axis=-1)
    wt = jnp.swapaxes(jnp.concatenate([q, k, v, qi, ki, pad_cols(wi, WI_ROWS)], axis=-1), 1, 2)
    assert ext.shape[-1] == C_END and wn.shape[-1] == N_END and wt.shape[1] == T_END
    return ext, wn, wt


def _rope_tables(pos, head_dim):
    half = head_dim // 2
    inv = ROPE_THETA ** (-jnp.arange(half, dtype=F32) / half)
    ang = pos.astype(F32)[:, None] * inv[None, :]
    cos, sin = jnp.cos(ang), jnp.sin(ang)
    reps = LANES // head_dim
    return (jnp.tile(jnp.concatenate([cos, cos], axis=-1), (1, reps)),
            jnp.tile(jnp.concatenate([-sin, sin], axis=-1), (1, reps)))


def _pick(n, prefs):
    for p in prefs:
        if n % p == 0:
            return p
    return n


def kernel(x_prompt, x_sample, p_prompt, p_sample, cache_k, cache_v, cache_kidx, page_table, w_in, sgu_ln_g,
           sgu_ln_b, sgu_w, sgu_b, w_o, ln1_g, ln1_b, w_ff1, w_ff2, w_ple_gate, w_ple_proj, ln2_g, ln2_b):
    nb, seq, _ = x_prompt.shape
    db, nt, _ = x_sample.shape
    depth = w_in.shape[0]
    npg = page_table.shape[1]
    page = cache_k.shape[2]
    past = npg * page
    alpha = (2.0 * depth) ** 0.25
    topk_p = min(TOPK_MAX, seq // 4)
    topk_s = min(TOPK_MAX, (past + nt) // 4)
    n_p, n_s = nb * seq, db * nt
    assert n_s == CHUNK and seq % COUNT_CHUNK == 0 and page == LANES

    w_ext, w_n, w_t = _projection_weights(w_in)
    wo_b, wf1_b, wf2_b = w_o.astype(BF16), w_ff1.astype(BF16), w_ff2.astype(BF16)
    wg_b, wp_b = w_ple_gate.astype(BF16), w_ple_proj.astype(BF16)
    tril = jnp.tril(jnp.ones((CHUNK, CHUNK), F32))
    mm_p = (sgu_w * tril).astype(BF16)
    mb_p = jnp.repeat(jnp.swapaxes(sgu_b, 1, 2), A_GW, axis=2)
    r_i = jnp.arange(CHUNK)
    same_req = (r_i[:, None] // nt == r_i[None, :] // nt) & (r_i[None, :] % nt <= r_i[:, None] % nt)
    mm_s = jnp.where(same_req, sgu_w[:, :, r_i % nt][:, :, :, r_i % nt], 0.0).astype(BF16)
    mb_s = jnp.tile(jnp.repeat(jnp.swapaxes(sgu_b[:, :, :nt], 1, 2), A_GW, axis=2), (1, CHUNK // nt, 1))
    lng = sgu_ln_g.reshape(depth, 1, A_WIDTH)
    lnb = sgu_ln_b.reshape(depth, 1, A_WIDTH)
    mlp_w = (wo_b, ln1_g.reshape(depth, 1, D_MODEL), ln1_b.reshape(depth, 1, D_MODEL), wf1_b, wf2_b, wg_b, wp_b,
             ln2_g.reshape(depth, 1, D_MODEL), ln2_b.reshape(depth, 1, D_MODEL))
    pe_p = p_prompt.reshape(depth, n_p, D_PLE)
    pe_s = p_sample.reshape(depth, n_s, D_PLE)

    pos_p = jnp.arange(seq, dtype=I32)
    pos_s = past + (jnp.arange(n_s, dtype=I32) % nt)
    tabs_p = _rope_tables(pos_p, HEAD_DIM) + _rope_tables(pos_p, IDX_DIM)
    tabs_pt = tuple(t.T for t in tabs_p)
    tabs_s = _rope_tables(pos_s, HEAD_DIM) + _rope_tables(pos_s, IDX_DIM)

    cki = jnp.swapaxes(cache_kidx, 2, 3)
    ck = jnp.transpose(cache_k, (0, 1, 3, 4, 2)).reshape(cache_k.shape[:2] + (LANES, page))
    cv = jnp.transpose(cache_v, (0, 1, 3, 4, 2)).reshape(cache_v.shape[:2] + (LANES, page))

    rt_proj = _pick(seq, (512, 256, 128))
    rt_mlp = _pick(n_p, (512, 256, 128))
    qw_p = _pick(seq, (256, 128))
    assert npg % PAGE_GROUP == 0
    head_lo = (jnp.arange(N_HEADS) // KV_GROUP) * HEAD_DIM

    hp = x_prompt.reshape(n_p, D_MODEL)
    hs = x_sample.reshape(n_s, D_MODEL)
    outs = [[] for _ in range(8)]
    for i in range(depth):
        a_out, vn, qt, kb, kt, vt, qit, kib, kit, wit = _proj_prompt_call(
            hp, i, w_n, w_t, mm_p, mb_p, lng, lnb, tabs_pt, nb, seq, rt_proj)
        b_out = _prompt_attn_call(qit, wit, qt, kib, kb, vt, nb, seq, topk_p, qw_p)
        hp = _mlp_call(hp, a_out, b_out, pe_p, i, mlp_w, alpha, rt_mlp)
        to_tokens = lambda a: jnp.transpose(a.reshape(nb, N_KV_HEADS, HEAD_DIM, seq), (0, 3, 1, 2))
        outs[0].append(to_tokens(kt))
        outs[1].append(to_tokens(vt))
        outs[2].append(jnp.swapaxes(kit, 1, 2))
        outs[3].append(vn)

        a_out, vn, q, k, v, qi, ki4, wi = _proj_call(hs, i, w_ext, mm_s, mb_s, lng, lnb, tabs_s)
        qs = qi.reshape(db, nt * IDX_HEADS, IDX_DIM)
        wrep = jnp.broadcast_to((wi[:, :IDX_HEADS] * IDX_SCALE).reshape(db, nt * IDX_HEADS, 1),
                                (db, nt * IDX_HEADS, LANES))
        keys_past = _sample_idx_call(page_table, qs, wrep, cki, i)
        bias_past, bias_new = _sample_select_call(keys_past, qi, wi, ki4, topk_s, nt)
        q4 = q.reshape(db, nt, N_HEADS, 1, HEAD_DIM)
        half = (jnp.arange(N_KV_HEADS) * HEAD_DIM)[None, None, None, :, None]
        qz = jnp.where(half == head_lo[None, None, :, None, None], q4, jnp.zeros_like(q4))
        qz = qz.reshape(db, nt * N_HEADS, LANES)
        o = _sample_attn_call(page_table, qz, bias_past, bias_new.reshape(db, nt, LANES),
                              k, v, ck, cv, i, nt)
        o = o.reshape(db, nt, N_HEADS, N_KV_HEADS, HEAD_DIM)
        b_out = jnp.concatenate([o[:, :, g * KV_GROUP:(g + 1) * KV_GROUP, g, :] for g in range(N_KV_HEADS)], axis=2)
        b_out = b_out.reshape(n_s, N_HEADS * HEAD_DIM).astype(BF16)
        hs = _mlp_call(hs, a_out, b_out, pe_s, i, mlp_w, alpha, n_s)
        outs[4].append(k.reshape(db, nt, N_KV_HEADS, HEAD_DIM))
        outs[5].append(v.reshape(db, nt, N_KV_HEADS, HEAD_DIM))
        outs[6].append(ki4[:, :IDX_DIM].reshape(db, nt, IDX_DIM))
        outs[7].append(vn.reshape(db, nt, A_WIDTH))

    st = [jnp.stack(o) for o in outs]
    return (hp.reshape(nb, seq, D_MODEL), hs.reshape(db, nt, D_MODEL),
            st[0], st[1], st[2], st[3], st[4], st[5], st[6], st[7])
```

```python
import functools

import jax
import jax.numpy as jnp
from jax import lax
from jax.experimental import pallas as pl
from jax.experimental.pallas import tpu as pltpu

F32 = jnp.float32
BF16 = jnp.bfloat16
I32 = jnp.int32
I16 = jnp.int16

D_MODEL = 1024
CHUNK = 128
A_WIDTH = 512
A_GROUPS = 4
A_GW = A_WIDTH // A_GROUPS
HEAD_DIM = 64
N_HEADS = 8
N_KV_HEADS = 2
KV_GROUP = N_HEADS // N_KV_HEADS
IDX_HEADS = 8
IDX_DIM = 32
TOPK_MAX = 256
ROPE_THETA = 10000.0
D_FF = 4 * D_MODEL
D_PLE = 256
LN_EPS = 1e-5

LANES = 128
SUBLANES = 8
PACK = 16
VMEM_LIMIT = 56 * 1024 * 1024

INT_MIN = -(2 ** 31)
I16_MIN = -(2 ** 15)
NEG = -1e30
DENOM_FLOOR = 2.0 ** -100

C_AU, C_AV, C_Q, C_K, C_V, C_QI, C_KI, C_WI = 0, 512, 1024, 1536, 1664, 1792, 2048, 2176
C_QS, C_KS, C_QIS, C_KIS, C_END = 2304, 2816, 2944, 3200, 3328
N_AU, N_AV, N_END = 0, 512, 1024
T_Q, T_K, T_V, T_QI, T_KI, T_WI, T_END = 0, 512, 640, 768, 1024, 1056, 1072
WI_ROWS = T_END - T_WI
Q_SCALE_LOG2 = (HEAD_DIM ** -0.5) * 1.4426950408889634
IDX_SCALE = (IDX_DIM ** -0.5) * (IDX_HEADS ** -0.5)

KEY_CHUNK = 512
COUNT_CHUNK = 512

NT_DIMS = (((1,), (1,)), ((), ()))


def _dot(a, b):
    return jnp.dot(a, b, preferred_element_type=F32)


def _dot_nt(a, b):
    return lax.dot_general(a, b, NT_DIMS, preferred_element_type=F32)


def _layer_norm(x, g, b):
    mu = jnp.mean(x, axis=-1, keepdims=True)
    xc = x - mu
    var = jnp.mean(xc * xc, axis=-1, keepdims=True)
    return xc * lax.rsqrt(var + LN_EPS) * g + b


def _gelu(x):
    return 0.5 * x * (1.0 + lax.erf(x * (2.0 ** -0.5)))


def _sortable(x):
    bits = pltpu.bitcast(x, I32)
    return bits ^ ((bits >> 31) & 0x7FFFFFFF)


def _layer_spec(stacked, layer):
    shape = stacked.shape[1:]
    return pl.BlockSpec((None,) + shape, lambda *_: (layer,) + (0,) * len(shape),
                        pipeline_mode=pl.Buffered(1))


def _gated_mixer(au, av, g, lng_ref, lnb_ref, mm_ref, mb_ref, vn_ref, aout_ref):
    rows = au.shape[0]
    lo = g * A_GW
    vn = _layer_norm(av, lng_ref[:, lo:lo + A_GW], lnb_ref[:, lo:lo + A_GW])
    vn_ref[:, lo:lo + A_GW] = vn[rows - vn_ref.shape[0]:, :]
    vnb = vn.astype(BF16)
    for c in range(rows // CHUNK):
        r0 = c * CHUNK
        gate = _dot(mm_ref[g], vnb[r0:r0 + CHUNK, :]) + mb_ref[:, lo:lo + A_GW]
        aout_ref[r0:r0 + CHUNK, lo:lo + A_GW] = (au[r0:r0 + CHUNK, :] * gate).astype(BF16)


def _proj_kernel(h_ref, w_ref, mm_ref, mb_ref, lng_ref, lnb_ref, cq_ref, sq_ref, ci_ref, si_ref,
                 aout_ref, vn_ref, q_ref, k_ref, v_ref, qi_ref, ki_ref, wi_ref):
    hb = h_ref[...].astype(BF16)

    def proj(lo, width=LANES):
        return _dot(hb, w_ref[:, lo:lo + width])

    for g in range(A_GROUPS):
        lo = g * A_GW
        _gated_mixer(_gelu(proj(C_AU + lo)), _gelu(proj(C_AV + lo)), g,
                     lng_ref, lnb_ref, mm_ref, mb_ref, vn_ref, aout_ref)

    cq, sq = cq_ref[...], sq_ref[...]
    ci, si = ci_ref[...], si_ref[...]
    for j in range(N_HEADS * HEAD_DIM // LANES):
        z = proj(C_Q + j * LANES) * cq + proj(C_QS + j * LANES) * sq
        q_ref[:, j * LANES:(j + 1) * LANES] = (z * (HEAD_DIM ** -0.5)).astype(BF16)
    k_ref[...] = proj(C_K) * cq + proj(C_KS) * sq
    v_ref[...] = proj(C_V)
    for j in range(IDX_HEADS * IDX_DIM // LANES):
        z = proj(C_QI + j * LANES) * ci + proj(C_QIS + j * LANES) * si
        qi_ref[:, j * LANES:(j + 1) * LANES] = z.astype(BF16)
    ki_ref[...] = proj(C_KI) * ci + proj(C_KIS) * si
    wi_ref[...] = proj(C_WI)


def _proj_call(h2d, layer, w_ext, mm, mb, lng, lnb, tabs):
    n = h2d.shape[0]
    whole = lambda a: pl.BlockSpec(a.shape, lambda i: (0, 0))
    out_widths = [(A_WIDTH, BF16), (A_WIDTH, F32), (N_HEADS * HEAD_DIM, BF16), (LANES, F32), (LANES, F32),
                  (IDX_HEADS * IDX_DIM, BF16), (LANES, F32), (LANES, F32)]
    return pl.pallas_call(
        _proj_kernel,
        grid=(1,),
        in_specs=[whole(h2d)] + [_layer_spec(a, layer) for a in (w_ext, mm, mb, lng, lnb)]
                 + [whole(t) for t in tabs],
        out_specs=[pl.BlockSpec((n, w), lambda i: (0, 0)) for w, _ in out_widths],
        out_shape=[jax.ShapeDtypeStruct((n, w), dt) for w, dt in out_widths],
        compiler_params=pltpu.CompilerParams(dimension_semantics=("arbitrary",), vmem_limit_bytes=VMEM_LIMIT),
        name="proj",
    )(h2d, w_ext, mm, mb, lng, lnb, *tabs)


def _rope_t(x, cos_t, sin_t, head_dim):
    half = head_dim // 2
    parts = []
    for r in range(0, x.shape[0], head_dim):
        parts += [x[r + half:r + head_dim], x[r:r + half]]
    return x * cos_t + jnp.concatenate(parts, axis=0) * sin_t


def _proj_prompt_kernel(h_ref, wn_ref, wt_ref, mm_ref, mb_ref, lng_ref, lnb_ref,
                        cqt_ref, sqt_ref, cit_ref, sit_ref,
                        aout_ref, vn_ref, qt_ref, kb_ref, kt_ref, vt_ref, qit_ref, kib_ref, kit_ref, wit_ref):
    hb = h_ref[...].astype(BF16)
    wide = 2 * LANES

    zt = _dot_nt(wt_ref[...], hb)
    cqt, sqt = cqt_ref[...], sqt_ref[...]
    cit, sit = cit_ref[...], sit_ref[...]
    for j in range(N_HEADS * HEAD_DIM // LANES):
        r = j * LANES
        q = _rope_t(zt[T_Q + r:T_Q + r + LANES], cqt, sqt, HEAD_DIM)
        qt_ref[r:r + LANES, :] = (q * Q_SCALE_LOG2).astype(BF16)
    kt = _rope_t(zt[T_K:T_K + LANES], cqt, sqt, HEAD_DIM)
    kt_ref[...] = kt
    vt_ref[...] = zt[T_V:T_V + LANES]
    for j in range(IDX_HEADS * IDX_DIM // LANES):
        r = j * LANES
        qit_ref[r:r + LANES, :] = _rope_t(zt[T_QI + r:T_QI + r + LANES], cit, sit, IDX_DIM).astype(BF16)
    kit = _rope_t(zt[T_KI:T_KI + IDX_DIM], cit[:IDX_DIM], sit[:IDX_DIM], IDX_DIM)
    kit_ref[...] = kit
    wit_ref[...] = zt[T_WI:T_END]

    kb_ref[...] = kt.T.astype(BF16)
    kib_ref[...] = jnp.concatenate([kit] * (LANES // IDX_DIM), axis=0).T.astype(BF16)

    for gp in range(A_GROUPS // 2):
        au2 = _gelu(_dot(hb, wn_ref[:, N_AU + gp * wide:N_AU + (gp + 1) * wide]))
        av2 = _gelu(_dot(hb, wn_ref[:, N_AV + gp * wide:N_AV + (gp + 1) * wide]))
        for e in range(2):
            _gated_mixer(au2[:, e * A_GW:(e + 1) * A_GW], av2[:, e * A_GW:(e + 1) * A_GW], 2 * gp + e,
                         lng_ref, lnb_ref, mm_ref, mb_ref, vn_ref, aout_ref)


def _proj_prompt_call(h2d, layer, wn, wt, mm, mb, lng, lnb, tabs_t, nb, seq, row_tile):
    n = nb * seq
    ns = seq // row_tile
    row_map = lambda b, s: (b * ns + s, 0)
    col_map = lambda b, s: (0, b * ns + s)
    bcol_map = lambda b, s: (b, 0, s)
    hd, idd = N_HEADS * HEAD_DIM, IDX_HEADS * IDX_DIM
    out_specs = [pl.BlockSpec((row_tile, A_WIDTH), row_map),
                 pl.BlockSpec((None, CHUNK, A_WIDTH), lambda b, s: (b, 0, 0)),
                 pl.BlockSpec((hd, row_tile), col_map),
                 pl.BlockSpec((row_tile, LANES), row_map),
                 pl.BlockSpec((None, LANES, row_tile), bcol_map),
                 pl.BlockSpec((None, LANES, row_tile), bcol_map),
                 pl.BlockSpec((idd, row_tile), col_map),
                 pl.BlockSpec((row_tile, LANES), row_map),
                 pl.BlockSpec((None, IDX_DIM, row_tile), bcol_map),
                 pl.BlockSpec((WI_ROWS, row_tile), col_map)]
    out_shape = [jax.ShapeDtypeStruct((n, A_WIDTH), BF16), jax.ShapeDtypeStruct((nb, CHUNK, A_WIDTH), F32),
                 jax.ShapeDtypeStruct((hd, n), BF16),
                 jax.ShapeDtypeStruct((n, LANES), BF16),
                 jax.ShapeDtypeStruct((nb, LANES, seq), F32),
                 jax.ShapeDtypeStruct((nb, LANES, seq), F32),
                 jax.ShapeDtypeStruct((idd, n), BF16),
                 jax.ShapeDtypeStruct((n, LANES), BF16),
                 jax.ShapeDtypeStruct((nb, IDX_DIM, seq), F32),
                 jax.ShapeDtypeStruct((WI_ROWS, n), F32)]
    return pl.pallas_call(
        _proj_prompt_kernel,
        grid=(nb, ns),
        in_specs=[pl.BlockSpec((row_tile, D_MODEL), row_map)]
                 + [_layer_spec(a, layer) for a in (wn, wt, mm, mb, lng, lnb)]
                 + [pl.BlockSpec((LANES, row_tile), lambda b, s: (0, s))] * 4,
        out_specs=out_specs,
        out_shape=out_shape,
        compiler_params=pltpu.CompilerParams(dimension_semantics=("arbitrary", "arbitrary"),
                                             vmem_limit_bytes=VMEM_LIMIT),
        name="proj_prompt",
    )(h2d, wn, wt, mm, mb, lng, lnb, *tabs_t)


def _count16(ref, nsteps, cand, strict=False):
    one, zero = jnp.ones(cand.shape, I16), jnp.zeros(cand.shape, I16)
    n_acc = 4

    def body(c, accs):
        off = pl.multiple_of(c * COUNT_CHUNK, COUNT_CHUNK)
        blk = ref[pl.ds(off, COUNT_CHUNK), :]
        accs = list(accs)
        for j in range(COUNT_CHUNK // PACK):
            kk = blk[j * PACK:(j + 1) * PACK, :]
            hit = (kk > cand) if strict else (kk >= cand)
            accs[j % n_acc] = accs[j % n_acc] + jnp.where(hit, one, zero)
        return tuple(accs)
    accs = lax.fori_loop(0, nsteps, body, (zero,) * n_acc)
    tot = (accs[0] + accs[1]) + (accs[2] + accs[3])
    return jnp.sum(tot.astype(F32), axis=0, keepdims=True)


def _search16(ref, nsteps, target):
    qw = ref.shape[1]

    def bit_body(i, prefix):
        cand = prefix + lax.shift_left(jnp.int32(1), 15 - i)
        cand16 = jnp.broadcast_to(cand, (PACK, qw)).astype(I16)
        return jnp.where(_count16(ref, nsteps, cand16) >= target, cand, prefix)
    return lax.fori_loop(0, 16, bit_body, jnp.full((1, qw), I16_MIN, I32))


def _kth_threshold(keys_ref, hi_ref, lo_ref, nch, nsteps, topk):
    qw = keys_ref.shape[1]
    p_hi = _search16(hi_ref, nsteps, float(topk))
    p16 = jnp.broadcast_to(p_hi, (PACK, qw)).astype(I16)
    room = topk - _count16(hi_ref, nsteps, p16, strict=True)

    def low_body(c, carry):
        off = pl.multiple_of(c * COUNT_CHUNK, COUNT_CHUNK)
        lowest = jnp.full((PACK, qw), I16_MIN, I16)
        his, los = hi_ref[pl.ds(off, COUNT_CHUNK), :], lo_ref[pl.ds(off, COUNT_CHUNK), :]
        lo_ref[pl.ds(off, COUNT_CHUNK), :] = jnp.concatenate(
            [jnp.where(his[j * PACK:(j + 1) * PACK, :] == p16, los[j * PACK:(j + 1) * PACK, :], lowest)
             for j in range(COUNT_CHUNK // PACK)], axis=0)
        return carry
    lax.fori_loop(0, nsteps, low_body, 0)
    p_lo = _search16(lo_ref, nsteps, room)
    thr = jnp.maximum(p_hi * 65536 + (p_lo + 2 ** 15), INT_MIN + 1)

    def count32(strict):
        sub = SUBLANES

        def body(c, acc):
            off = pl.multiple_of(c * KEY_CHUNK, KEY_CHUNK)
            kk = keys_ref[pl.ds(off, KEY_CHUNK), :]
            ones = jnp.where((kk > thr) if strict else (kk >= thr), 1.0, 0.0)
            parts = [ones[j * sub:(j + 1) * sub, :] for j in range(KEY_CHUNK // sub)]
            while len(parts) > 1:
                parts = [a + b for a, b in zip(parts[0::2], parts[1::2])]
            return acc + parts[0]
        acc = lax.fori_loop(0, nch, body, jnp.zeros((sub, qw), F32))
        return jnp.sum(acc, axis=0, keepdims=True)

    @pl.when(jnp.max(count32(False)) > topk)
    def _():
        spare = topk - count32(True)
        r_i = lax.broadcasted_iota(I32, (KEY_CHUNK, KEY_CHUNK), 0)
        c_i = lax.broadcasted_iota(I32, (KEY_CHUNK, KEY_CHUNK), 1)
        lower = jnp.where(c_i <= r_i, 1.0, 0.0).astype(BF16)

        def body(c, seen):
            off = pl.multiple_of(c * KEY_CHUNK, KEY_CHUNK)
            kk = keys_ref[pl.ds(off, KEY_CHUNK), :]
            tie = kk == thr
            rank = seen + _dot(lower, jnp.where(tie, 1.0, 0.0).astype(BF16))
            keys_ref[pl.ds(off, KEY_CHUNK), :] = jnp.where(tie & (rank > spare), thr - 1, kk)
            return rank[KEY_CHUNK - 1:KEY_CHUNK, :]
        lax.fori_loop(0, nch, body, jnp.zeros((1, qw), F32))

    return thr


def _prompt_attn_kernel(qit_ref, wit_ref, qt_ref, kib_ref, kb_ref, vt_ref, o_ref,
                        keys_ref, hi_ref, lo_ref, rhs_ref, qz_ref, acc_ref, kmax_ref, *, topk):
    qw = o_ref.shape[0]
    qb = pl.program_id(1)
    nch = (qb * qw + qw + KEY_CHUNK - 1) // KEY_CHUNK
    nsteps = (nch * KEY_CHUNK + COUNT_CHUNK - 1) // COUNT_CHUNK
    qpos = qb * qw + lax.broadcasted_iota(I32, (KEY_CHUNK, qw), 1)
    krow = lax.broadcasted_iota(I32, (KEY_CHUNK, qw), 0)

    @pl.when((pl.program_id(0) == 0) & (qb == 0))
    def _():
        rhs_ref[...] = jnp.zeros(rhs_ref.shape, BF16)
        qz_ref[...] = jnp.zeros(qz_ref.shape, BF16)
    slots = LANES // IDX_DIM
    for h in range(IDX_HEADS):
        r = (h % slots) * IDX_DIM
        rhs_ref[r:r + IDX_DIM, h * qw:(h + 1) * qw] = qit_ref[h * IDX_DIM:(h + 1) * IDX_DIM, :]
    for h in range(N_HEADS):
        r = (h // KV_GROUP) * HEAD_DIM
        qz_ref[r:r + HEAD_DIM, h * qw:(h + 1) * qw] = qt_ref[h * HEAD_DIM:(h + 1) * HEAD_DIM, :]
    wsc = wit_ref[...] * IDX_SCALE

    def score_chunk(c, causal_edge):
        off = pl.multiple_of(c * KEY_CHUNK, KEY_CHUNK)
        kc = kib_ref[pl.ds(off, KEY_CHUNK), :]
        acc = None
        for h in range(IDX_HEADS):
            s = _dot(kc, rhs_ref[:, h * qw:(h + 1) * qw])
            t = jnp.maximum(s, 0.0) * wsc[h:h + 1, :]
            acc = t if acc is None else acc + t
        key = _sortable(acc)
        if causal_edge:
            key = jnp.where(krow + off <= qpos, key, INT_MIN)
        keys_ref[pl.ds(off, KEY_CHUNK), :] = key
        hi_ref[pl.ds(off, KEY_CHUNK), :] = (key >> 16).astype(I16)
        lo_ref[pl.ds(off, KEY_CHUNK), :] = (key ^ 0x8000).astype(I16)

    def score_body(c, carry):
        score_chunk(c, False)
        return carry
    lax.fori_loop(0, nch - 1, score_body, 0)
    score_chunk(nch - 1, True)

    @pl.when(nch * KEY_CHUNK < nsteps * COUNT_CHUNK)
    def _():
        off = pl.multiple_of(nch * KEY_CHUNK, KEY_CHUNK)
        fill = jnp.full((KEY_CHUNK, qw), I16_MIN, I16)
        hi_ref[pl.ds(off, KEY_CHUNK), :] = fill
        lo_ref[pl.ds(off, KEY_CHUNK), :] = fill

    thr = _kth_threshold(keys_ref, hi_ref, lo_ref, nch, nsteps, topk)

    ones_rows = jnp.where(lax.broadcasted_iota(I32, (PACK, KEY_CHUNK), 0) == 0, 1.0, 0.0).astype(BF16)

    @pl.when(qb == 0)
    def _():
        kf = kb_ref[...].astype(F32)
        kmax_ref[0] = jnp.max(jnp.sum(kf * kf, axis=1, keepdims=True))
    shifts = []
    for h in range(N_HEADS):
        qf = qt_ref[h * HEAD_DIM:(h + 1) * HEAD_DIM, :].astype(F32)
        shifts.append(jnp.sqrt(jnp.sum(qf * qf, axis=0, keepdims=True) * kmax_ref[0]))

    def attn_chunk(c, shift_fn):
        off = pl.multiple_of(c * KEY_CHUNK, KEY_CHUNK)
        sel = keys_ref[pl.ds(off, KEY_CHUNK), :] >= thr
        kb = kb_ref[pl.ds(off, KEY_CHUNK), :]
        vt = vt_ref[:, pl.ds(off, KEY_CHUNK)].astype(BF16)
        for g in range(N_KV_HEADS):
            ps, alphas = [], []
            for j in range(KV_GROUP):
                h = g * KV_GROUP + j
                s = jnp.where(sel, _dot(kb, qz_ref[:, h * qw:(h + 1) * qw]), NEG)
                shift, alpha = shift_fn(h, s)
                alphas.append(alpha)
                ps.append(jnp.exp2(s - shift).astype(BF16))
            vg = jnp.concatenate([vt[g * HEAD_DIM:(g + 1) * HEAD_DIM, :], ones_rows], axis=0)
            pv = _dot(vg, jnp.concatenate(ps, axis=1))
            if alphas[0] is None:
                acc_ref[g] = acc_ref[g] + pv
            else:
                acc_ref[g] = acc_ref[g] * jnp.concatenate(alphas, axis=1) + pv

    acc_ref[...] = jnp.zeros(acc_ref.shape, F32)

    def fast_body(c, carry):
        attn_chunk(c, lambda h, s: (shifts[h], None))
        return carry
    lax.fori_loop(0, nch, fast_body, 0)

    denoms = [acc_ref[g][HEAD_DIM:HEAD_DIM + 1, :] for g in range(N_KV_HEADS)]
    denom_min = jnp.min(functools.reduce(jnp.minimum, denoms))

    @pl.when(jnp.logical_not(denom_min > DENOM_FLOOR))
    def _():
        acc_ref[...] = jnp.zeros(acc_ref.shape, F32)

        def online_body(c, ms):
            ms = list(ms)

            def running_max(h, s):
                m_new = jnp.maximum(ms[h], jnp.max(s, axis=0, keepdims=True))
                alpha = jnp.exp2(ms[h] - m_new)
                ms[h] = m_new
                return m_new, alpha
            attn_chunk(c, running_max)
            return tuple(ms)
        lax.fori_loop(0, nch, online_body, (jnp.full((1, qw), NEG, F32),) * N_HEADS)

    for p in range(N_HEADS // 2):
        halves = []
        for h in (2 * p, 2 * p + 1):
            g, j = h // KV_GROUP, h % KV_GROUP
            acc = acc_ref[g][:, j * qw:(j + 1) * qw]
            halves.append(acc[:HEAD_DIM, :] / acc[HEAD_DIM:HEAD_DIM + 1, :])
        o_ref[:, p * LANES:(p + 1) * LANES] = jnp.concatenate(halves, axis=0).T.astype(BF16)


def _prompt_attn_call(qit, wit, qt, kib, kb, vt, nb, seq, topk, qw):
    nq = seq // qw
    hd, idd = N_HEADS * HEAD_DIM, IDX_HEADS * IDX_DIM
    col = lambda b, i: (0, b * nq + i)
    kern = functools.partial(_prompt_attn_kernel, topk=topk)
    return pl.pallas_call(
        kern,
        grid=(nb, nq),
        in_specs=[pl.BlockSpec((idd, qw), col),
                  pl.BlockSpec((WI_ROWS, qw), col),
                  pl.BlockSpec((hd, qw), col),
                  pl.BlockSpec((seq, LANES), lambda b, i: (b, 0)),
                  pl.BlockSpec((seq, LANES), lambda b, i: (b, 0)),
                  pl.BlockSpec((None, LANES, seq), lambda b, i: (b, 0, 0))],
        out_specs=pl.BlockSpec((qw, hd), lambda b, i: (b * nq + i, 0)),
        out_shape=jax.ShapeDtypeStruct((nb * seq, hd), BF16),
        scratch_shapes=[pltpu.VMEM((seq, qw), I32),
                        pltpu.VMEM((seq, qw), I16),
                        pltpu.VMEM((seq, qw), I16),
                        pltpu.VMEM((LANES, IDX_HEADS * qw), BF16),
                        pltpu.VMEM((LANES, N_HEADS * qw), BF16),
                        pltpu.VMEM((N_KV_HEADS, HEAD_DIM + PACK, KV_GROUP * qw), F32),
                        pltpu.SMEM((1,), F32)],
        compiler_params=pltpu.CompilerParams(dimension_semantics=("arbitrary", "arbitrary"),
                                             vmem_limit_bytes=VMEM_LIMIT),
        name="prompt_attn",
    )(qit, wit, qt, kib, kb, vt)


def _mlp_kernel(h_ref, a_ref, b_ref, pe_ref, wo_ref, g1_ref, b1_ref, wf1_ref, wf2_ref, wg_ref, wp_ref,
                g2_ref, b2_ref, out_ref, *, alpha, ff_chunk):
    mix = _dot(a_ref[...], wo_ref[0:A_WIDTH, :]) + _dot(b_ref[...], wo_ref[A_WIDTH:2 * A_WIDTH, :])
    h1 = _layer_norm(alpha * h_ref[...] + mix, g1_ref[...], b1_ref[...])
    h1b = h1.astype(BF16)
    ff = None
    for c in range(D_FF // ff_chunk):
        hid = jnp.maximum(_dot(h1b, wf1_ref[:, c * ff_chunk:(c + 1) * ff_chunk]), 0.0)
        part = _dot((hid * hid).astype(BF16), wf2_ref[c * ff_chunk:(c + 1) * ff_chunk, :])
        ff = part if ff is None else ff + part
    ple = jax.nn.sigmoid(_dot(h1b, wg_ref[...])) * _dot(pe_ref[...].astype(BF16), wp_ref[...])
    out_ref[...] = _layer_norm(alpha * h1 + ff + ple, g2_ref[...], b2_ref[...])


def _mlp_call(h2d, a_out, b_out, pe, layer, consts, alpha, row_tile):
    n = h2d.shape[0]
    row = lambda i: (i, 0)
    kern = functools.partial(_mlp_kernel, alpha=alpha, ff_chunk=1024)
    return pl.pallas_call(
        kern,
        grid=(n // row_tile,),
        in_specs=[pl.BlockSpec((row_tile, D_MODEL), row),
                  pl.BlockSpec((row_tile, A_WIDTH), row),
                  pl.BlockSpec((row_tile, A_WIDTH), row),
                  pl.BlockSpec((None, row_tile, D_PLE), lambda i: (layer, i, 0))]
                 + [_layer_spec(c, layer) for c in consts],
        out_specs=pl.BlockSpec((row_tile, D_MODEL), row),
        out_shape=jax.ShapeDtypeStruct((n, D_MODEL), F32),
        compiler_params=pltpu.CompilerParams(dimension_semantics=("arbitrary",),
                                             vmem_limit_bytes=VMEM_LIMIT),
        name="mlp",
    )(h2d, a_out, b_out, pe, *consts)


def _head_masked_queries(qi_ref):
    lane_head = lax.broadcasted_iota(I32, (CHUNK, LANES), 1) >> 5
    heads_per_col = LANES // IDX_DIM
    out = []
    for h in range(IDX_HEADS):
        col = qi_ref[:, (h // heads_per_col) * LANES:(h // heads_per_col + 1) * LANES]
        out.append(jnp.where(lane_head == (h % heads_per_col), col, jnp.zeros_like(col)))
    return out


def _index_scores(qh, wsc, kc):
    acc = None
    for h in range(IDX_HEADS):
        s = _dot_nt(qh[h], kc)
        t = jnp.maximum(s, 0.0) * wsc[:, h:h + 1]
        acc = t if acc is None else acc + t
    return acc


PAGE_GROUP = 8


def _request_pages(pt_ref, layer, npg, streams):
    b, nreq = pl.program_id(0), pl.num_programs(0)

    def page_copy(stream, req, slot, j):
        cache, buf, sem = stream
        return pltpu.make_async_copy(cache.at[layer, pt_ref[req, j]], buf.at[slot, j], sem.at[slot])

    def start_all(req, slot):
        def body(i, carry):
            for prio in range(2):
                for stream in streams:
                    page_copy(stream, req, slot, 2 * i + prio).start(priority=prio)
            return carry
        lax.fori_loop(0, npg // 2, body, 0)

    @pl.when(b == 0)
    def _():
        start_all(0, 0)

    @pl.when(b + 1 < nreq)
    def _():
        start_all(b + 1, (b + 1) % 2)

    slot = b % 2

    def wait_body(j, carry):
        for stream in streams:
            page_copy(stream, b, slot, j).wait()
        return carry
    lax.fori_loop(0, npg, wait_body, 0)
    return slot


def _sample_idx_kernel(pt_ref, qs_ref, w_ref, cki_ref, out_ref, buf, sem, *, layer):
    npg = buf.shape[1]
    slot = _request_pages(pt_ref, layer, npg, [(cki_ref, buf, sem)])
    qs = qs_ref[0]
    w = jnp.concatenate([w_ref[0]] * PAGE_GROUP, axis=1)
    nt = out_ref.shape[1]
    width = PAGE_GROUP * LANES
    for c in range(npg // PAGE_GROUP):
        kp = jnp.concatenate([buf[slot, c * PAGE_GROUP + i] for i in range(PAGE_GROUP)], axis=1)
        t = jnp.maximum(_dot(qs, kp.astype(BF16)), 0.0) * w
        for q in range(nt):
            sc = jnp.sum(t[q * IDX_HEADS:(q + 1) * IDX_HEADS, :], axis=0, keepdims=True)
            out_ref[0, q:q + 1, c * width:(c + 1) * width] = _sortable(sc)


def _sample_idx_call(page_table, qs, wrep, cache_kidx, layer):
    db, npg = page_table.shape
    nt = qs.shape[1] // IDX_HEADS
    page = cache_kidx.shape[3]
    kern = functools.partial(_sample_idx_kernel, layer=layer)
    grid_spec = pltpu.PrefetchScalarGridSpec(
        num_scalar_prefetch=1,
        grid=(db,),
        in_specs=[pl.BlockSpec((1,) + qs.shape[1:], lambda b, pt: (b, 0, 0)),
                  pl.BlockSpec((1,) + wrep.shape[1:], lambda b, pt: (b, 0, 0)),
                  pl.BlockSpec(memory_space=pl.ANY)],
        out_specs=pl.BlockSpec((1, nt, npg * page), lambda b, pt: (b, 0, 0)),
        scratch_shapes=[pltpu.VMEM((2, npg, IDX_DIM, page), F32), pltpu.SemaphoreType.DMA((2,))],
    )
    return pl.pallas_call(
        kern, grid_spec=grid_spec,
        out_shape=jax.ShapeDtypeStruct((db, nt, npg * page), I32),
        compiler_params=pltpu.CompilerParams(dimension_semantics=("arbitrary",), vmem_limit_bytes=VMEM_LIMIT),
        name="sample_idx",
    )(page_table, qs, wrep, cache_kidx)


def _sample_select_kernel(kp_ref, qi_ref, wi_ref, ki_ref, bp_ref, bn_ref,
                          keys_ref, kt_ref, hi_ref, lo_ref, bias_ref, *, topk, nt):
    nreq, _, past = kp_ref.shape
    width = keys_ref.shape[1]
    for b in range(nreq):
        keys_ref[b * nt:(b + 1) * nt, 0:past] = kp_ref[b]
    qh = _head_masked_queries(qi_ref)
    wsc = wi_ref[...] * IDX_SCALE
    key = _sortable(_index_scores(qh, wsc, ki_ref[...].astype(BF16)))
    r_i = lax.broadcasted_iota(I32, (CHUNK, LANES), 0)
    c_i = lax.broadcasted_iota(I32, (CHUNK, LANES), 1)
    visible = ((r_i // nt) == (c_i // nt)) & ((c_i % nt) <= (r_i % nt))
    keys_ref[:, past:past + LANES] = jnp.where(visible, key, INT_MIN)
    if width > past + LANES:
        keys_ref[:, past + LANES:width] = jnp.full((CHUNK, width - past - LANES), INT_MIN, I32)

    for c in range(width // LANES):
        rows = slice(c * LANES, (c + 1) * LANES)
        blk = keys_ref[:, rows].T
        kt_ref[rows, :] = blk
        hi_ref[rows, :] = (blk >> 16).astype(I16)
        lo_ref[rows, :] = (blk ^ 0x8000).astype(I16)

    thr = _kth_threshold(kt_ref, hi_ref, lo_ref, width // KEY_CHUNK, width // COUNT_CHUNK, topk)

    for c in range(width // LANES):
        rows = slice(c * LANES, (c + 1) * LANES)
        bias_ref[:, rows] = jnp.where(kt_ref[rows, :] >= thr, 0.0, NEG).T
    for b in range(nreq):
        bp_ref[b] = bias_ref[b * nt:(b + 1) * nt, 0:past]
    bn_ref[...] = bias_ref[:, past:past + LANES]


def _sample_select_call(keys_past, qi, wi, ki4, topk, nt):
    nreq, _, past = keys_past.shape
    rows = nreq * nt
    step = max(KEY_CHUNK, COUNT_CHUNK)
    width = -(-(past + LANES) // step) * step
    kern = functools.partial(_sample_select_kernel, topk=topk, nt=nt)
    return pl.pallas_call(
        kern,
        out_shape=[jax.ShapeDtypeStruct((nreq, nt, past), F32), jax.ShapeDtypeStruct((rows, LANES), F32)],
        scratch_shapes=[pltpu.VMEM((rows, width), I32), pltpu.VMEM((width, rows), I32),
                        pltpu.VMEM((width, rows), I16), pltpu.VMEM((width, rows), I16),
                        pltpu.VMEM((rows, width), F32)],
        compiler_params=pltpu.CompilerParams(vmem_limit_bytes=VMEM_LIMIT),
        name="sample_select",
    )(keys_past, qi, wi, ki4)


def _sample_attn_kernel(pt_ref, qz_ref, bp_ref, bn_ref, kn_ref, vn_ref, ck_ref, cv_ref, o_ref,
                        kbuf, vbuf, ksem, vsem, s_ref, *, layer, nt):
    npg = kbuf.shape[1]
    slot = _request_pages(pt_ref, layer, npg, [(ck_ref, kbuf, ksem), (cv_ref, vbuf, vsem)])
    qz = qz_ref[0]
    rows = qz.shape[0]
    hpq = rows // nt
    width = PAGE_GROUP * LANES
    groups = npg // PAGE_GROUP
    past = npg * LANES

    def lane_blocks(x):
        return [x[:, i * LANES:(i + 1) * LANES] for i in range(x.shape[1] // LANES)]

    def masked(s, bias):
        return jnp.concatenate(
            [s[q * hpq:(q + 1) * hpq, :] + bias[q:q + 1, :] for q in range(nt)], axis=0)

    def pages_of(buf, c):
        return jnp.concatenate([buf[slot, c * PAGE_GROUP + i] for i in range(PAGE_GROUP)], axis=1).astype(BF16)

    bias_past = bp_ref[0]
    m_run = jnp.full((rows, LANES), NEG, F32)
    for c in range(groups):
        s = masked(_dot(qz, pages_of(kbuf, c)), bias_past[:, c * width:(c + 1) * width])
        s_ref[:, c * width:(c + 1) * width] = s
        m_run = functools.reduce(jnp.maximum, lane_blocks(s), m_run)
    s_new = masked(_dot_nt(qz, kn_ref[...].astype(BF16)), bn_ref[0])
    s_ref[:, past:past + LANES] = s_new
    m = jnp.max(jnp.maximum(m_run, s_new), axis=-1, keepdims=True)

    pr = jnp.exp(s_new - m)
    l_run = pr
    acc = _dot(pr.astype(BF16), vn_ref[...].astype(BF16))
    for c in range(groups):
        pr = jnp.exp(s_ref[:, c * width:(c + 1) * width] - m)
        l_run = functools.reduce(jnp.add, lane_blocks(pr), l_run)
        acc = acc + _dot_nt(pr.astype(BF16), pages_of(vbuf, c))
    o_ref[0] = acc / jnp.sum(l_run, axis=-1, keepdims=True)


def _sample_attn_call(page_table, qz, bias_past, bias_new, k_new, v_new, cache_k, cache_v, layer, nt):
    db, npg = page_table.shape
    page = cache_k.shape[3]
    rows = qz.shape[1]
    kern = functools.partial(_sample_attn_kernel, layer=layer, nt=nt)
    grid_spec = pltpu.PrefetchScalarGridSpec(
        num_scalar_prefetch=1,
        grid=(db,),
        in_specs=[pl.BlockSpec((1, rows, LANES), lambda b, pt: (b, 0, 0)),
                  pl.BlockSpec((1, nt, npg * page), lambda b, pt: (b, 0, 0)),
                  pl.BlockSpec((1, nt, LANES), lambda b, pt: (b, 0, 0)),
                  pl.BlockSpec(k_new.shape, lambda b, pt: (0, 0)),
                  pl.BlockSpec(v_new.shape, lambda b, pt: (0, 0)),
                  pl.BlockSpec(memory_space=pl.ANY),
                  pl.BlockSpec(memory_space=pl.ANY)],
        out_specs=pl.BlockSpec((1, rows, LANES), lambda b, pt: (b, 0, 0)),
        scratch_shapes=[pltpu.VMEM((2, npg, LANES, page), F32), pltpu.VMEM((2, npg, LANES, page), F32),
                        pltpu.SemaphoreType.DMA((2,)), pltpu.SemaphoreType.DMA((2,)),
                        pltpu.VMEM((rows, npg * page + LANES), F32)],
    )
    return pl.pallas_call(
        kern, grid_spec=grid_spec,
        out_shape=jax.ShapeDtypeStruct((db, rows, LANES), F32),
        compiler_params=pltpu.CompilerParams(dimension_semantics=("arbitrary",), vmem_limit_bytes=VMEM_LIMIT),
        name="sample_attn",
    )(page_table, qz, bias_past, bias_new, k_new, v_new, cache_k, cache_v)


def _swap_halves(w, head_dim):
    lead = w.shape[:-1]
    nh = w.shape[-1] // head_dim
    return w.reshape(lead + (nh, 2, head_dim // 2))[..., ::-1, :].reshape(lead + (nh * head_dim,))


def _projection_weights(w_in):
    sizes = (A_WIDTH, A_WIDTH, N_HEADS * HEAD_DIM, N_KV_HEADS * HEAD_DIM, N_KV_HEADS * HEAD_DIM,
             IDX_HEADS * IDX_DIM, IDX_DIM, IDX_HEADS)
    pts = [sum(sizes[:i + 1]) for i in range(len(sizes) - 1)]
    au, av, q, k, v, qi, ki, wi = jnp.split(w_in.astype(BF16), pts, axis=-1)
    rep = LANES // IDX_DIM
    q_s, k_s, qi_s, ki_s = (_swap_halves(q, HEAD_DIM), _swap_halves(k, HEAD_DIM),
                            _swap_halves(qi, IDX_DIM), _swap_halves(ki, IDX_DIM))
    ki4, ki4_s = jnp.tile(ki, (1, 1, rep)), jnp.tile(ki_s, (1, 1, rep))
    pad_cols = lambda w, n: jnp.pad(w, ((0, 0), (0, 0), (0, n - w.shape[-1])))
    ext = jnp.concatenate([au, av, q, k, v, qi, ki4, pad_cols(wi, LANES), q_s, k_s, qi_s, ki4_s], axis=-1)
    wn = jnp.concatenate([au, av], axis=-1)
    wt = jnp.swapaxes(jnp.concatenate([q, k, v, qi, ki, pad_cols(wi, WI_ROWS)], axis=-1), 1, 2)
    assert ext.shape[-1] == C_END and wn.shape[-1] == N_END and wt.shape[1] == T_END
    return ext, wn, wt


def _rope_tables(pos, head_dim):
    half = head_dim // 2
    inv = ROPE_THETA ** (-jnp.arange(half, dtype=F32) / half)
    ang = pos.astype(F32)[:, None] * inv[None, :]
    cos, sin = jnp.cos(ang), jnp.sin(ang)
    reps = LANES // head_dim
    return (jnp.tile(jnp.concatenate([cos, cos], axis=-1), (1, reps)),
            jnp.tile(jnp.concatenate([-sin, sin], axis=-1), (1, reps)))


def _pick(n, prefs):
    for p in prefs:
        if n % p == 0:
            return p
    return n


def kernel(x_prompt, x_sample, p_prompt, p_sample, cache_k, cache_v, cache_kidx, page_table, w_in, sgu_ln_g,
           sgu_ln_b, sgu_w, sgu_b, w_o, ln1_g, ln1_b, w_ff1, w_ff2, w_ple_gate, w_ple_proj, ln2_g, ln2_b):
    nb, seq, _ = x_prompt.shape
    db, nt, _ = x_sample.shape
    depth = w_in.shape[0]
    npg = page_table.shape[1]
    page = cache_k.shape[2]
    past = npg * page
    alpha = (2.0 * depth) ** 0.25
    topk_p = min(TOPK_MAX, seq // 4)
    topk_s = min(TOPK_MAX, (past + nt) // 4)
    n_p, n_s = nb * seq, db * nt
    assert n_s == CHUNK and seq % COUNT_CHUNK == 0 and page == LANES

    w_ext, w_n, w_t = _projection_weights(w_in)
    wo_b, wf1_b, wf2_b = w_o.astype(BF16), w_ff1.astype(BF16), w_ff2.astype(BF16)
    wg_b, wp_b = w_ple_gate.astype(BF16), w_ple_proj.astype(BF16)
    tril = jnp.tril(jnp.ones((CHUNK, CHUNK), F32))
    mm_p = (sgu_w * tril).astype(BF16)
    mb_p = jnp.repeat(jnp.swapaxes(sgu_b, 1, 2), A_GW, axis=2)
    r_i = jnp.arange(CHUNK)
    same_req = (r_i[:, None] // nt == r_i[None, :] // nt) & (r_i[None, :] % nt <= r_i[:, None] % nt)
    mm_s = jnp.where(same_req, sgu_w[:, :, r_i % nt][:, :, :, r_i % nt], 0.0).astype(BF16)
    mb_s = jnp.tile(jnp.repeat(jnp.swapaxes(sgu_b[:, :, :nt], 1, 2), A_GW, axis=2), (1, CHUNK // nt, 1))
    lng = sgu_ln_g.reshape(depth, 1, A_WIDTH)
    lnb = sgu_ln_b.reshape(depth, 1, A_WIDTH)
    mlp_w = (wo_b, ln1_g.reshape(depth, 1, D_MODEL), ln1_b.reshape(depth, 1, D_MODEL), wf1_b, wf2_b, wg_b, wp_b,
             ln2_g.reshape(depth, 1, D_MODEL), ln2_b.reshape(depth, 1, D_MODEL))
    pe_p = p_prompt.reshape(depth, n_p, D_PLE)
    pe_s = p_sample.reshape(depth, n_s, D_PLE)

    pos_p = jnp.arange(seq, dtype=I32)
    pos_s = past + (jnp.arange(n_s, dtype=I32) % nt)
    tabs_p = _rope_tables(pos_p, HEAD_DIM) + _rope_tables(pos_p, IDX_DIM)
    tabs_pt = tuple(t.T for t in tabs_p)
    tabs_s = _rope_tables(pos_s, HEAD_DIM) + _rope_tables(pos_s, IDX_DIM)

    cki = jnp.swapaxes(cache_kidx, 2, 3)
    ck = jnp.transpose(cache_k, (0, 1, 3, 4, 2)).reshape(cache_k.shape[:2] + (LANES, page))
    cv = jnp.transpose(cache_v, (0, 1, 3, 4, 2)).reshape(cache_v.shape[:2] + (LANES, page))

    rt_proj = _pick(seq, (512, 256, 128))
    rt_mlp = _pick(n_p, (512, 256, 128))
    qw_p = _pick(seq, (256, 128))
    assert npg % PAGE_GROUP == 0
    head_lo = (jnp.arange(N_HEADS) // KV_GROUP) * HEAD_DIM

    hp = x_prompt.reshape(n_p, D_MODEL)
    hs = x_sample.reshape(n_s, D_MODEL)
    outs = [[] for _ in range(8)]
    for i in range(depth):
        a_out, vn, qt, kb, kt, vt, qit, kib, kit, wit = _proj_prompt_call(
            hp, i, w_n, w_t, mm_p, mb_p, lng, lnb, tabs_pt, nb, seq, rt_proj)
        b_out = _prompt_attn_call(qit, wit, qt, kib, kb, vt, nb, seq, topk_p, qw_p)
        hp = _mlp_call(hp, a_out, b_out, pe_p, i, mlp_w, alpha, rt_mlp)
        to_tokens = lambda a: jnp.transpose(a.reshape(nb, N_KV_HEADS, HEAD_DIM, seq), (0, 3, 1, 2))
        outs[0].append(to_tokens(kt))
        outs[1].append(to_tokens(vt))
        outs[2].append(jnp.swapaxes(kit, 1, 2))
        outs[3].append(vn)

        a_out, vn, q, k, v, qi, ki4, wi = _proj_call(hs, i, w_ext, mm_s, mb_s, lng, lnb, tabs_s)
        qs = qi.reshape(db, nt * IDX_HEADS, IDX_DIM)
        wrep = jnp.broadcast_to((wi[:, :IDX_HEADS] * IDX_SCALE).reshape(db, nt * IDX_HEADS, 1),
                                (db, nt * IDX_HEADS, LANES))
        keys_past = _sample_idx_call(page_table, qs, wrep, cki, i)
        bias_past, bias_new = _sample_select_call(keys_past, qi, wi, ki4, topk_s, nt)
        q4 = q.reshape(db, nt, N_HEADS, 1, HEAD_DIM)
        half = (jnp.arange(N_KV_HEADS) * HEAD_DIM)[None, None, None, :, None]
        qz = jnp.where(half == head_lo[None, None, :, None, None], q4, jnp.zeros_like(q4))
        qz = qz.reshape(db, nt * N_HEADS, LANES)
        o = _sample_attn_call(page_table, qz, bias_past, bias_new.reshape(db, nt, LANES),
                              k, v, ck, cv, i, nt)
        o = o.reshape(db, nt, N_HEADS, N_KV_HEADS, HEAD_DIM)
        b_out = jnp.concatenate([o[:, :, g * KV_GROUP:(g + 1) * KV_GROUP, g, :] for g in range(N_KV_HEADS)], axis=2)
        b_out = b_out.reshape(n_s, N_HEADS * HEAD_DIM).astype(BF16)
        hs = _mlp_call(hs, a_out, b_out, pe_s, i, mlp_w, alpha, n_s)
        outs[4].append(k.reshape(db, nt, N_KV_HEADS, HEAD_DIM))
        outs[5].append(v.reshape(db, nt, N_KV_HEADS, HEAD_DIM))
        outs[6].append(ki4[:, :IDX_DIM].reshape(db, nt, IDX_DIM))
        outs[7].append(vn.reshape(db, nt, A_WIDTH))

    st = [jnp.stack(o) for o in outs]
    return (hp.reshape(nb, seq, D_MODEL), hs.reshape(db, nt, D_MODEL),
            st[0], st[1], st[2], st[3], st[4], st[5], st[6], st[7])
```

```python
import functools

import jax
import jax.numpy as jnp
from jax import lax
from jax.experimental import pallas as pl
from jax.experimental.pallas import tpu as pltpu

F32 = jnp.float32
BF16 = jnp.bfloat16
I32 = jnp.int32
I16 = jnp.int16

D_MODEL = 1024
CHUNK = 128
A_WIDTH = 512
A_GROUPS = 4
A_GW = A_WIDTH // A_GROUPS
HEAD_DIM = 64
N_HEADS = 8
N_KV_HEADS = 2
KV_GROUP = N_HEADS // N_KV_HEADS
IDX_HEADS = 8
IDX_DIM = 32
TOPK_MAX = 256
ROPE_THETA = 10000.0
D_FF = 4 * D_MODEL
D_PLE = 256
LN_EPS = 1e-5

LANES = 128
SUBLANES = 8
PACK = 16
VMEM_LIMIT = 56 * 1024 * 1024

INT_MIN = -(2 ** 31)
I16_MIN = -(2 ** 15)
NEG = -1e30
DENOM_FLOOR = 2.0 ** -100

C_AU, C_AV, C_Q, C_K, C_V, C_QI, C_KI, C_WI = 0, 512, 1024, 1536, 1664, 1792, 2048, 2176
C_QS, C_KS, C_QIS, C_KIS, C_END = 2304, 2816, 2944, 3200, 3328
N_AU, N_AV, N_END = 0, 512, 1024
T_Q, T_K, T_V, T_QI, T_KI, T_WI, T_END = 0, 512, 640, 768, 1024, 1056, 1072
WI_ROWS = T_END - T_WI
Q_SCALE_LOG2 = (HEAD_DIM ** -0.5) * 1.4426950408889634
IDX_SCALE = (IDX_DIM ** -0.5) * (IDX_HEADS ** -0.5)

KEY_CHUNK = 512
COUNT_CHUNK = 512

NT_DIMS = (((1,), (1,)), ((), ()))


def _dot(a, b):
    return jnp.dot(a, b, preferred_element_type=F32)


def _dot_nt(a, b):
    return lax.dot_general(a, b, NT_DIMS, preferred_element_type=F32)


def _layer_norm(x, g, b):
    mu = jnp.mean(x, axis=-1, keepdims=True)
    xc = x - mu
    var = jnp.mean(xc * xc, axis=-1, keepdims=True)
    return xc * lax.rsqrt(var + LN_EPS) * g + b


def _gelu(x):
    return 0.5 * x * (1.0 + lax.erf(x * (2.0 ** -0.5)))


def _sortable(x):
    bits = pltpu.bitcast(x, I32)
    return bits ^ ((bits >> 31) & 0x7FFFFFFF)


def _layer_spec(stacked, layer):
    shape = stacked.shape[1:]
    return pl.BlockSpec((None,) + shape, lambda *_: (layer,) + (0,) * len(shape),
                        pipeline_mode=pl.Buffered(1))


def _gated_mixer(au, av, g, lng_ref, lnb_ref, mm_ref, mb_ref, vn_ref, aout_ref):
    rows = au.shape[0]
    lo = g * A_GW
    vn = _layer_norm(av, lng_ref[:, lo:lo + A_GW], lnb_ref[:, lo:lo + A_GW])
    vn_ref[:, lo:lo + A_GW] = vn[rows - vn_ref.shape[0]:, :]
    vnb = vn.astype(BF16)
    for c in range(rows // CHUNK):
        r0 = c * CHUNK
        gate = _dot(mm_ref[g], vnb[r0:r0 + CHUNK, :]) + mb_ref[:, lo:lo + A_GW]
        aout_ref[r0:r0 + CHUNK, lo:lo + A_GW] = (au[r0:r0 + CHUNK, :] * gate).astype(BF16)


def _proj_kernel(h_ref, w_ref, mm_ref, mb_ref, lng_ref, lnb_ref, cq_ref, sq_ref, ci_ref, si_ref,
                 aout_ref, vn_ref, q_ref, k_ref, v_ref, qi_ref, ki_ref, wi_ref):
    hb = h_ref[...].astype(BF16)

    def proj(lo, width=LANES):
        return _dot(hb, w_ref[:, lo:lo + width])

    for g in range(A_GROUPS):
        lo = g * A_GW
        _gated_mixer(_gelu(proj(C_AU + lo)), _gelu(proj(C_AV + lo)), g,
                     lng_ref, lnb_ref, mm_ref, mb_ref, vn_ref, aout_ref)

    cq, sq = cq_ref[...], sq_ref[...]
    ci, si = ci_ref[...], si_ref[...]
    for j in range(N_HEADS * HEAD_DIM // LANES):
        z = proj(C_Q + j * LANES) * cq + proj(C_QS + j * LANES) * sq
        q_ref[:, j * LANES:(j + 1) * LANES] = (z * (HEAD_DIM ** -0.5)).astype(BF16)
    k_ref[...] = proj(C_K) * cq + proj(C_KS) * sq
    v_ref[...] = proj(C_V)
    for j in range(IDX_HEADS * IDX_DIM // LANES):
        z = proj(C_QI + j * LANES) * ci + proj(C_QIS + j * LANES) * si
        qi_ref[:, j * LANES:(j + 1) * LANES] = z.astype(BF16)
    ki_ref[...] = proj(C_KI) * ci + proj(C_KIS) * si
    wi_ref[...] = proj(C_WI)


def _proj_call(h2d, layer, w_ext, mm, mb, lng, lnb, tabs):
    n = h2d.shape[0]
    whole = lambda a: pl.BlockSpec(a.shape, lambda i: (0, 0))
    out_widths = [(A_WIDTH, BF16), (A_WIDTH, F32), (N_HEADS * HEAD_DIM, BF16), (LANES, F32), (LANES, F32),
                  (IDX_HEADS * IDX_DIM, BF16), (LANES, F32), (LANES, F32)]
    return pl.pallas_call(
        _proj_kernel,
        grid=(1,),
        in_specs=[whole(h2d)] + [_layer_spec(a, layer) for a in (w_ext, mm, mb, lng, lnb)]
                 + [whole(t) for t in tabs],
        out_specs=[pl.BlockSpec((n, w), lambda i: (0, 0)) for w, _ in out_widths],
        out_shape=[jax.ShapeDtypeStruct((n, w), dt) for w, dt in out_widths],
        compiler_params=pltpu.CompilerParams(dimension_semantics=("arbitrary",), vmem_limit_bytes=VMEM_LIMIT),
        name="proj",
    )(h2d, w_ext, mm, mb, lng, lnb, *tabs)


def _rope_t(x, cos_t, sin_t, head_dim):
    half = head_dim // 2
    parts = []
    for r in range(0, x.shape[0], head_dim):
        parts += [x[r + half:r + head_dim], x[r:r + half]]
    return x * cos_t + jnp.concatenate(parts, axis=0) * sin_t


def _proj_prompt_kernel(h_ref, wn_ref, wt_ref, mm_ref, mb_ref, lng_ref, lnb_ref,
                        cqt_ref, sqt_ref, cit_ref, sit_ref,
                        aout_ref, vn_ref, qt_ref, kb_ref, kt_ref, vt_ref, qit_ref, kib_ref, kit_ref, wit_ref):
    hb = h_ref[...].astype(BF16)
    wide = 2 * LANES

    zt = _dot_nt(wt_ref[...], hb)
    cqt, sqt = cqt_ref[...], sqt_ref[...]
    cit, sit = cit_ref[...], sit_ref[...]
    for j in range(N_HEADS * HEAD_DIM // LANES):
        r = j * LANES
        q = _rope_t(zt[T_Q + r:T_Q + r + LANES], cqt, sqt, HEAD_DIM)
        qt_ref[r:r + LANES, :] = (q * Q_SCALE_LOG2).astype(BF16)
    kt = _rope_t(zt[T_K:T_K + LANES], cqt, sqt, HEAD_DIM)
    kt_ref[...] = kt
    vt_ref[...] = zt[T_V:T_V + LANES]
    for j in range(IDX_HEADS * IDX_DIM // LANES):
        r = j * LANES
        qit_ref[r:r + LANES, :] = _rope_t(zt[T_QI + r:T_QI + r + LANES], cit, sit, IDX_DIM).astype(BF16)
    kit = _rope_t(zt[T_KI:T_KI + IDX_DIM], cit[:IDX_DIM], sit[:IDX_DIM], IDX_DIM)
    kit_ref[...] = kit
    wit_ref[...] = zt[T_WI:T_END]

    kb_ref[...] = kt.T.astype(BF16)
    kib_ref[...] = jnp.concatenate([kit] * (LANES // IDX_DIM), axis=0).T.astype(BF16)

    for gp in range(A_GROUPS // 2):
        au2 = _gelu(_dot(hb, wn_ref[:, N_AU + gp * wide:N_AU + (gp + 1) * wide]))
        av2 = _gelu(_dot(hb, wn_ref[:, N_AV + gp * wide:N_AV + (gp + 1) * wide]))
        for e in range(2):
            _gated_mixer(au2[:, e * A_GW:(e + 1) * A_GW], av2[:, e * A_GW:(e + 1) * A_GW], 2 * gp + e,
                         lng_ref, lnb_ref, mm_ref, mb_ref, vn_ref, aout_ref)


def _proj_prompt_call(h2d, layer, wn, wt, mm, mb, lng, lnb, tabs_t, nb, seq, row_tile):
    n = nb * seq
    ns = seq // row_tile
    row_map = lambda b, s: (b * ns + s, 0)
    col_map = lambda b, s: (0, b * ns + s)
    bcol_map = lambda b, s: (b, 0, s)
    hd, idd = N_HEADS * HEAD_DIM, IDX_HEADS * IDX_DIM
    out_specs = [pl.BlockSpec((row_tile, A_WIDTH), row_map),
                 pl.BlockSpec((None, CHUNK, A_WIDTH), lambda b, s: (b, 0, 0)),
                 pl.BlockSpec((hd, row_tile), col_map),
                 pl.BlockSpec((row_tile, LANES), row_map),
                 pl.BlockSpec((None, LANES, row_tile), bcol_map),
                 pl.BlockSpec((None, LANES, row_tile), bcol_map),
                 pl.BlockSpec((idd, row_tile), col_map),
                 pl.BlockSpec((row_tile, LANES), row_map),
                 pl.BlockSpec((None, IDX_DIM, row_tile), bcol_map),
                 pl.BlockSpec((WI_ROWS, row_tile), col_map)]
    out_shape = [jax.ShapeDtypeStruct((n, A_WIDTH), BF16), jax.ShapeDtypeStruct((nb, CHUNK, A_WIDTH), F32),
                 jax.ShapeDtypeStruct((hd, n), BF16),
                 jax.ShapeDtypeStruct((n, LANES), BF16),
                 jax.ShapeDtypeStruct((nb, LANES, seq), F32),
                 jax.ShapeDtypeStruct((nb, LANES, seq), F32),
                 jax.ShapeDtypeStruct((idd, n), BF16),
                 jax.ShapeDtypeStruct((n, LANES), BF16),
                 jax.ShapeDtypeStruct((nb, IDX_DIM, seq), F32),
                 jax.ShapeDtypeStruct((WI_ROWS, n), F32)]
    return pl.pallas_call(
        _proj_prompt_kernel,
        grid=(nb, ns),
        in_specs=[pl.BlockSpec((row_tile, D_MODEL), row_map)]
                 + [_layer_spec(a, layer) for a in (wn, wt, mm, mb, lng, lnb)]
                 + [pl.BlockSpec((LANES, row_tile), lambda b, s: (0, s))] * 4,
        out_specs=out_specs,
        out_shape=out_shape,
        compiler_params=pltpu.CompilerParams(dimension_semantics=("arbitrary", "arbitrary"),
                                             vmem_limit_bytes=VMEM_LIMIT),
        name="proj_prompt",
    )(h2d, wn, wt, mm, mb, lng, lnb, *tabs_t)


def _count16(ref, nsteps, cand, strict=False):
    one, zero = jnp.ones(cand.shape, I16), jnp.zeros(cand.shape, I16)
    n_acc = 4

    def body(c, accs):
        off = pl.multiple_of(c * COUNT_CHUNK, COUNT_CHUNK)
        blk = ref[pl.ds(off, COUNT_CHUNK), :]
        accs = list(accs)
        for j in range(COUNT_CHUNK // PACK):
            kk = blk[j * PACK:(j + 1) * PACK, :]
            hit = (kk > cand) if strict else (kk >= cand)
            accs[j % n_acc] = accs[j % n_acc] + jnp.where(hit, one, zero)
        return tuple(accs)
    accs = lax.fori_loop(0, nsteps, body, (zero,) * n_acc)
    tot = (accs[0] + accs[1]) + (accs[2] + accs[3])
    return jnp.sum(tot.astype(F32), axis=0, keepdims=True)


def _search16(ref, nsteps, target):
    qw = ref.shape[1]

    def bit_body(i, prefix):
        cand = prefix + lax.shift_left(jnp.int32(1), 15 - i)
        cand16 = jnp.broadcast_to(cand, (PACK, qw)).astype(I16)
        return jnp.where(_count16(ref, nsteps, cand16) >= target, cand, prefix)
    return lax.fori_loop(0, 16, bit_body, jnp.full((1, qw), I16_MIN, I32))


def _kth_threshold(keys_ref, hi_ref, lo_ref, nch, nsteps, topk):
    qw = keys_ref.shape[1]
    p_hi = _search16(hi_ref, nsteps, float(topk))
    p16 = jnp.broadcast_to(p_hi, (PACK, qw)).astype(I16)
    room = topk - _count16(hi_ref, nsteps, p16, strict=True)

    def low_body(c, carry):
        off = pl.multiple_of(c * COUNT_CHUNK, COUNT_CHUNK)
        lowest = jnp.full((PACK, qw), I16_MIN, I16)
        his, los = hi_ref[pl.ds(off, COUNT_CHUNK), :], lo_ref[pl.ds(off, COUNT_CHUNK), :]
        lo_ref[pl.ds(off, COUNT_CHUNK), :] = jnp.concatenate(
            [jnp.where(his[j * PACK:(j + 1) * PACK, :] == p16, los[j * PACK:(j + 1) * PACK, :], lowest)
             for j in range(COUNT_CHUNK // PACK)], axis=0)
        return carry
    lax.fori_loop(0, nsteps, low_body, 0)
    p_lo = _search16(lo_ref, nsteps, room)
    thr = jnp.maximum(p_hi * 65536 + (p_lo + 2 ** 15), INT_MIN + 1)

    def count32(strict):
        sub = SUBLANES

        def body(c, acc):
            off = pl.multiple_of(c * KEY_CHUNK, KEY_CHUNK)
            kk = keys_ref[pl.ds(off, KEY_CHUNK), :]
            ones = jnp.where((kk > thr) if strict else (kk >= thr), 1.0, 0.0)
            parts = [ones[j * sub:(j + 1) * sub, :] for j in range(KEY_CHUNK // sub)]
            while len(parts) > 1:
                parts = [a + b for a, b in zip(parts[0::2], parts[1::2])]
            return acc + parts[0]
        acc = lax.fori_loop(0, nch, body, jnp.zeros((sub, qw), F32))
        return jnp.sum(acc, axis=0, keepdims=True)

    @pl.when(jnp.max(count32(False)) > topk)
    def _():
        spare = topk - count32(True)
        r_i = lax.broadcasted_iota(I32, (KEY_CHUNK, KEY_CHUNK), 0)
        c_i = lax.broadcasted_iota(I32, (KEY_CHUNK, KEY_CHUNK), 1)
        lower = jnp.where(c_i <= r_i, 1.0, 0.0).astype(BF16)

        def body(c, seen):
            off = pl.multiple_of(c * KEY_CHUNK, KEY_CHUNK)
            kk = keys_ref[pl.ds(off, KEY_CHUNK), :]
            tie = kk == thr
            rank = seen + _dot(lower, jnp.where(tie, 1.0, 0.0).astype(BF16))
            keys_ref[pl.ds(off, KEY_CHUNK), :] = jnp.where(tie & (rank > spare), thr - 1, kk)
            return rank[KEY_CHUNK - 1:KEY_CHUNK, :]
        lax.fori_loop(0, nch, body, jnp.zeros((1, qw), F32))

    return thr


def _prompt_attn_kernel(qit_ref, wit_ref, qt_ref, kib_ref, kb_ref, vt_ref, o_ref,
                        keys_ref, hi_ref, lo_ref, rhs_ref, qz_ref, acc_ref, kmax_ref, *, topk):
    qw = o_ref.shape[0]
    qb = pl.program_id(1)
    nch = (qb * qw + qw + KEY_CHUNK - 1) // KEY_CHUNK
    nsteps = (nch * KEY_CHUNK + COUNT_CHUNK - 1) // COUNT_CHUNK
    qpos = qb * qw + lax.broadcasted_iota(I32, (KEY_CHUNK, qw), 1)
    krow = lax.broadcasted_iota(I32, (KEY_CHUNK, qw), 0)

    @pl.when((pl.program_id(0) == 0) & (qb == 0))
    def _():
        rhs_ref[...] = jnp.zeros(rhs_ref.shape, BF16)
        qz_ref[...] = jnp.zeros(qz_ref.shape, BF16)
    slots = LANES // IDX_DIM
    for h in range(IDX_HEADS):
        r = (h % slots) * IDX_DIM
        rhs_ref[r:r + IDX_DIM, h * qw:(h + 1) * qw] = qit_ref[h * IDX_DIM:(h + 1) * IDX_DIM, :]
    for h in range(N_HEADS):
        r = (h // KV_GROUP) * HEAD_DIM
        qz_ref[r:r + HEAD_DIM, h * qw:(h + 1) * qw] = qt_ref[h * HEAD_DIM:(h + 1) * HEAD_DIM, :]
    wsc = wit_ref[...] * IDX_SCALE

    def score_chunk(c, causal_edge):
        off = pl.multiple_of(c * KEY_CHUNK, KEY_CHUNK)
        kc = kib_ref[pl.ds(off, KEY_CHUNK), :]
        acc = None
        for h in range(IDX_HEADS):
            s = _dot(kc, rhs_ref[:, h * qw:(h + 1) * qw])
            t = jnp.maximum(s, 0.0) * wsc[h:h + 1, :]
            acc = t if acc is None else acc + t
        key = _sortable(acc)
        if causal_edge:
            key = jnp.where(krow + off <= qpos, key, INT_MIN)
        keys_ref[pl.ds(off, KEY_CHUNK), :] = key
        hi_ref[pl.ds(off, KEY_CHUNK), :] = (key >> 16).astype(I16)
        lo_ref[pl.ds(off, KEY_CHUNK), :] = (key ^ 0x8000).astype(I16)

    def score_body(c, carry):
        score_chunk(c, False)
        return carry
    lax.fori_loop(0, nch - 1, score_body, 0)
    score_chunk(nch - 1, True)

    @pl.when(nch * KEY_CHUNK < nsteps * COUNT_CHUNK)
    def _():
        off = pl.multiple_of(nch * KEY_CHUNK, KEY_CHUNK)
        fill = jnp.full((KEY_CHUNK, qw), I16_MIN, I16)
        hi_ref[pl.ds(off, KEY_CHUNK), :] = fill
        lo_ref[pl.ds(off, KEY_CHUNK), :] = fill

    thr = _kth_threshold(keys_ref, hi_ref, lo_ref, nch, nsteps, topk)

    ones_rows = jnp.where(lax.broadcasted_iota(I32, (PACK, KEY_CHUNK), 0) == 0, 1.0, 0.0).astype(BF16)

    @pl.when(qb == 0)
    def _():
        kf = kb_ref[...].astype(F32)
        kmax_ref[0] = jnp.max(jnp.sum(kf * kf, axis=1, keepdims=True))
    shifts = []
    for h in range(N_HEADS):
        qf = qt_ref[h * HEAD_DIM:(h + 1) * HEAD_DIM, :].astype(F32)
        shifts.append(jnp.sqrt(jnp.sum(qf * qf, axis=0, keepdims=True) * kmax_ref[0]))

    def attn_chunk(c, shift_fn):
        off = pl.multiple_of(c * KEY_CHUNK, KEY_CHUNK)
        sel = keys_ref[pl.ds(off, KEY_CHUNK), :] >= thr
        kb = kb_ref[pl.ds(off, KEY_CHUNK), :]
        vt = vt_ref[:, pl.ds(off, KEY_CHUNK)].astype(BF16)
        for g in range(N_KV_HEADS):
            ps, alphas = [], []
            for j in range(KV_GROUP):
                h = g * KV_GROUP + j
                s = jnp.where(sel, _dot(kb, qz_ref[:, h * qw:(h + 1) * qw]), NEG)
                shift, alpha = shift_fn(h, s)
                alphas.append(alpha)
                ps.append(jnp.exp2(s - shift).astype(BF16))
            vg = jnp.concatenate([vt[g * HEAD_DIM:(g + 1) * HEAD_DIM, :], ones_rows], axis=0)
            pv = _dot(vg, jnp.concatenate(ps, axis=1))
            if alphas[0] is None:
                acc_ref[g] = acc_ref[g] + pv
            else:
                acc_ref[g] = acc_ref[g] * jnp.concatenate(alphas, axis=1) + pv

    acc_ref[...] = jnp.zeros(acc_ref.shape, F32)

    def fast_body(c, carry):
        attn_chunk(c, lambda h, s: (shifts[h], None))
        return carry
    lax.fori_loop(0, nch, fast_body, 0)

    denoms = [acc_ref[g][HEAD_DIM:HEAD_DIM + 1, :] for g in range(N_KV_HEADS)]
    denom_min = jnp.min(functools.reduce(jnp.minimum, denoms))

    @pl.when(jnp.logical_not(denom_min > DENOM_FLOOR))
    def _():
        acc_ref[...] = jnp.zeros(acc_ref.shape, F32)

        def online_body(c, ms):
            ms = list(ms)

            def running_max(h, s):
                m_new = jnp.maximum(ms[h], jnp.max(s, axis=0, keepdims=True))
                alpha = jnp.exp2(ms[h] - m_new)
                ms[h] = m_new
                return m_new, alpha
            attn_chunk(c, running_max)
            return tuple(ms)
        lax.fori_loop(0, nch, online_body, (jnp.full((1, qw), NEG, F32),) * N_HEADS)

    for p in range(N_HEADS // 2):
        halves = []
        for h in (2 * p, 2 * p + 1):
            g, j = h // KV_GROUP, h % KV_GROUP
            acc = acc_ref[g][:, j * qw:(j + 1) * qw]
            halves.append(acc[:HEAD_DIM, :] / acc[HEAD_DIM:HEAD_DIM + 1, :])
        o_ref[:, p * LANES:(p + 1) * LANES] = jnp.concatenate(halves, axis=0).T.astype(BF16)


def _prompt_attn_call(qit, wit, qt, kib, kb, vt, nb, seq, topk, qw):
    nq = seq // qw
    hd, idd = N_HEADS * HEAD_DIM, IDX_HEADS * IDX_DIM
    col = lambda b, i: (0, b * nq + i)
    kern = functools.partial(_prompt_attn_kernel, topk=topk)
    return pl.pallas_call(
        kern,
        grid=(nb, nq),
        in_specs=[pl.BlockSpec((idd, qw), col),
                  pl.BlockSpec((WI_ROWS, qw), col),
                  pl.BlockSpec((hd, qw), col),
                  pl.BlockSpec((seq, LANES), lambda b, i: (b, 0)),
                  pl.BlockSpec((seq, LANES), lambda b, i: (b, 0)),
                  pl.BlockSpec((None, LANES, seq), lambda b, i: (b, 0, 0))],
        out_specs=pl.BlockSpec((qw, hd), lambda b, i: (b * nq + i, 0)),
        out_shape=jax.ShapeDtypeStruct((nb * seq, hd), BF16),
        scratch_shapes=[pltpu.VMEM((seq, qw), I32),
                        pltpu.VMEM((seq, qw), I16),
                        pltpu.VMEM((seq, qw), I16),
                        pltpu.VMEM((LANES, IDX_HEADS * qw), BF16),
                        pltpu.VMEM((LANES, N_HEADS * qw), BF16),
                        pltpu.VMEM((N_KV_HEADS, HEAD_DIM + PACK, KV_GROUP * qw), F32),
                        pltpu.SMEM((1,), F32)],
        compiler_params=pltpu.CompilerParams(dimension_semantics=("arbitrary", "arbitrary"),
                                             vmem_limit_bytes=VMEM_LIMIT),
        name="prompt_attn",
    )(qit, wit, qt, kib, kb, vt)


def _mlp_kernel(h_ref, a_ref, b_ref, pe_ref, wo_ref, g1_ref, b1_ref, wf1_ref, wf2_ref, wg_ref, wp_ref,
                g2_ref, b2_ref, out_ref, *, alpha, ff_chunk):
    mix = _dot(a_ref[...], wo_ref[0:A_WIDTH, :]) + _dot(b_ref[...], wo_ref[A_WIDTH:2 * A_WIDTH, :])
    h1 = _layer_norm(alpha * h_ref[...] + mix, g1_ref[...], b1_ref[...])
    h1b = h1.astype(BF16)
    ff = None
    for c in range(D_FF // ff_chunk):
        hid = jnp.maximum(_dot(h1b, wf1_ref[:, c * ff_chunk:(c + 1) * ff_chunk]), 0.0)
        part = _dot((hid * hid).astype(BF16), wf2_ref[c * ff_chunk:(c + 1) * ff_chunk, :])
        ff = part if ff is None else ff + part
    ple = jax.nn.sigmoid(_dot(h1b, wg_ref[...])) * _dot(pe_ref[...].astype(BF16), wp_ref[...])
    out_ref[...] = _layer_norm(alpha * h1 + ff + ple, g2_ref[...], b2_ref[...])


def _mlp_call(h2d, a_out, b_out, pe, layer, consts, alpha, row_tile):
    n = h2d.shape[0]
    row = lambda i: (i, 0)
    kern = functools.partial(_mlp_kernel, alpha=alpha, ff_chunk=1024)
    return pl.pallas_call(
        kern,
        grid=(n // row_tile,),
        in_specs=[pl.BlockSpec((row_tile, D_MODEL), row),
                  pl.BlockSpec((row_tile, A_WIDTH), row),
                  pl.BlockSpec((row_tile, A_WIDTH), row),
                  pl.BlockSpec((None, row_tile, D_PLE), lambda i: (layer, i, 0))]
                 + [_layer_spec(c, layer) for c in consts],
        out_specs=pl.BlockSpec((row_tile, D_MODEL), row),
        out_shape=jax.ShapeDtypeStruct((n, D_MODEL), F32),
        compiler_params=pltpu.CompilerParams(dimension_semantics=("arbitrary",),
                                             vmem_limit_bytes=VMEM_LIMIT),
        name="mlp",
    )(h2d, a_out, b_out, pe, *consts)


def _head_masked_queries(qi_ref):
    lane_head = lax.broadcasted_iota(I32, (CHUNK, LANES), 1) >> 5
    heads_per_col = LANES // IDX_DIM
    out = []
    for h in range(IDX_HEADS):
        col = qi_ref[:, (h // heads_per_col) * LANES:(h // heads_per_col + 1) * LANES]
        out.append(jnp.where(lane_head == (h % heads_per_col), col, jnp.zeros_like(col)))
    return out


def _index_scores(qh, wsc, kc):
    acc = None
    for h in range(IDX_HEADS):
        s = _dot_nt(qh[h], kc)
        t = jnp.maximum(s, 0.0) * wsc[:, h:h + 1]
        acc = t if acc is None else acc + t
    return acc


PAGE_GROUP = 8


def _request_pages(pt_ref, layer, npg, streams):
    b, nreq = pl.program_id(0), pl.num_programs(0)

    def page_copy(stream, req, slot, j):
        cache, buf, sem = stream
        return pltpu.make_async_copy(cache.at[layer, pt_ref[req, j]], buf.at[slot, j], sem.at[slot])

    def start_all(req, slot):
        def body(i, carry):
            for prio in range(2):
                for stream in streams:
                    page_copy(stream, req, slot, 2 * i + prio).start(priority=prio)
            return carry
        lax.fori_loop(0, npg // 2, body, 0)

    @pl.when(b == 0)
    def _():
        start_all(0, 0)

    @pl.when(b + 1 < nreq)
    def _():
        start_all(b + 1, (b + 1) % 2)

    slot = b % 2

    def wait_body(j, carry):
        for stream in streams:
            page_copy(stream, b, slot, j).wait()
        return carry
    lax.fori_loop(0, npg, wait_body, 0)
    return slot


def _sample_idx_kernel(pt_ref, qs_ref, w_ref, cki_ref, out_ref, buf, sem, *, layer):
    npg = buf.shape[1]
    slot = _request_pages(pt_ref, layer, npg, [(cki_ref, buf, sem)])
    qs = qs_ref[0]
    w = jnp.concatenate([w_ref[0]] * PAGE_GROUP, axis=1)
    nt = out_ref.shape[1]
    width = PAGE_GROUP * LANES
    for c in range(npg // PAGE_GROUP):
        kp = jnp.concatenate([buf[slot, c * PAGE_GROUP + i] for i in range(PAGE_GROUP)], axis=1)
        t = jnp.maximum(_dot(qs, kp.astype(BF16)), 0.0) * w
        for q in range(nt):
            sc = jnp.sum(t[q * IDX_HEADS:(q + 1) * IDX_HEADS, :], axis=0, keepdims=True)
            out_ref[0, q:q + 1, c * width:(c + 1) * width] = _sortable(sc)


def _sample_idx_call(page_table, qs, wrep, cache_kidx, layer):
    db, npg = page_table.shape
    nt = qs.shape[1] // IDX_HEADS
    page = cache_kidx.shape[3]
    kern = functools.partial(_sample_idx_kernel, layer=layer)
    grid_spec = pltpu.PrefetchScalarGridSpec(
        num_scalar_prefetch=1,
        grid=(db,),
        in_specs=[pl.BlockSpec((1,) + qs.shape[1:], lambda b, pt: (b, 0, 0)),
                  pl.BlockSpec((1,) + wrep.shape[1:], lambda b, pt: (b, 0, 0)),
                  pl.BlockSpec(memory_space=pl.ANY)],
        out_specs=pl.BlockSpec((1, nt, npg * page), lambda b, pt: (b, 0, 0)),
        scratch_shapes=[pltpu.VMEM((2, npg, IDX_DIM, page), F32), pltpu.SemaphoreType.DMA((2,))],
    )
    return pl.pallas_call(
        kern, grid_spec=grid_spec,
        out_shape=jax.ShapeDtypeStruct((db, nt, npg * page), I32),
        compiler_params=pltpu.CompilerParams(dimension_semantics=("arbitrary",), vmem_limit_bytes=VMEM_LIMIT),
        name="sample_idx",
    )(page_table, qs, wrep, cache_kidx)


def _sample_select_kernel(kp_ref, qi_ref, wi_ref, ki_ref, bp_ref, bn_ref,
                          keys_ref, kt_ref, hi_ref, lo_ref, bias_ref, *, topk, nt):
    nreq, _, past = kp_ref.shape
    width = keys_ref.shape[1]
    for b in range(nreq):
        keys_ref[b * nt:(b + 1) * nt, 0:past] = kp_ref[b]
    qh = _head_masked_queries(qi_ref)
    wsc = wi_ref[...] * IDX_SCALE
    key = _sortable(_index_scores(qh, wsc, ki_ref[...].astype(BF16)))
    r_i = lax.broadcasted_iota(I32, (CHUNK, LANES), 0)
    c_i = lax.broadcasted_iota(I32, (CHUNK, LANES), 1)
    visible = ((r_i // nt) == (c_i // nt)) & ((c_i % nt) <= (r_i % nt))
    keys_ref[:, past:past + LANES] = jnp.where(visible, key, INT_MIN)
    if width > past + LANES:
        keys_ref[:, past + LANES:width] = jnp.full((CHUNK, width - past - LANES), INT_MIN, I32)

    for c in range(width // LANES):
        rows = slice(c * LANES, (c + 1) * LANES)
        blk = keys_ref[:, rows].T
        kt_ref[rows, :] = blk
        hi_ref[rows, :] = (blk >> 16).astype(I16)
        lo_ref[rows, :] = (blk ^ 0x8000).astype(I16)

    thr = _kth_threshold(kt_ref, hi_ref, lo_ref, width // KEY_CHUNK, width // COUNT_CHUNK, topk)

    for c in range(width // LANES):
        rows = slice(c * LANES, (c + 1) * LANES)
        bias_ref[:, rows] = jnp.where(kt_ref[rows, :] >= thr, 0.0, NEG).T
    for b in range(nreq):
        bp_ref[b] = bias_ref[b * nt:(b + 1) * nt, 0:past]
    bn_ref[...] = bias_ref[:, past:past + LANES]


def _sample_select_call(keys_past, qi, wi, ki4, topk, nt):
    nreq, _, past = keys_past.shape
    rows = nreq * nt
    step = max(KEY_CHUNK, COUNT_CHUNK)
    width = -(-(past + LANES) // step) * step
    kern = functools.partial(_sample_select_kernel, topk=topk, nt=nt)
    return pl.pallas_call(
        kern,
        out_shape=[jax.ShapeDtypeStruct((nreq, nt, past), F32), jax.ShapeDtypeStruct((rows, LANES), F32)],
        scratch_shapes=[pltpu.VMEM((rows, width), I32), pltpu.VMEM((width, rows), I32),
                        pltpu.VMEM((width, rows), I16), pltpu.VMEM((width, rows), I16),
                        pltpu.VMEM((rows, width), F32)],
        compiler_params=pltpu.CompilerParams(vmem_limit_bytes=VMEM_LIMIT),
        name="sample_select",
    )(keys_past, qi, wi, ki4)


def _sample_attn_kernel(pt_ref, qz_ref, bp_ref, bn_ref, kn_ref, vn_ref, ck_ref, cv_ref, o_ref,
                        kbuf, vbuf, ksem, vsem, s_ref, *, layer, nt):
    npg = kbuf.shape[1]
    slot = _request_pages(pt_ref, layer, npg, [(ck_ref, kbuf, ksem), (cv_ref, vbuf, vsem)])
    qz = qz_ref[0]
    rows = qz.shape[0]
    hpq = rows // nt
    width = PAGE_GROUP * LANES
    groups = npg // PAGE_GROUP
    past = npg * LANES

    def lane_blocks(x):
        return [x[:, i * LANES:(i + 1) * LANES] for i in range(x.shape[1] // LANES)]

    def masked(s, bias):
        return jnp.concatenate(
            [s[q * hpq:(q + 1) * hpq, :] + bias[q:q + 1, :] for q in range(nt)], axis=0)

    def pages_of(buf, c):
        return jnp.concatenate([buf[slot, c * PAGE_GROUP + i] for i in range(PAGE_GROUP)], axis=1).astype(BF16)

    bias_past = bp_ref[0]
    m_run = jnp.full((rows, LANES), NEG, F32)
    for c in range(groups):
        s = masked(_dot(qz, pages_of(kbuf, c)), bias_past[:, c * width:(c + 1) * width])
        s_ref[:, c * width:(c + 1) * width] = s
        m_run = functools.reduce(jnp.maximum, lane_blocks(s), m_run)
    s_new = masked(_dot_nt(qz, kn_ref[...].astype(BF16)), bn_ref[0])
    s_ref[:, past:past + LANES] = s_new
    m = jnp.max(jnp.maximum(m_run, s_new), axis=-1, keepdims=True)

    pr = jnp.exp(s_new - m)
    l_run = pr
    acc = _dot(pr.astype(BF16), vn_ref[...].astype(BF16))
    for c in range(groups):
        pr = jnp.exp(s_ref[:, c * width:(c + 1) * width] - m)
        l_run = functools.reduce(jnp.add, lane_blocks(pr), l_run)
        acc = acc + _dot_nt(pr.astype(BF16), pages_of(vbuf, c))
    o_ref[0] = acc / jnp.sum(l_run, axis=-1, keepdims=True)


def _sample_attn_call(page_table, qz, bias_past, bias_new, k_new, v_new, cache_k, cache_v, layer, nt):
    db, npg = page_table.shape
    page = cache_k.shape[3]
    rows = qz.shape[1]
    kern = functools.partial(_sample_attn_kernel, layer=layer, nt=nt)
    grid_spec = pltpu.PrefetchScalarGridSpec(
        num_scalar_prefetch=1,
        grid=(db,),
        in_specs=[pl.BlockSpec((1, rows, LANES), lambda b, pt: (b, 0, 0)),
                  pl.BlockSpec((1, nt, npg * page), lambda b, pt: (b, 0, 0)),
                  pl.BlockSpec((1, nt, LANES), lambda b, pt: (b, 0, 0)),
                  pl.BlockSpec(k_new.shape, lambda b, pt: (0, 0)),
                  pl.BlockSpec(v_new.shape, lambda b, pt: (0, 0)),
                  pl.BlockSpec(memory_space=pl.ANY),
                  pl.BlockSpec(memory_space=pl.ANY)],
        out_specs=pl.BlockSpec((1, rows, LANES), lambda b, pt: (b, 0, 0)),
        scratch_shapes=[pltpu.VMEM((2, npg, LANES, page), F32), pltpu.VMEM((2, npg, LANES, page), F32),
                        pltpu.SemaphoreType.DMA((2,)), pltpu.SemaphoreType.DMA((2,)),
                        pltpu.VMEM((rows, npg * page + LANES), F32)],
    )
    return pl.pallas_call(
        kern, grid_spec=grid_spec,
        out_shape=jax.ShapeDtypeStruct((db, rows, LANES), F32),
        compiler_params=pltpu.CompilerParams(dimension_semantics=("arbitrary",), vmem_limit_bytes=VMEM_LIMIT),
        name="sample_attn",
    )(page_table, qz, bias_past, bias_new, k_new, v_new, cache_k, cache_v)


def _swap_halves(w, head_dim):
    lead = w.shape[:-1]
    nh = w.shape[-1] // head_dim
    return w.reshape(lead + (nh, 2, head_dim // 2))[..., ::-1, :].reshape(lead + (nh * head_dim,))


def _projection_weights(w_in):
    sizes = (A_WIDTH, A_WIDTH, N_HEADS * HEAD_DIM, N_KV_HEADS * HEAD_DIM, N_KV_HEADS * HEAD_DIM,
             IDX_HEADS * IDX_DIM, IDX_DIM, IDX_HEADS)
    pts = [sum(sizes[:i + 1]) for i in range(len(sizes) - 1)]
    au, av, q, k, v, qi, ki, wi = jnp.split(w_in.astype(BF16), pts, axis=-1)
    rep = LANES // IDX_DIM
    q_s, k_s, qi_s, ki_s = (_swap_halves(q, HEAD_DIM), _swap_halves(k, HEAD_DIM),
                            _swap_halves(qi, IDX_DIM), _swap_halves(ki, IDX_DIM))
    ki4, ki4_s = jnp.tile(ki, (1, 1, rep)), jnp.tile(ki_s, (1, 1, rep))
    pad_cols = lambda w, n: jnp.pad(w, ((0, 0), (0, 0), (0, n - w.shape[-1])))
    ext = jnp.concatenate([au, av, q, k, v, qi, ki4, pad_cols(wi, LANES), q_s, k_s, qi_s, ki4_s], axis=-1)
    wn = jnp.concatenate([au, av], axis=-1)
    wt = jnp.swapaxes(jnp.concatenate([q, k, v, qi, ki, pad_cols(wi, WI_ROWS)], axis=-1), 1, 2)
    assert ext.shape[-1] == C_END and wn.shape[-1] == N_END and wt.shape[1] == T_END
    return ext, wn, wt


def _rope_tables(pos, head_dim):
    half = head_dim // 2
    inv = ROPE_THETA ** (-jnp.arange(half, dtype=F32) / half)
    ang = pos.astype(F32)[:, None] * inv[None, :]
    cos, sin = jnp.cos(ang), jnp.sin(ang)
    reps = LANES // head_dim
    return (jnp.tile(jnp.concatenate([cos, cos], axis=-1), (1, reps)),
            jnp.tile(jnp.concatenate([-sin, sin], axis=-1), (1, reps)))


def _pick(n, prefs):
    for p in prefs:
        if n % p == 0:
            return p
    return n


def kernel(x_prompt, x_sample, p_prompt, p_sample, cache_k, cache_v, cache_kidx, page_table, w_in, sgu_ln_g,
           sgu_ln_b, sgu_w, sgu_b, w_o, ln1_g, ln1_b, w_ff1, w_ff2, w_ple_gate, w_ple_proj, ln2_g, ln2_b):
    nb, seq, _ = x_prompt.shape
    db, nt, _ = x_sample.shape
    depth = w_in.shape[0]
    npg = page_table.shape[1]
    page = cache_k.shape[2]
    past = npg * page
    alpha = (2.0 * depth) ** 0.25
    topk_p = min(TOPK_MAX, seq // 4)
    topk_s = min(TOPK_MAX, (past + nt) // 4)
    n_p, n_s = nb * seq, db * nt
    assert n_s == CHUNK and seq % COUNT_CHUNK == 0 and page == LANES

    w_ext, w_n, w_t = _projection_weights(w_in)
    wo_b, wf1_b, wf2_b = w_o.astype(BF16), w_ff1.astype(BF16), w_ff2.astype(BF16)
    wg_b, wp_b = w_ple_gate.astype(BF16), w_ple_proj.astype(BF16)
    tril = jnp.tril(jnp.ones((CHUNK, CHUNK), F32))
    mm_p = (sgu_w * tril).astype(BF16)
    mb_p = jnp.repeat(jnp.swapaxes(sgu_b, 1, 2), A_GW, axis=2)
    r_i = jnp.arange(CHUNK)
    same_req = (r_i[:, None] // nt == r_i[None, :] // nt) & (r_i[None, :] % nt <= r_i[:, None] % nt)
    mm_s = jnp.where(same_req, sgu_w[:, :, r_i % nt][:, :, :, r_i % nt], 0.0).astype(BF16)
    mb_s = jnp.tile(jnp.repeat(jnp.swapaxes(sgu_b[:, :, :nt], 1, 2), A_GW, axis=2), (1, CHUNK // nt, 1))
    lng = sgu_ln_g.reshape(depth, 1, A_WIDTH)
    lnb = sgu_ln_b.reshape(depth, 1, A_WIDTH)
    mlp_w = (wo_b, ln1_g.reshape(depth, 1, D_MODEL), ln1_b.reshape(depth, 1, D_MODEL), wf1_b, wf2_b, wg_b, wp_b,
             ln2_g.reshape(depth, 1, D_MODEL), ln2_b.reshape(depth, 1, D_MODEL))
    pe_p = p_prompt.reshape(depth, n_p, D_PLE)
    pe_s = p_sample.reshape(depth, n_s, D_PLE)

    pos_p = jnp.arange(seq, dtype=I32)
    pos_s = past + (jnp.arange(n_s, dtype=I32) % nt)
    tabs_p = _rope_tables(pos_p, HEAD_DIM) + _rope_tables(pos_p, IDX_DIM)
    tabs_pt = tuple(t.T for t in tabs_p)
    tabs_s = _rope_tables(pos_s, HEAD_DIM) + _rope_tables(pos_s, IDX_DIM)

    cki = jnp.swapaxes(cache_kidx, 2, 3)
    ck = jnp.transpose(cache_k, (0, 1, 3, 4, 2)).reshape(cache_k.shape[:2] + (LANES, page))
    cv = jnp.transpose(cache_v, (0, 1, 3, 4, 2)).reshape(cache_v.shape[:2] + (LANES, page))

    rt_proj = _pick(seq, (1024, 512, 256, 128))
    rt_mlp = _pick(n_p, (512, 256, 128))
    qw_p = _pick(seq, (256, 128))
    assert npg % PAGE_GROUP == 0
    head_lo = (jnp.arange(N_HEADS) // KV_GROUP) * HEAD_DIM

    hp = x_prompt.reshape(n_p, D_MODEL)
    hs = x_sample.reshape(n_s, D_MODEL)
    outs = [[] for _ in range(8)]
    for i in range(depth):
        a_out, vn, qt, kb, kt, vt, qit, kib, kit, wit = _proj_prompt_call(
            hp, i, w_n, w_t, mm_p, mb_p, lng, lnb, tabs_pt, nb, seq, rt_proj)
        b_out = _prompt_attn_call(qit, wit, qt, kib, kb, vt, nb, seq, topk_p, qw_p)
        hp = _mlp_call(hp, a_out, b_out, pe_p, i, mlp_w, alpha, rt_mlp)
        to_tokens = lambda a: jnp.transpose(a.reshape(nb, N_KV_HEADS, HEAD_DIM, seq), (0, 3, 1, 2))
        outs[0].append(to_tokens(kt))
        outs[1].append(to_tokens(vt))
        outs[2].append(jnp.swapaxes(kit, 1, 2))
        outs[3].append(vn)

        a_out, vn, q, k, v, qi, ki4, wi = _proj_call(hs, i, w_ext, mm_s, mb_s, lng, lnb, tabs_s)
        qs = qi.reshape(db, nt * IDX_HEADS, IDX_DIM)
        wrep = jnp.broadcast_to((wi[:, :IDX_HEADS] * IDX_SCALE).reshape(db, nt * IDX_HEADS, 1),
                                (db, nt * IDX_HEADS, LANES))
        keys_past = _sample_idx_call(page_table, qs, wrep, cki, i)
        bias_past, bias_new = _sample_select_call(keys_past, qi, wi, ki4, topk_s, nt)
        q4 = q.reshape(db, nt, N_HEADS, 1, HEAD_DIM)
        half = (jnp.arange(N_KV_HEADS) * HEAD_DIM)[None, None, None, :, None]
        qz = jnp.where(half == head_lo[None, None, :, None, None], q4, jnp.zeros_like(q4))
        qz = qz.reshape(db, nt * N_HEADS, LANES)
        o = _sample_attn_call(page_table, qz, bias_past, bias_new.reshape(db, nt, LANES),
                              k, v, ck, cv, i, nt)
        o = o.reshape(db, nt, N_HEADS, N_KV_HEADS, HEAD_DIM)
        b_out = jnp.concatenate([o[:, :, g * KV_GROUP:(g + 1) * KV_GROUP, g, :] for g in range(N_KV_HEADS)], axis=2)
        b_out = b_out.reshape(n_s, N_HEADS * HEAD_DIM).astype(BF16)
        hs = _mlp_call(hs, a_out, b_out, pe_s, i, mlp_w, alpha, n_s)
        outs[4].append(k.reshape(db, nt, N_KV_HEADS, HEAD_DIM))
        outs[5].append(v.reshape(db, nt, N_KV_HEADS, HEAD_DIM))
        outs[6].append(ki4[:, :IDX_DIM].reshape(db, nt, IDX_DIM))
        outs[7].append(vn.reshape(db, nt, A_WIDTH))

    st = [jnp.stack(o) for o in outs]
    return (hp.reshape(nb, seq, D_MODEL), hs.reshape(db, nt, D_MODEL),
            st[0], st[1], st[2], st[3], st[4], st[5], st[6], st[7])
```
